```python
import jax, jax.numpy as jnp
from jax import lax
import numpy as np

D_MODEL = 1024
BATCH = 16
SEQ = 2048
DEPTH = 1

N_META = 16
BLOCK = 128
WINDOW = 128
PREFIX = BLOCK
N_PAD = PREFIX - N_META
HEAD_DIM = 64
A_HEADS = D_MODEL // 128
A_KV_HEADS = A_HEADS // 4
A_GROUP = A_HEADS // A_KV_HEADS
B_HEADS = D_MODEL // 128
A_WIDTH = A_HEADS * HEAD_DIM
A_KV_WIDTH = A_KV_HEADS * HEAD_DIM
B_WIDTH = B_HEADS * HEAD_DIM
W_IN_COLS = A_WIDTH + 2 * A_KV_WIDTH + 3 * B_WIDTH + B_HEADS + 2 * D_MODEL
D_FF = ((8 * D_MODEL // 3 + 127) // 128) * 128
EPS = 1e-6
NEG = -1e30

kernel_name = "hybrid_swa_sink_fox_macaron_block"


def rms_norm(x, g):
    xf = x.astype(jnp.float32)
    y = xf * lax.rsqrt(jnp.mean(xf * xf, axis=-1, keepdims=True) + EPS)
    return (y * g.astype(jnp.float32)).astype(x.dtype)


def swiglu(x, w_in, w_out):
    gu = x @ w_in
    g, u = jnp.split(gu, 2, axis=-1)
    return (jax.nn.silu(g) * u) @ w_out


def alibi_slopes(n_heads):
    return jnp.exp2(-8.0 * jnp.arange(1, n_heads + 1, dtype=jnp.float32) / n_heads)


def sliding_window_sink_attention(q, k, v, sinks):
    b, l, _, dh = q.shape
    nb = l // BLOCK
    qb = q.reshape(b, nb, BLOCK, A_KV_HEADS, A_GROUP, dh)
    kb = k.reshape(b, nb, BLOCK, A_KV_HEADS, dh)
    vb = v.reshape(b, nb, BLOCK, A_KV_HEADS, dh)
    pad_blk = ((0, 0), (1, 0), (0, 0), (0, 0), (0, 0))
    band_k = jnp.concatenate([jnp.pad(kb[:, :-1], pad_blk), kb], axis=2)
    band_v = jnp.concatenate([jnp.pad(vb[:, :-1], pad_blk), vb], axis=2)
    meta_k = jnp.broadcast_to(k[:, None, N_PAD:PREFIX], (b, nb, N_META, A_KV_HEADS, dh))
    meta_v = jnp.broadcast_to(v[:, None, N_PAD:PREFIX], (b, nb, N_META, A_KV_HEADS, dh))
    keys = jnp.concatenate([meta_k, band_k], axis=2)
    vals = jnp.concatenate([meta_v, band_v], axis=2)

    q_pos = jnp.arange(l).reshape(nb, BLOCK)
    band_pos = (jnp.arange(nb)[:, None] - 1) * BLOCK + jnp.arange(2 * BLOCK)[None, :]
    meta_pos = jnp.broadcast_to(N_PAD + jnp.arange(N_META)[None, :], (nb, N_META))
    k_pos = jnp.concatenate([meta_pos, band_pos], axis=1)
    is_band = jnp.concatenate([jnp.zeros((N_META,), bool), jnp.ones((2 * BLOCK,), bool)])
    dist = q_pos[:, :, None] - k_pos[:, None, :]
    band_ok = (dist < WINDOW) & (k_pos[:, None, :] >= PREFIX)
    allowed = (dist >= 0) & jnp.where(is_band[None, None, :], band_ok, True)

    slopes = alibi_slopes(A_HEADS).reshape(A_KV_HEADS, A_GROUP)
    s = jnp.einsum('bnqkgd,bnskd->bnkgqs', qb, keys).astype(jnp.float32) * (dh ** -0.5)
    s = s - slopes[None, None, :, :, None, None] * dist.astype(jnp.float32)[None, :, None, None, :, :]
    s = jnp.where(allowed[None, :, None, None, :, :], s, NEG)
    sink = jnp.broadcast_to(
        sinks.astype(jnp.float32).reshape(A_KV_HEADS, A_GROUP)[None, None, :, :, None, None],
        s.shape[:-1] + (1,))
    p = jax.nn.softmax(jnp.concatenate([s, sink], axis=-1), axis=-1)[..., :-1]
    o = jnp.einsum('bnkgqs,bnskd->bnqkgd', p.astype(v.dtype), vals)
    return o.reshape(b, l, A_HEADS * dh)


def forgetting_attention(q, k, v, log_f):
    b, l, h, dh = q.shape
    nb = l // BLOCK
    c = jnp.cumsum(log_f, axis=1).transpose(0, 2, 1)
    outs = []
    for i in range(nb):
        q_lo, k_hi = i * BLOCK, (i + 1) * BLOCK
        s = jnp.einsum('bqhd,bshd->bhqs', q[:, q_lo:k_hi], k[:, :k_hi]).astype(jnp.float32) * (dh ** -0.5)
        s = s + c[:, :, q_lo:k_hi, None] - c[:, :, None, :k_hi]
        q_pos = q_lo + jnp.arange(BLOCK)
        k_pos = jnp.arange(k_hi)
        allowed = (k_pos[None, :] <= q_pos[:, None]) & (k_pos[None, :] >= N_PAD)
        s = jnp.where(allowed[None, None], s, NEG)
        p = jax.nn.softmax(s, axis=-1)
        outs.append(jnp.einsum('bhqs,bshd->bqhd', p.astype(v.dtype), v[:, :k_hi]))
    return jnp.concatenate(outs, axis=1).reshape(b, l, h * dh)


def setup_inputs(seed: int = 0) -> dict:
    key = jax.random.key(seed)
    ks = jax.random.split(key, 20)
    f32 = jnp.float32
    n = lambda k, shape, scale: jax.random.normal(k, shape, f32) * scale
    return {
        "x": n(ks[0], (BATCH, SEQ, D_MODEL), 1.0),
        "meta_tokens": n(ks[1], (N_META, D_MODEL), 1.0),
        "ffn1_norm": 1.0 + n(ks[2], (DEPTH, D_MODEL), 0.02),
        "ffn1_w_in": n(ks[3], (DEPTH, D_MODEL, 2 * D_FF), D_MODEL ** -0.5),
        "ffn1_w_out": n(ks[4], (DEPTH, D_FF, D_MODEL), D_FF ** -0.5),
        "mix_norm": 1.0 + n(ks[5], (DEPTH, D_MODEL), 0.02),
        "w_in": n(ks[6], (DEPTH, D_MODEL, W_IN_COLS), D_MODEL ** -0.5),
        "b_forget": 2.0 + n(ks[7], (DEPTH, B_HEADS), 0.1),
        "attn_sinks": n(ks[8], (DEPTH, A_HEADS), 0.5),
        "w_branch_a": n(ks[9], (DEPTH, A_WIDTH, D_MODEL), A_WIDTH ** -0.5),
        "w_branch_b": n(ks[10], (DEPTH, B_WIDTH, D_MODEL), B_WIDTH ** -0.5),
        "w_out": n(ks[11], (DEPTH, D_MODEL, D_MODEL), D_MODEL ** -0.5),
        "ffn2_norm": 1.0 + n(ks[12], (DEPTH, D_MODEL), 0.02),
        "ffn2_w_in": n(ks[13], (DEPTH, D_MODEL, 2 * D_FF), D_MODEL ** -0.5),
        "ffn2_w_out": n(ks[14], (DEPTH, D_FF, D_MODEL), D_FF ** -0.5),
        "final_norm": 1.0 + n(ks[15], (D_MODEL,), 0.02),
    }


def reference(x, meta_tokens, ffn1_norm, ffn1_w_in, ffn1_w_out, mix_norm, w_in, b_forget,
              attn_sinks, w_branch_a, w_branch_b, w_out, ffn2_norm, ffn2_w_in, ffn2_w_out,
              final_norm):
    b = x.shape[0]
    pads = jnp.zeros((b, N_PAD, D_MODEL), x.dtype)
    meta = jnp.broadcast_to(meta_tokens.astype(x.dtype)[None], (b, N_META, D_MODEL))
    h = jnp.concatenate([pads, meta, x], axis=1)
    l = h.shape[1]

    sizes = [A_WIDTH, A_KV_WIDTH, A_KV_WIDTH, B_WIDTH, B_WIDTH, B_WIDTH, B_HEADS, D_MODEL, D_MODEL]
    offsets = []
    acc = 0
    for sz in sizes[:-1]:
        acc += sz
        offsets.append(acc)

    for i in range(DEPTH):
        h = h + 0.5 * swiglu(rms_norm(h, ffn1_norm[i]), ffn1_w_in[i], ffn1_w_out[i])

        u = rms_norm(h, mix_norm[i])
        proj = u @ w_in[i]
        qa, ka, va, qb, kb, vb, f_logit, g_a, g_b = jnp.split(proj, offsets, axis=-1)
        qa = qa.reshape(b, l, A_HEADS, HEAD_DIM)
        ka = ka.reshape(b, l, A_KV_HEADS, HEAD_DIM)
        va = va.reshape(b, l, A_KV_HEADS, HEAD_DIM)
        qb = qb.reshape(b, l, B_HEADS, HEAD_DIM)
        kb = kb.reshape(b, l, B_HEADS, HEAD_DIM)
        vb = vb.reshape(b, l, B_HEADS, HEAD_DIM)
        log_f = jax.nn.log_sigmoid((f_logit + b_forget[i]).astype(jnp.float32))

        y_a = sliding_window_sink_attention(qa, ka, va, attn_sinks[i]) @ w_branch_a[i]
        y_b = forgetting_attention(qb, kb, vb, log_f) @ w_branch_b[i]
        mixed = jax.nn.sigmoid(g_a) * y_a + jax.nn.sigmoid(g_b) * y_b
        h = h + mixed @ w_out[i]

        h = h + 0.5 * swiglu(rms_norm(h, ffn2_norm[i]), ffn2_w_in[i], ffn2_w_out[i])

    return rms_norm(h, final_norm)[:, PREFIX:]
```

```python
import functools

import jax
import jax.numpy as jnp
import numpy as np
from jax import lax
from jax.experimental import pallas as pl
from jax.experimental.pallas import tpu as pltpu

F32 = jnp.float32
BF16 = jnp.bfloat16

D_MODEL = 1024
N_META = 16
BLOCK = 128
HEAD_DIM = 64
A_HEADS = 8
A_KV_HEADS = 2
A_GROUP = A_HEADS // A_KV_HEADS
B_HEADS = 8
A_WIDTH = A_HEADS * HEAD_DIM
B_WIDTH = B_HEADS * HEAD_DIM
D_FF = 2816
EPS = 1e-6
NEG = -1e30
Q_SCALE = HEAD_DIM ** -0.5

LANES = 128
TOKEN_TILE = 512
Q_TILE = 256
KV_TILE = 256
FF_CHUNKS = ((0, 1024), (1024, 1024), (2048, 768))
VMEM_LIMIT_BYTES = 56 * 1024 * 1024

_QA, _KA, _VA, _QB, _KB, _VB, _GA, _GB = 0, 512, 768, 1024, 1536, 2048, 2560, 3584
PROJ_COLS = 4608


def _dot(a, b):
    return jnp.dot(a, b, preferred_element_type=F32)


def _dot_nt(a, b):
    return lax.dot_general(a, b, (((1,), (1,)), ((), ())), preferred_element_type=F32)


def _rms(x, g):
    ms = jnp.mean(x * x, axis=-1, keepdims=True)
    return x * lax.rsqrt(ms + EPS) * g


def _swiglu(xn, w_in_ref, w_out_ref):
    acc = None
    for lo, width in FF_CHUNKS:
        g = _dot(xn, w_in_ref[:, lo:lo + width])
        u = _dot(xn, w_in_ref[:, D_FF + lo:D_FF + lo + width])
        a = (g * jax.nn.sigmoid(g) * u).astype(BF16)
        part = _dot(a, w_out_ref[lo:lo + width, :])
        acc = part if acc is None else acc + part
    return acc


def _ffn_proj_kernel(x_ref, n1_ref, w1i_ref, w1o_ref, nm_ref, wp_ref, wf_ref, bf_ref,
                     h1_ref, qa_ref, ka_ref, va_ref, qb_ref, kb_ref, vb_ref, ga_ref, gb_ref, lf_ref):
    x = x_ref[...]
    xn = _rms(x, n1_ref[...]).astype(BF16)
    h1 = x + 0.5 * _swiglu(xn, w1i_ref, w1o_ref)
    h1_ref[...] = h1
    u = _rms(h1, nm_ref[...]).astype(BF16)

    def proj(lo, width):
        return _dot(u, wp_ref[:, lo:lo + width])

    qa_ref[...] = (proj(_QA, 512) * Q_SCALE).astype(BF16)
    ka_ref[...] = proj(_KA, 256).astype(BF16)
    va_ref[...] = proj(_VA, 256).astype(BF16)
    qb_ref[...] = (proj(_QB, 512) * Q_SCALE).astype(BF16)
    kb_ref[...] = proj(_KB, 512).astype(BF16)
    vb_ref[...] = proj(_VB, 512).astype(BF16)
    ga_ref[...] = jax.nn.sigmoid(proj(_GA, 1024)).astype(BF16)
    gb_ref[...] = jax.nn.sigmoid(proj(_GB, 1024)).astype(BF16)
    z = _dot_nt(wf_ref[...], u)[:B_HEADS] + bf_ref[...]
    lf_ref[...] = jnp.minimum(z, 0.0) - jnp.log1p(jnp.exp(-jnp.abs(z)))


def _const_spec(shape):
    zeros = (0,) * len(shape)
    return pl.BlockSpec(shape, lambda *_: zeros, pipeline_mode=pl.Buffered(1))


def _ffn_proj(x2d, n1, w1i, w1o, nm, wp, wf, bfg, tile):
    rows = x2d.shape[0]
    grid = (rows // tile,)

    def row_spec(cols):
        return pl.BlockSpec((tile, cols), lambda i: (i, 0))

    out_cols = (D_MODEL, 512, 256, 256, 512, 512, 512, 1024, 1024)
    out_dtypes = (F32,) + (BF16,) * 8
    out_shape = [jax.ShapeDtypeStruct((rows, c), dt) for c, dt in zip(out_cols, out_dtypes)]
    out_shape.append(jax.ShapeDtypeStruct((B_HEADS, rows), F32))
    out_specs = [row_spec(c) for c in out_cols] + [pl.BlockSpec((B_HEADS, tile), lambda i: (0, i))]
    in_specs = [row_spec(D_MODEL), _const_spec(n1.shape), _const_spec(w1i.shape), _const_spec(w1o.shape),
                _const_spec(nm.shape), _const_spec(wp.shape), _const_spec(wf.shape), _const_spec(bfg.shape)]
    return pl.pallas_call(
        _ffn_proj_kernel,
        grid=grid,
        in_specs=in_specs,
        out_specs=out_specs,
        out_shape=out_shape,
        compiler_params=pltpu.CompilerParams(
            dimension_semantics=("arbitrary",), vmem_limit_bytes=VMEM_LIMIT_BYTES),
        name="ffn1_proj",
    )(x2d, n1, w1i, w1o, nm, wp, wf, bfg)


def _lane_scan(v, lane):
    shift = 1
    while shift < LANES:
        v = v + jnp.where(lane >= shift, pltpu.roll(v, shift, 1), 0.0)
        shift *= 2
    return v


def _split_heads(q, low):
    zero = jnp.zeros_like(q)
    return jnp.concatenate([jnp.where(low, q, zero), jnp.where(low, zero, q)], axis=0)


def _attn_kernel(sinks_ref, qa_ref, ka_ref, va_ref, kam_ref, vam_ref,
                 qb_ref, kb_ref, vb_ref, kbm_ref, vbm_ref, lf_ref, lfm_ref, band_ref, mtab_ref,
                 oa_ref, ob_ref, c_ref, cm_ref, m_ref, l_ref, acc_ref):
    i = pl.program_id(1)
    low = lax.broadcasted_iota(jnp.int32, (1, LANES), 1) < HEAD_DIM

    @pl.when(i == 0)
    def _():
        lane = lax.broadcasted_iota(jnp.int32, (B_HEADS, LANES), 1)
        cm = _lane_scan(lfm_ref[...], lane)
        cm_ref[...] = jnp.where(lane < N_META, -cm, NEG)
        carry = cm[:, LANES - 1:LANES]
        for blk in range(lf_ref.shape[1] // LANES):
            v = _lane_scan(lf_ref[:, blk * LANES:(blk + 1) * LANES], lane) + carry
            per = KV_TILE // LANES
            c_ref[blk // per, :, (blk % per) * LANES:(blk % per + 1) * LANES] = -v
            carry = v[:, LANES - 1:LANES]

    for sb in range(Q_TILE // BLOCK):
        n = i * (Q_TILE // BLOCK) + sb
        nf = n.astype(F32)
        variant = jnp.minimum(n, 1)
        start = pl.multiple_of(jnp.maximum(n - 1, 0) * BLOCK, BLOCK)
        q = qa_ref[sb * BLOCK:(sb + 1) * BLOCK, :]
        for g in range(A_KV_HEADS):
            heads = [g * A_GROUP + j for j in range(A_GROUP)]
            qq = jnp.concatenate(
                [_split_heads(q[:, (2 * g + p) * LANES:(2 * g + p + 1) * LANES], low) for p in range(2)], axis=0)
            kband = ka_ref[pl.ds(start, 2 * BLOCK), g * LANES:(g + 1) * LANES]
            vband = va_ref[pl.ds(start, 2 * BLOCK), g * LANES:(g + 1) * LANES]
            kmeta = kam_ref[:, g * LANES:(g + 1) * LANES]
            vmeta = vam_ref[:, g * LANES:(g + 1) * LANES]
            s_band = _dot_nt(qq, kband) + band_ref[variant, g]
            s_meta = _dot_nt(qq, kmeta) + mtab_ref[g]
            s_meta = jnp.concatenate(
                [s_meta[j * BLOCK:(j + 1) * BLOCK] - (2.0 ** -(h + 1) * BLOCK) * nf for j, h in enumerate(heads)],
                axis=0)
            sink = jnp.concatenate([jnp.full((BLOCK, 1), sinks_ref[h], F32) for h in heads], axis=0)
            m = jnp.maximum(jnp.max(s_band, axis=-1, keepdims=True), jnp.max(s_meta, axis=-1, keepdims=True))
            m = jnp.maximum(m, sink)
            p_band = jnp.exp(s_band - m)
            p_meta = jnp.exp(s_meta - m)
            denom = (jnp.sum(p_band, axis=-1, keepdims=True) + jnp.sum(p_meta, axis=-1, keepdims=True)
                     + jnp.exp(sink - m))
            o = (_dot(p_band.astype(BF16), vband) + _dot(p_meta.astype(BF16), vmeta)) / denom
            for p in range(2):
                pair = jnp.where(low, o[(2 * p) * BLOCK:(2 * p + 1) * BLOCK], o[(2 * p + 1) * BLOCK:(2 * p + 2) * BLOCK])
                oa_ref[sb * BLOCK:(sb + 1) * BLOCK, (2 * g + p) * LANES:(2 * g + p + 1) * LANES] = pair.astype(BF16)

    row = lax.broadcasted_iota(jnp.int32, (Q_TILE, KV_TILE), 0)
    col = lax.broadcasted_iota(jnp.int32, (Q_TILE, KV_TILE), 1)
    causal = jnp.where(col <= row, 0.0, NEG)

    def add_rows(s, r0, r1):
        return jnp.concatenate([s[:Q_TILE] + r0, s[Q_TILE:] + r1], axis=0)

    for pr in range(B_HEADS // 2):
        h0, h1 = 2 * pr, 2 * pr + 1
        cols = slice(pr * LANES, (pr + 1) * LANES)
        qq = _split_heads(qb_ref[:, cols], low)

        s = add_rows(_dot_nt(qq, kbm_ref[:, cols]), cm_ref[h0:h0 + 1, :], cm_ref[h1:h1 + 1, :])
        m0 = jnp.max(s, axis=-1, keepdims=True)
        p = jnp.exp(s - m0)
        m_ref[...] = m0
        l_ref[...] = jnp.sum(p, axis=-1, keepdims=True)
        acc_ref[...] = _dot(p.astype(BF16), vbm_ref[:, cols])

        def update(j, extra):
            k0 = pl.multiple_of(j * KV_TILE, KV_TILE)
            cj = c_ref[j]
            s = add_rows(_dot_nt(qq, kb_ref[pl.ds(k0, KV_TILE), cols]), cj[h0:h0 + 1, :], cj[h1:h1 + 1, :])
            if extra is not None:
                s = add_rows(s, extra, extra)
            m_old = m_ref[...]
            m_new = jnp.maximum(m_old, jnp.max(s, axis=-1, keepdims=True))
            alpha = jnp.exp(m_old - m_new)
            p = jnp.exp(s - m_new)
            l_ref[...] = alpha * l_ref[...] + jnp.sum(p, axis=-1, keepdims=True)
            acc_ref[...] = alpha * acc_ref[...] + _dot(p.astype(BF16), vb_ref[pl.ds(k0, KV_TILE), cols])
            m_ref[...] = m_new

        def body(j, carry):
            update(j, None)
            return carry

        lax.fori_loop(0, i, body, 0)
        update(i, causal)

        o = acc_ref[...] / l_ref[...]
        ob_ref[:, cols] = jnp.where(low, o[:Q_TILE], o[Q_TILE:]).astype(BF16)


def _attention(sinks, qa, ka, va, kam, vam, qb, kb, vb, kbm, vbm, lf, lfm, band_tab, meta_tab, batch, seq):
    nq = seq // Q_TILE
    grid = (batch, nq)

    def q_spec(cols):
        return pl.BlockSpec((Q_TILE, cols), lambda b, i: (b * nq + i, 0))

    def kv_spec(cols):
        return pl.BlockSpec((seq, cols), lambda b, i: (b, 0))

    in_specs = [
        pl.BlockSpec(memory_space=pltpu.SMEM),
        q_spec(A_WIDTH), kv_spec(256), kv_spec(256), _const_spec(kam.shape), _const_spec(vam.shape),
        q_spec(B_WIDTH), kv_spec(B_WIDTH), kv_spec(B_WIDTH), _const_spec(kbm.shape), _const_spec(vbm.shape),
        pl.BlockSpec((B_HEADS, seq), lambda b, i: (0, b)), _const_spec(lfm.shape),
        _const_spec(band_tab.shape), _const_spec(meta_tab.shape),
    ]
    out_shape = [jax.ShapeDtypeStruct((batch * seq, A_WIDTH), BF16),
                 jax.ShapeDtypeStruct((batch * seq, B_WIDTH), BF16)]
    out_specs = [q_spec(A_WIDTH), q_spec(B_WIDTH)]
    scratch = [
        pltpu.VMEM((seq // KV_TILE, B_HEADS, KV_TILE), F32),
        pltpu.VMEM((B_HEADS, LANES), F32),
        pltpu.VMEM((2 * Q_TILE, 1), F32),
        pltpu.VMEM((2 * Q_TILE, 1), F32),
        pltpu.VMEM((2 * Q_TILE, LANES), F32),
    ]
    return pl.pallas_call(
        _attn_kernel,
        grid=grid,
        in_specs=in_specs,
        out_specs=out_specs,
        out_shape=out_shape,
        scratch_shapes=scratch,
        compiler_params=pltpu.CompilerParams(
            dimension_semantics=("arbitrary", "arbitrary"), vmem_limit_bytes=VMEM_LIMIT_BYTES),
        name="attention",
    )(sinks, qa, ka, va, kam, vam, qb, kb, vb, kbm, vbm, lf, lfm, band_tab, meta_tab)


def _mix_ffn_kernel(h1_ref, oa_ref, ob_ref, ga_ref, gb_ref, wa_ref, wb_ref, wo_ref,
                    n2_ref, w2i_ref, w2o_ref, nf_ref, out_ref):
    ya = _dot(oa_ref[...], wa_ref[...])
    yb = _dot(ob_ref[...], wb_ref[...])
    mixed = (ga_ref[...].astype(F32) * ya + gb_ref[...].astype(F32) * yb).astype(BF16)
    h2 = h1_ref[...] + _dot(mixed, wo_ref[...])
    xn = _rms(h2, n2_ref[...]).astype(BF16)
    h3 = h2 + 0.5 * _swiglu(xn, w2i_ref, w2o_ref)
    out_ref[...] = _rms(h3, nf_ref[...])


def _mix_ffn(h1, oa, ob, ga, gb, wa, wb, wo, n2, w2i, w2o, nf, tile):
    rows = h1.shape[0]

    def row_spec(cols):
        return pl.BlockSpec((tile, cols), lambda i: (i, 0))

    in_specs = [row_spec(D_MODEL), row_spec(A_WIDTH), row_spec(B_WIDTH), row_spec(D_MODEL), row_spec(D_MODEL)]
    in_specs += [_const_spec(w.shape) for w in (wa, wb, wo, n2, w2i, w2o, nf)]
    return pl.pallas_call(
        _mix_ffn_kernel,
        grid=(rows // tile,),
        in_specs=in_specs,
        out_specs=row_spec(D_MODEL),
        out_shape=jax.ShapeDtypeStruct((rows, D_MODEL), F32),
        compiler_params=pltpu.CompilerParams(
            dimension_semantics=("arbitrary",), vmem_limit_bytes=VMEM_LIMIT_BYTES),
        name="mix_ffn2",
    )(h1, oa, ob, ga, gb, wa, wb, wo, n2, w2i, w2o, nf)


def _prep_proj_weight(w):
    qa = w[:, 0:512]
    ka = w[:, 512:640]
    va = w[:, 640:768]
    qb, kb, vb = w[:, 768:1280], w[:, 1280:1792], w[:, 1792:2304]
    wf = w[:, 2304:2312]
    g_a, g_b = w[:, 2312:3336], w[:, 3336:4360]

    def dup(kv):
        k0, k1 = kv[:, :HEAD_DIM], kv[:, HEAD_DIM:]
        return jnp.concatenate([k0, k0, k1, k1], axis=1)

    wp = jnp.concatenate([qa, dup(ka), dup(va), qb, kb, vb, g_a, g_b], axis=1).astype(BF16)
    wf_t = jnp.pad(wf.T, ((0, 16 - B_HEADS), (0, 0))).astype(BF16)
    return wp, wf_t


def _alibi_tables():
    slopes = 2.0 ** -(np.arange(1, A_HEADS + 1, dtype=np.float64))
    r = np.arange(BLOCK)[:, None]
    c = np.arange(BLOCK)[None, :]
    band = np.empty((2, A_KV_HEADS, A_GROUP * BLOCK, 2 * BLOCK), np.float32)
    meta = np.empty((A_KV_HEADS, A_GROUP * BLOCK, LANES), np.float32)
    mcol = np.arange(LANES)[None, :]
    for h in range(A_HEADS):
        g, j = divmod(h, A_GROUP)
        rows = slice(j * BLOCK, (j + 1) * BLOCK)
        cur = np.where(c <= r, -slopes[h] * (r - c), NEG)
        prev = np.where(c > r, -slopes[h] * (r - c + BLOCK), NEG)
        band[0, g, rows] = np.concatenate([cur, np.full_like(cur, NEG)], axis=1)
        band[1, g, rows] = np.concatenate([prev, cur], axis=1)
        meta[g, rows] = np.where(mcol < N_META, -slopes[h] * (r + N_META - mcol), NEG)
    return jnp.asarray(band), jnp.asarray(meta)


def kernel(x, meta_tokens, ffn1_norm, ffn1_w_in, ffn1_w_out, mix_norm, w_in, b_forget, attn_sinks,
           w_branch_a, w_branch_b, w_out, ffn2_norm, ffn2_w_in, ffn2_w_out, final_norm):
    batch, seq, d = x.shape
    assert d == D_MODEL and seq % Q_TILE == 0 and (batch * seq) % TOKEN_TILE == 0
    assert ffn1_norm.shape[0] == 1, "single layer"

    n1 = ffn1_norm[0][None].astype(F32)
    nm = mix_norm[0][None].astype(F32)
    n2 = ffn2_norm[0][None].astype(F32)
    nf = final_norm[None].astype(F32)
    w1i, w1o = ffn1_w_in[0].astype(BF16), ffn1_w_out[0].astype(BF16)
    w2i, w2o = ffn2_w_in[0].astype(BF16), ffn2_w_out[0].astype(BF16)
    wp, wf_t = _prep_proj_weight(w_in[0])
    bfg = b_forget[0].astype(F32)[:, None]
    wa, wb, wo = w_branch_a[0].astype(BF16), w_branch_b[0].astype(BF16), w_out[0].astype(BF16)

    x2d = x.reshape(batch * seq, D_MODEL)
    h1, qa, ka, va, qb, kb, vb, ga, gb, lf = _ffn_proj(x2d, n1, w1i, w1o, nm, wp, wf_t, bfg, TOKEN_TILE)
    _, _, kam, vam, _, kbm, vbm, _, _, lfm = _ffn_proj(
        meta_tokens.astype(F32), n1, w1i, w1o, nm, wp, wf_t, bfg, N_META)

    pad_rows = ((0, BLOCK - N_META), (0, 0))
    kam, vam, kbm, vbm = (jnp.pad(a, pad_rows) for a in (kam, vam, kbm, vbm))
    lfm = jnp.pad(lfm, ((0, 0), (0, LANES - N_META)))
    band_tab, meta_tab = _alibi_tables()

    oa, ob = _attention(attn_sinks[0].astype(F32), qa, ka, va, kam, vam, qb, kb, vb, kbm, vbm,
                        lf, lfm, band_tab, meta_tab, batch, seq)
    out = _mix_ffn(h1, oa, ob, ga, gb, wa, wb, wo, n2, w2i, w2o, nf, TOKEN_TILE)
    return out.reshape(batch, seq, D_MODEL)
```

```python
import jax
import jax.numpy as jnp
import numpy as np
from jax import lax
from jax.experimental import pallas as pl
from jax.experimental.pallas import tpu as pltpu

F32 = jnp.float32
BF16 = jnp.bfloat16

D_MODEL = 1024
N_META = 16
BLOCK = 128
HEAD_DIM = 64
A_HEADS = 8
A_KV_HEADS = 2
A_GROUP = A_HEADS // A_KV_HEADS
B_HEADS = 8
A_WIDTH = A_HEADS * HEAD_DIM
B_WIDTH = B_HEADS * HEAD_DIM
V_ROWS = B_WIDTH + A_KV_HEADS * HEAD_DIM
D_FF = 2816
EPS = 1e-6
NEG = -1e30
Q_SCALE = HEAD_DIM ** -0.5

LANES = 128
TOKEN_TILE = 512
Q_TILE = 256
KV_TILE = 256
FF_CHUNKS = ((0, 1024), (1024, 1024), (2048, 768))
VMEM_LIMIT_BYTES = 56 * 1024 * 1024
BIAS_PIECES = 3

_QA, _KA, _QB, _KB, _GA, _GB, _FG = 0, 512, 768, 1280, 1792, 2816, 3840
PROJ_COLS = 3968


def _dot(a, b):
    return jnp.dot(a, b, preferred_element_type=F32)


def _dot_nt(a, b):
    return lax.dot_general(a, b, (((1,), (1,)), ((), ())), preferred_element_type=F32)


def _rms(x, g):
    ms = jnp.mean(x * x, axis=-1, keepdims=True)
    return x * lax.rsqrt(ms + EPS) * g


def _swiglu(xn, w_in_ref, w_out_ref):
    acc = None
    for lo, width in FF_CHUNKS:
        g = _dot(xn, w_in_ref[:, lo:lo + width])
        u = _dot(xn, w_in_ref[:, D_FF + lo:D_FF + lo + width])
        a = (g * jax.nn.sigmoid(g) * u).astype(BF16)
        part = _dot(a, w_out_ref[lo:lo + width, :])
        acc = part if acc is None else acc + part
    return acc


def _split3(x):
    hi = x.astype(BF16)
    r = x - hi.astype(F32)
    mid = r.astype(BF16)
    lo = (r - mid.astype(F32)).astype(BF16)
    return hi, mid, lo


def _ffn_proj_kernel(tiles_per_seq, is_meta,
                     x_ref, n1_ref, w1i_ref, w1o_ref, nm_ref, wp_ref, wv_ref, bf_ref, tri_ref, place_ref,
                     h1_ref, qa_ref, ka_ref, qb_ref, kb_ref, ga_ref, gb_ref, cb_ref, v_ref, carry_ref):
    x = x_ref[...]
    tile = x.shape[0]
    xn = _rms(x, n1_ref[...]).astype(BF16)
    h1 = x + 0.5 * _swiglu(xn, w1i_ref, w1o_ref)
    h1_ref[...] = h1
    u = _rms(h1, nm_ref[...]).astype(BF16)

    def proj(lo, width):
        return _dot(u, wp_ref[:, lo:lo + width])

    qa_ref[...] = (proj(_QA, 512) * Q_SCALE).astype(BF16)
    ka_ref[...] = proj(_KA, 256).astype(BF16)
    qb_ref[...] = (proj(_QB, 512) * Q_SCALE).astype(BF16)
    kb_ref[...] = proj(_KB, 512).astype(BF16)
    ga_ref[...] = jax.nn.sigmoid(proj(_GA, 1024)).astype(BF16)
    gb_ref[...] = jax.nn.sigmoid(proj(_GB, 1024)).astype(BF16)

    if is_meta:
        v_ref[...] = _dot(u, wv_ref[...]).astype(BF16)
    else:
        vt = _dot_nt(wv_ref[...], u).astype(BF16)
        for c in range(tile // LANES):
            v_ref[c] = vt[:, c * LANES:(c + 1) * LANES]

    z = proj(_FG, LANES) + bf_ref[...]
    lane = lax.broadcasted_iota(jnp.int32, (1, LANES), 1)
    lf = jnp.where(lane < B_HEADS, jnp.minimum(z, 0.0) - jnp.log1p(jnp.exp(-jnp.abs(z))), 0.0)

    @pl.when(pl.program_id(0) % tiles_per_seq == 0)
    def _():
        carry_ref[...] = jnp.zeros_like(carry_ref)

    tri = tri_ref[...]
    c = carry_ref[...] + sum(_dot(tri, piece) for piece in _split3(lf))
    carry_ref[...] = c[tile - 1:tile, :]
    bias = (c[tile - 1:tile, :] - c) if is_meta else -c
    cb_ref[...] = sum(_dot(piece, place_ref[p]) for p, piece in enumerate(_split3(bias))).astype(BF16)


def _const_spec(shape):
    zeros = (0,) * len(shape)
    return pl.BlockSpec(shape, lambda *_: zeros, pipeline_mode=pl.Buffered(1))


def _ffn_proj(x2d, n1, w1i, w1o, nm, wp, wv, bfg, place, tile, tiles_per_seq, is_meta):
    rows = x2d.shape[0]
    grid = (rows // tile,)
    tri = jnp.asarray(np.tril(np.ones((tile, tile), np.float32)), BF16)

    def row_spec(cols):
        return pl.BlockSpec((tile, cols), lambda i: (i, 0))

    out_cols = (D_MODEL, 512, 256, 512, 512, 1024, 1024, LANES)
    out_dtypes = (F32,) + (BF16,) * 7
    out_shape = [jax.ShapeDtypeStruct((rows, c), dt) for c, dt in zip(out_cols, out_dtypes)]
    out_specs = [row_spec(c) for c in out_cols]
    if is_meta:
        out_shape.append(jax.ShapeDtypeStruct((rows, V_ROWS), BF16))
        out_specs.append(row_spec(V_ROWS))
    else:
        out_shape.append(jax.ShapeDtypeStruct((rows // LANES, V_ROWS, LANES), BF16))
        out_specs.append(pl.BlockSpec((tile // LANES, V_ROWS, LANES), lambda i: (i, 0, 0)))
    consts = (n1, w1i, w1o, nm, wp, wv, bfg, tri, place)
    in_specs = [row_spec(D_MODEL)] + [_const_spec(a.shape) for a in consts]
    return pl.pallas_call(
        lambda *refs: _ffn_proj_kernel(tiles_per_seq, is_meta, *refs),
        grid=grid,
        in_specs=in_specs,
        out_specs=out_specs,
        out_shape=out_shape,
        scratch_shapes=[pltpu.VMEM((1, LANES), F32)],
        compiler_params=pltpu.CompilerParams(
            dimension_semantics=("arbitrary",), vmem_limit_bytes=VMEM_LIMIT_BYTES),
        name="ffn1_proj_meta" if is_meta else "ffn1_proj",
    )(x2d, *consts)


def _split_heads(q, low):
    zero = jnp.zeros_like(q)
    return jnp.concatenate([jnp.where(low, q, zero), jnp.where(low, zero, q)], axis=0)


def _attn_kernel(sinks_ref, qa_ref, ka_ref, qb_ref, kb_ref, cb_ref, vt_ref,
                 kam_ref, kbm_ref, cbm_ref, vtm_ref, band_ref, mtab_ref, slope_ref, qext_ref,
                 oa_ref, ob_ref, m_ref, l_ref, acc_ref):
    i = pl.program_id(1)
    low = lax.broadcasted_iota(jnp.int32, (1, LANES), 1) < HEAD_DIM
    sub = Q_TILE // BLOCK

    def v_tile(blk, row0):
        return jnp.concatenate([vt_ref[blk, row0:row0 + HEAD_DIM, :], vt_ref[blk + 1, row0:row0 + HEAD_DIM, :]], axis=1)

    for sb in range(sub):
        n = i * sub + sb
        nf = n.astype(F32)
        variant = jnp.minimum(n, 1)
        blk0 = jnp.maximum(n - 1, 0)
        start = pl.multiple_of(blk0 * BLOCK, BLOCK)
        q = qa_ref[sb * BLOCK:(sb + 1) * BLOCK, :]
        for g in range(A_KV_HEADS):
            heads = [g * A_GROUP + j for j in range(A_GROUP)]
            qq = jnp.concatenate(
                [_split_heads(q[:, (2 * g + p) * LANES:(2 * g + p + 1) * LANES], low) for p in range(2)], axis=0)
            kband = ka_ref[pl.ds(start, 2 * BLOCK), g * LANES:(g + 1) * LANES]
            kmeta = kam_ref[:, g * LANES:(g + 1) * LANES]
            s_b = _dot_nt(kband, qq) + band_ref[variant, g]
            s_m = _dot_nt(kmeta, qq) + mtab_ref[g] - nf * slope_ref[g]
            sink = jnp.concatenate([jnp.full((1, BLOCK), sinks_ref[h], F32) for h in heads], axis=1)
            m = jnp.maximum(jnp.max(s_b, axis=0, keepdims=True), jnp.max(s_m, axis=0, keepdims=True))
            m = jnp.maximum(m, sink)
            p_b = jnp.exp(s_b - m)
            p_m = jnp.exp(s_m - m)
            denom = jnp.sum(p_b, axis=0, keepdims=True) + jnp.sum(p_m, axis=0, keepdims=True) + jnp.exp(sink - m)
            row0 = B_WIDTH + g * HEAD_DIM
            o_t = (_dot(v_tile(blk0, row0), p_b.astype(BF16))
                   + _dot(vtm_ref[row0:row0 + HEAD_DIM, :], p_m.astype(BF16))) / denom
            for p in range(2):
                pair_t = jnp.concatenate([o_t[:, (2 * p) * BLOCK:(2 * p + 1) * BLOCK],
                                          o_t[:, (2 * p + 1) * BLOCK:(2 * p + 2) * BLOCK]], axis=0)
                oa_ref[sb * BLOCK:(sb + 1) * BLOCK, (2 * g + p) * LANES:(2 * g + p + 1) * LANES] = pair_t.T.astype(BF16)

    n_pairs = B_HEADS // 2
    q_aug = []
    for pr in range(n_pairs):
        qq = _split_heads(qb_ref[:, pr * LANES:(pr + 1) * LANES], low)
        q_aug.append(jnp.concatenate([qq, qext_ref[pr]], axis=1))

    row = lax.broadcasted_iota(jnp.int32, (KV_TILE, Q_TILE), 0)
    col = lax.broadcasted_iota(jnp.int32, (KV_TILE, Q_TILE), 1)
    causal1 = jnp.where(row <= col, 0.0, NEG)
    causal = jnp.concatenate([causal1, causal1], axis=1)

    for pr in range(n_pairs):
        cols = slice(pr * LANES, (pr + 1) * LANES)
        k_aug = jnp.concatenate([kbm_ref[:, cols], cbm_ref[...]], axis=1)
        s = _dot_nt(k_aug, q_aug[pr])
        m0 = jnp.max(s, axis=0, keepdims=True)
        p = jnp.exp(s - m0)
        m_ref[pr] = m0
        l_ref[pr] = jnp.sum(p, axis=0, keepdims=True)
        pb = p.astype(BF16)
        for hh in range(2):
            h = 2 * pr + hh
            acc_ref[h] = _dot(vtm_ref[h * HEAD_DIM:(h + 1) * HEAD_DIM, :], pb[:, hh * Q_TILE:(hh + 1) * Q_TILE])

    def update(j, mask):
        k0 = pl.multiple_of(j * KV_TILE, KV_TILE)
        cb = cb_ref[pl.ds(k0, KV_TILE), :]
        for pr in range(n_pairs):
            k_aug = jnp.concatenate([kb_ref[pl.ds(k0, KV_TILE), pr * LANES:(pr + 1) * LANES], cb], axis=1)
            s = _dot_nt(k_aug, q_aug[pr])
            if mask is not None:
                s = s + mask
            m_old = m_ref[pr]
            m_new = jnp.maximum(m_old, jnp.max(s, axis=0, keepdims=True))
            alpha = jnp.exp(m_old - m_new)
            p = jnp.exp(s - m_new)
            l_ref[pr] = alpha * l_ref[pr] + jnp.sum(p, axis=0, keepdims=True)
            m_ref[pr] = m_new
            pb = p.astype(BF16)
            for hh in range(2):
                h = 2 * pr + hh
                qs = slice(hh * Q_TILE, (hh + 1) * Q_TILE)
                acc_ref[h] = alpha[:, qs] * acc_ref[h] + _dot(v_tile(2 * j, h * HEAD_DIM), pb[:, qs])

    def body(j, carry):
        update(j, None)
        return carry

    lax.fori_loop(0, i, body, 0)
    update(i, causal)

    for pr in range(n_pairs):
        l = l_ref[pr]
        o_t = jnp.concatenate([acc_ref[2 * pr] / l[:, :Q_TILE], acc_ref[2 * pr + 1] / l[:, Q_TILE:]], axis=0)
        ob_ref[:, pr * LANES:(pr + 1) * LANES] = o_t.T.astype(BF16)


def _attention(sinks, qa, ka, qb, kb, cb, vt, kam, kbm, cbm, vtm, tables, batch, seq):
    nq = seq // Q_TILE
    grid = (batch, nq)

    def q_spec(cols):
        return pl.BlockSpec((Q_TILE, cols), lambda b, i: (b * nq + i, 0))

    def kv_spec(cols):
        return pl.BlockSpec((seq, cols), lambda b, i: (b, 0))

    in_specs = [
        pl.BlockSpec(memory_space=pltpu.SMEM),
        q_spec(A_WIDTH), kv_spec(256), q_spec(B_WIDTH), kv_spec(B_WIDTH), kv_spec(LANES),
        pl.BlockSpec((seq // LANES, V_ROWS, LANES), lambda b, i: (b, 0, 0)),
    ]
    in_specs += [_const_spec(a.shape) for a in (kam, kbm, cbm, vtm) + tuple(tables)]
    out_shape = [jax.ShapeDtypeStruct((batch * seq, A_WIDTH), BF16),
                 jax.ShapeDtypeStruct((batch * seq, B_WIDTH), BF16)]
    out_specs = [q_spec(A_WIDTH), q_spec(B_WIDTH)]
    scratch = [
        pltpu.VMEM((B_HEADS // 2, 1, 2 * Q_TILE), F32),
        pltpu.VMEM((B_HEADS // 2, 1, 2 * Q_TILE), F32),
        pltpu.VMEM((B_HEADS, HEAD_DIM, Q_TILE), F32),
    ]
    return pl.pallas_call(
        _attn_kernel,
        grid=grid,
        in_specs=in_specs,
        out_specs=out_specs,
        out_shape=out_shape,
        scratch_shapes=scratch,
        compiler_params=pltpu.CompilerParams(
            dimension_semantics=("arbitrary", "arbitrary"), vmem_limit_bytes=VMEM_LIMIT_BYTES),
        name="attention",
    )(sinks, qa, ka, qb, kb, cb, vt, kam, kbm, cbm, vtm, *tables)


def _mix_ffn_kernel(h1_ref, oa_ref, ob_ref, ga_ref, gb_ref, wa_ref, wb_ref, wo_ref,
                    n2_ref, w2i_ref, w2o_ref, nf_ref, out_ref):
    ya = _dot(oa_ref[...], wa_ref[...])
    yb = _dot(ob_ref[...], wb_ref[...])
    mixed = (ga_ref[...].astype(F32) * ya + gb_ref[...].astype(F32) * yb).astype(BF16)
    h2 = h1_ref[...] + _dot(mixed, wo_ref[...])
    xn = _rms(h2, n2_ref[...]).astype(BF16)
    h3 = h2 + 0.5 * _swiglu(xn, w2i_ref, w2o_ref)
    out_ref[...] = _rms(h3, nf_ref[...])


def _mix_ffn(h1, oa, ob, ga, gb, wa, wb, wo, n2, w2i, w2o, nf, tile):
    rows = h1.shape[0]

    def row_spec(cols):
        return pl.BlockSpec((tile, cols), lambda i: (i, 0))

    in_specs = [row_spec(D_MODEL), row_spec(A_WIDTH), row_spec(B_WIDTH), row_spec(D_MODEL), row_spec(D_MODEL)]
    in_specs += [_const_spec(w.shape) for w in (wa, wb, wo, n2, w2i, w2o, nf)]
    return pl.pallas_call(
        _mix_ffn_kernel,
        grid=(rows // tile,),
        in_specs=in_specs,
        out_specs=row_spec(D_MODEL),
        out_shape=jax.ShapeDtypeStruct((rows, D_MODEL), F32),
        compiler_params=pltpu.CompilerParams(
            dimension_semantics=("arbitrary",), vmem_limit_bytes=VMEM_LIMIT_BYTES),
        name="mix_ffn2",
    )(h1, oa, ob, ga, gb, wa, wb, wo, n2, w2i, w2o, nf)


def _prep_proj_weight(w):
    qa = w[:, 0:512]
    ka = w[:, 512:640]
    va = w[:, 640:768]
    qb, kb, vb = w[:, 768:1280], w[:, 1280:1792], w[:, 1792:2304]
    wf = w[:, 2304:2312]
    g_a, g_b = w[:, 2312:3336], w[:, 3336:4360]
    k0, k1 = ka[:, :HEAD_DIM], ka[:, HEAD_DIM:]
    ka_dup = jnp.concatenate([k0, k0, k1, k1], axis=1)
    wf_pad = jnp.pad(wf, ((0, 0), (0, LANES - B_HEADS)))
    wp = jnp.concatenate([qa, ka_dup, qb, kb, g_a, g_b, wf_pad], axis=1).astype(BF16)
    wv = jnp.concatenate([vb, va], axis=1).astype(BF16)
    return wp, wv


def _attention_tables():
    slopes = 2.0 ** -(np.arange(1, A_HEADS + 1, dtype=np.float64))
    k = np.arange(BLOCK)[:, None]
    r = np.arange(BLOCK)[None, :]
    band = np.empty((2, A_KV_HEADS, 2 * BLOCK, A_GROUP * BLOCK), np.float32)
    meta = np.empty((A_KV_HEADS, N_META, A_GROUP * BLOCK), np.float32)
    slope_rows = np.empty((A_KV_HEADS, 1, A_GROUP * BLOCK), np.float32)
    mk = np.arange(N_META)[:, None]
    for h in range(A_HEADS):
        g, j = divmod(h, A_GROUP)
        cols = slice(j * BLOCK, (j + 1) * BLOCK)
        cur = np.where(k <= r, -slopes[h] * (r - k), NEG)
        prev = np.where(k > r, -slopes[h] * (r - k + BLOCK), NEG)
        band[0, g, :, cols] = np.concatenate([cur, np.full_like(cur, NEG)], axis=0)
        band[1, g, :, cols] = np.concatenate([prev, cur], axis=0)
        meta[g, :, cols] = -slopes[h] * (r + N_META - mk)
        slope_rows[g, :, cols] = slopes[h] * BLOCK
    qext = np.zeros((B_HEADS // 2, 2 * Q_TILE, LANES), np.float32)
    for h in range(B_HEADS):
        pr, hh = divmod(h, 2)
        qext[pr, hh * Q_TILE:(hh + 1) * Q_TILE, BIAS_PIECES * h:BIAS_PIECES * (h + 1)] = 1.0
    return (jnp.asarray(band), jnp.asarray(meta), jnp.asarray(slope_rows), jnp.asarray(qext, BF16))


def _bias_placement():
    place = np.zeros((BIAS_PIECES, LANES, LANES), np.float32)
    for h in range(B_HEADS):
        for p in range(BIAS_PIECES):
            place[p, h, BIAS_PIECES * h + p] = 1.0
    return jnp.asarray(place, BF16)


def kernel(x, meta_tokens, ffn1_norm, ffn1_w_in, ffn1_w_out, mix_norm, w_in, b_forget, attn_sinks,
           w_branch_a, w_branch_b, w_out, ffn2_norm, ffn2_w_in, ffn2_w_out, final_norm):
    batch, seq, d = x.shape
    assert d == D_MODEL and seq % TOKEN_TILE == 0 and seq % Q_TILE == 0
    assert ffn1_norm.shape[0] == 1, "single layer"

    n1 = ffn1_norm[0][None].astype(F32)
    nm = mix_norm[0][None].astype(F32)
    n2 = ffn2_norm[0][None].astype(F32)
    nf = final_norm[None].astype(F32)
    w1i, w1o = ffn1_w_in[0].astype(BF16), ffn1_w_out[0].astype(BF16)
    w2i, w2o = ffn2_w_in[0].astype(BF16), ffn2_w_out[0].astype(BF16)
    wp, wv = _prep_proj_weight(w_in[0])
    bfg = jnp.pad(b_forget[0].astype(F32), (0, LANES - B_HEADS))[None]
    wa, wb, wo = w_branch_a[0].astype(BF16), w_branch_b[0].astype(BF16), w_out[0].astype(BF16)
    place = _bias_placement()

    x2d = x.reshape(batch * seq, D_MODEL)
    h1, qa, ka, qb, kb, ga, gb, cb, vt = _ffn_proj(
        x2d, n1, w1i, w1o, nm, wp, wv.T, bfg, place, TOKEN_TILE, seq // TOKEN_TILE, False)
    _, _, kam, _, kbm, _, _, cbm, vm = _ffn_proj(
        meta_tokens.astype(F32), n1, w1i, w1o, nm, wp, wv, bfg, place, N_META, 1, True)

    oa, ob = _attention(attn_sinks[0].astype(F32), qa, ka, qb, kb, cb, vt, kam, kbm, cbm, vm.T,
                        _attention_tables(), batch, seq)
    out = _mix_ffn(h1, oa, ob, ga, gb, wa, wb, wo, n2, w2i, w2o, nf, TOKEN_TILE)
    return out.reshape(batch, seq, D_MODEL)
```

```python
import jax
import jax.numpy as jnp
import numpy as np
from jax import lax
from jax.experimental import pallas as pl
from jax.experimental.pallas import tpu as pltpu

F32 = jnp.float32
BF16 = jnp.bfloat16

D_MODEL = 1024
N_META = 16
BLOCK = 128
HEAD_DIM = 64
A_HEADS = 8
A_KV_HEADS = 2
A_GROUP = A_HEADS // A_KV_HEADS
B_HEADS = 8
A_WIDTH = A_HEADS * HEAD_DIM
B_WIDTH = B_HEADS * HEAD_DIM
V_ROWS = B_WIDTH + A_KV_HEADS * HEAD_DIM
D_FF = 2816
EPS = 1e-6
NEG = -1e30
LOG2E = 1.4426950408889634
Q_SCALE = HEAD_DIM ** -0.5 * LOG2E

LANES = 128
TOKEN_TILE = 512
Q_TILE = 256
KV_TILE = 256
FF_CHUNKS = ((0, 1024), (1024, 1024), (2048, 768))
VMEM_LIMIT_BYTES = 56 * 1024 * 1024
BIAS_PIECES = 3
ONES_ROWS = 16

_QA, _KA, _QB, _KB, _GA, _GB, _FG = 0, 512, 768, 1280, 1792, 2816, 3840
PROJ_COLS = 3968


def _dot(a, b):
    return jnp.dot(a, b, preferred_element_type=F32)


def _dot_nt(a, b):
    return lax.dot_general(a, b, (((1,), (1,)), ((), ())), preferred_element_type=F32)


def _rms(x, g):
    ms = jnp.mean(x * x, axis=-1, keepdims=True)
    return x * lax.rsqrt(ms + EPS) * g


def _swiglu(xn, w_in_ref, w_out_ref):
    acc = None
    for lo, width in FF_CHUNKS:
        g = _dot(xn, w_in_ref[:, lo:lo + width])
        u = _dot(xn, w_in_ref[:, D_FF + lo:D_FF + lo + width])
        a = (g * jax.nn.sigmoid(g) * u).astype(BF16)
        part = _dot(a, w_out_ref[lo:lo + width, :])
        acc = part if acc is None else acc + part
    return acc


def _split3(x):
    hi = x.astype(BF16)
    r = x - hi.astype(F32)
    mid = r.astype(BF16)
    lo = (r - mid.astype(F32)).astype(BF16)
    return hi, mid, lo


def _ffn_proj_kernel(tiles_per_seq, is_meta,
                     x_ref, n1_ref, w1i_ref, w1o_ref, nm_ref, wp_ref, wv_ref, bf_ref, tri_ref, place_ref,
                     h1_ref, qa_ref, ka_ref, qb_ref, kb_ref, ga_ref, gb_ref, cb_ref, v_ref, carry_ref):
    x = x_ref[...]
    tile = x.shape[0]
    xn = _rms(x, n1_ref[...]).astype(BF16)
    h1 = x + 0.5 * _swiglu(xn, w1i_ref, w1o_ref)
    h1_ref[...] = h1
    u = _rms(h1, nm_ref[...]).astype(BF16)

    def proj(lo, width):
        return _dot(u, wp_ref[:, lo:lo + width])

    qa_ref[...] = (proj(_QA, 512) * Q_SCALE).astype(BF16)
    ka_ref[...] = proj(_KA, 256).astype(BF16)
    qb_ref[...] = (proj(_QB, 512) * Q_SCALE).astype(BF16)
    kb_ref[...] = proj(_KB, 512).astype(BF16)
    ga_ref[...] = jax.nn.sigmoid(proj(_GA, 1024)).astype(BF16)
    gb_ref[...] = jax.nn.sigmoid(proj(_GB, 1024)).astype(BF16)

    if is_meta:
        v_ref[...] = _dot(u, wv_ref[...]).astype(BF16)
    else:
        vt = _dot_nt(wv_ref[...], u).astype(BF16)
        for c in range(tile // LANES):
            v_ref[c] = vt[:, c * LANES:(c + 1) * LANES]

    z = proj(_FG, LANES) + bf_ref[...]
    lane = lax.broadcasted_iota(jnp.int32, (1, LANES), 1)
    lf = jnp.where(lane < B_HEADS, jnp.minimum(z, 0.0) - jnp.log1p(jnp.exp(-jnp.abs(z))), 0.0)

    @pl.when(pl.program_id(0) % tiles_per_seq == 0)
    def _():
        carry_ref[...] = jnp.zeros_like(carry_ref)

    tri = tri_ref[...]
    c = carry_ref[...] + sum(_dot(tri, piece) for piece in _split3(lf))
    carry_ref[...] = c[tile - 1:tile, :]
    bias = ((c[tile - 1:tile, :] - c) if is_meta else -c) * LOG2E
    cb_ref[...] = sum(_dot(piece, place_ref[p]) for p, piece in enumerate(_split3(bias))).astype(BF16)


def _const_spec(shape):
    zeros = (0,) * len(shape)
    return pl.BlockSpec(shape, lambda *_: zeros, pipeline_mode=pl.Buffered(1))


def _ffn_proj(x2d, n1, w1i, w1o, nm, wp, wv, bfg, place, tile, tiles_per_seq, is_meta):
    rows = x2d.shape[0]
    grid = (rows // tile,)
    tri = jnp.asarray(np.tril(np.ones((tile, tile), np.float32)), BF16)

    def row_spec(cols):
        return pl.BlockSpec((tile, cols), lambda i: (i, 0))

    out_cols = (D_MODEL, 512, 256, 512, 512, 1024, 1024, LANES)
    out_dtypes = (F32,) + (BF16,) * 7
    out_shape = [jax.ShapeDtypeStruct((rows, c), dt) for c, dt in zip(out_cols, out_dtypes)]
    out_specs = [row_spec(c) for c in out_cols]
    if is_meta:
        out_shape.append(jax.ShapeDtypeStruct((rows, V_ROWS), BF16))
        out_specs.append(row_spec(V_ROWS))
    else:
        out_shape.append(jax.ShapeDtypeStruct((rows // LANES, V_ROWS, LANES), BF16))
        out_specs.append(pl.BlockSpec((tile // LANES, V_ROWS, LANES), lambda i: (i, 0, 0)))
    consts = (n1, w1i, w1o, nm, wp, wv, bfg, tri, place)
    in_specs = [row_spec(D_MODEL)] + [_const_spec(a.shape) for a in consts]
    return pl.pallas_call(
        lambda *refs: _ffn_proj_kernel(tiles_per_seq, is_meta, *refs),
        grid=grid,
        in_specs=in_specs,
        out_specs=out_specs,
        out_shape=out_shape,
        scratch_shapes=[pltpu.VMEM((1, LANES), F32)],
        compiler_params=pltpu.CompilerParams(
            dimension_semantics=("arbitrary",), vmem_limit_bytes=VMEM_LIMIT_BYTES),
        name="ffn1_proj_meta" if is_meta else "ffn1_proj",
    )(x2d, *consts)


def _split_heads(q, low):
    zero = jnp.zeros_like(q)
    return jnp.concatenate([jnp.where(low, q, zero), jnp.where(low, zero, q)], axis=0)


def _attn_kernel(sinks_ref, qa_ref, ka_ref, qb_ref, kb_ref, cb_ref, vt_ref,
                 kam_ref, kbm_ref, cbm_ref, vtm_ref, band_ref, mtab_ref, slope_ref, qext_ref,
                 oa_ref, ob_ref, m_ref, acc_ref):
    i = pl.program_id(1)
    low = lax.broadcasted_iota(jnp.int32, (1, LANES), 1) < HEAD_DIM
    sub = Q_TILE // BLOCK
    ones_tile = jnp.ones((ONES_ROWS, KV_TILE), BF16)
    ones_meta = jnp.ones((ONES_ROWS, N_META), BF16)

    def v_tile(blk, row0):
        return jnp.concatenate(
            [jnp.concatenate([vt_ref[blk, row0:row0 + HEAD_DIM, :], vt_ref[blk + 1, row0:row0 + HEAD_DIM, :]], axis=1),
             ones_tile], axis=0)

    def v_meta(row0):
        return jnp.concatenate([vtm_ref[row0:row0 + HEAD_DIM, :], ones_meta], axis=0)

    units = [(sb, g) for sb in range(sub) for g in range(A_KV_HEADS)]
    blk0s, s_bs, s_ms = {}, {}, {}
    for sb, g in units:
        n = i * sub + sb
        blk0s[sb] = jnp.maximum(n - 1, 0)
        start = pl.multiple_of(blk0s[sb] * BLOCK, BLOCK)
        q = qa_ref[sb * BLOCK:(sb + 1) * BLOCK, :]
        qq = jnp.concatenate(
            [_split_heads(q[:, (2 * g + p) * LANES:(2 * g + p + 1) * LANES], low) for p in range(2)], axis=0)
        kband = ka_ref[pl.ds(start, 2 * BLOCK), g * LANES:(g + 1) * LANES]
        kmeta = kam_ref[:, g * LANES:(g + 1) * LANES]
        s_bs[sb, g] = _dot_nt(kband, qq) + band_ref[jnp.minimum(n, 1), g]
        s_ms[sb, g] = _dot_nt(kmeta, qq) + mtab_ref[g] - n.astype(F32) * slope_ref[g]
    p_bs, p_ms, sink_ps = {}, {}, {}
    for sb, g in units:
        s_b, s_m = s_bs[sb, g], s_ms[sb, g]
        sink = jnp.concatenate(
            [jnp.full((1, BLOCK), sinks_ref[g * A_GROUP + j] * LOG2E, F32) for j in range(A_GROUP)], axis=1)
        m = jnp.maximum(jnp.max(s_b, axis=0, keepdims=True), jnp.max(s_m, axis=0, keepdims=True))
        m = jnp.maximum(m, sink)
        p_bs[sb, g] = jnp.exp2(s_b - m).astype(BF16)
        p_ms[sb, g] = jnp.exp2(s_m - m).astype(BF16)
        sink_ps[sb, g] = jnp.exp2(sink - m)
    for sb, g in units:
        row0 = B_WIDTH + g * HEAD_DIM
        o_aug = _dot(v_tile(blk0s[sb], row0), p_bs[sb, g]) + _dot(v_meta(row0), p_ms[sb, g])
        o_t = o_aug[:HEAD_DIM] / (o_aug[HEAD_DIM:HEAD_DIM + 1] + sink_ps[sb, g])
        for p in range(2):
            pair_t = jnp.concatenate([o_t[:, (2 * p) * BLOCK:(2 * p + 1) * BLOCK],
                                      o_t[:, (2 * p + 1) * BLOCK:(2 * p + 2) * BLOCK]], axis=0)
            oa_ref[sb * BLOCK:(sb + 1) * BLOCK, (2 * g + p) * LANES:(2 * g + p + 1) * LANES] = pair_t.T.astype(BF16)

    n_pairs = B_HEADS // 2
    q_aug = []
    for pr in range(n_pairs):
        qq = _split_heads(qb_ref[:, pr * LANES:(pr + 1) * LANES], low)
        q_aug.append(jnp.concatenate([qq, qext_ref[pr]], axis=1))

    row = lax.broadcasted_iota(jnp.int32, (KV_TILE, Q_TILE), 0)
    col = lax.broadcasted_iota(jnp.int32, (KV_TILE, Q_TILE), 1)
    causal1 = jnp.where(row <= col, 0.0, NEG)
    causal = jnp.concatenate([causal1, causal1], axis=1)

    scores = []
    for pr in range(n_pairs):
        k_aug = jnp.concatenate([kbm_ref[:, pr * LANES:(pr + 1) * LANES], cbm_ref[...]], axis=1)
        scores.append(_dot_nt(k_aug, q_aug[pr]))
    probs = []
    for pr in range(n_pairs):
        m0 = jnp.max(scores[pr], axis=0, keepdims=True)
        m_ref[pr] = m0
        probs.append(jnp.exp2(scores[pr] - m0).astype(BF16))
    for pr in range(n_pairs):
        for hh in range(2):
            h = 2 * pr + hh
            acc_ref[h] = _dot(v_meta(h * HEAD_DIM), probs[pr][:, hh * Q_TILE:(hh + 1) * Q_TILE])

    def update(j, mask):
        k0 = pl.multiple_of(j * KV_TILE, KV_TILE)
        cb = cb_ref[pl.ds(k0, KV_TILE), :]
        scores = []
        for pr in range(n_pairs):
            k_aug = jnp.concatenate([kb_ref[pl.ds(k0, KV_TILE), pr * LANES:(pr + 1) * LANES], cb], axis=1)
            scores.append(_dot_nt(k_aug, q_aug[pr]))
        probs, alphas = [], []
        for pr in range(n_pairs):
            s = scores[pr]
            if mask is not None:
                s = s + mask
            m_old = m_ref[pr]
            m_new = jnp.maximum(m_old, jnp.max(s, axis=0, keepdims=True))
            m_ref[pr] = m_new
            alphas.append(jnp.exp2(m_old - m_new))
            probs.append(jnp.exp2(s - m_new).astype(BF16))
        for pr in range(n_pairs):
            pb, alpha = probs[pr], alphas[pr]
            for hh in range(2):
                h = 2 * pr + hh
                qs = slice(hh * Q_TILE, (hh + 1) * Q_TILE)
                acc_ref[h] = alpha[:, qs] * acc_ref[h] + _dot(v_tile(2 * j, h * HEAD_DIM), pb[:, qs])

    def body(j, carry):
        update(j, None)
        return carry

    lax.fori_loop(0, i, body, 0)
    update(i, causal)

    for pr in range(n_pairs):
        acc0, acc1 = acc_ref[2 * pr], acc_ref[2 * pr + 1]
        o_t = jnp.concatenate([acc0[:HEAD_DIM] / acc0[HEAD_DIM:HEAD_DIM + 1],
                               acc1[:HEAD_DIM] / acc1[HEAD_DIM:HEAD_DIM + 1]], axis=0)
        ob_ref[:, pr * LANES:(pr + 1) * LANES] = o_t.T.astype(BF16)


def _attention(sinks, qa, ka, qb, kb, cb, vt, kam, kbm, cbm, vtm, tables, batch, seq):
    nq = seq // Q_TILE
    grid = (batch, nq)

    def q_spec(cols):
        return pl.BlockSpec((Q_TILE, cols), lambda b, i: (b * nq + i, 0))

    def kv_spec(cols):
        return pl.BlockSpec((seq, cols), lambda b, i: (b, 0))

    in_specs = [
        pl.BlockSpec(memory_space=pltpu.SMEM),
        q_spec(A_WIDTH), kv_spec(256), q_spec(B_WIDTH), kv_spec(B_WIDTH), kv_spec(LANES),
        pl.BlockSpec((seq // LANES, V_ROWS, LANES), lambda b, i: (b, 0, 0)),
    ]
    in_specs += [_const_spec(a.shape) for a in (kam, kbm, cbm, vtm) + tuple(tables)]
    out_shape = [jax.ShapeDtypeStruct((batch * seq, A_WIDTH), BF16),
                 jax.ShapeDtypeStruct((batch * seq, B_WIDTH), BF16)]
    out_specs = [q_spec(A_WIDTH), q_spec(B_WIDTH)]
    scratch = [
        pltpu.VMEM((B_HEADS // 2, 1, 2 * Q_TILE), F32),
        pltpu.VMEM((B_HEADS, HEAD_DIM + ONES_ROWS, Q_TILE), F32),
    ]
    return pl.pallas_call(
        _attn_kernel,
        grid=grid,
        in_specs=in_specs,
        out_specs=out_specs,
        out_shape=out_shape,
        scratch_shapes=scratch,
        compiler_params=pltpu.CompilerParams(
            dimension_semantics=("arbitrary", "arbitrary"), vmem_limit_bytes=VMEM_LIMIT_BYTES),
        name="attention",
    )(sinks, qa, ka, qb, kb, cb, vt, kam, kbm, cbm, vtm, *tables)


def _mix_ffn_kernel(h1_ref, oa_ref, ob_ref, ga_ref, gb_ref, wa_ref, wb_ref, wo_ref,
                    n2_ref, w2i_ref, w2o_ref, nf_ref, out_ref):
    ya = _dot(oa_ref[...], wa_ref[...])
    yb = _dot(ob_ref[...], wb_ref[...])
    mixed = (ga_ref[...].astype(F32) * ya + gb_ref[...].astype(F32) * yb).astype(BF16)
    h2 = h1_ref[...] + _dot(mixed, wo_ref[...])
    xn = _rms(h2, n2_ref[...]).astype(BF16)
    h3 = h2 + 0.5 * _swiglu(xn, w2i_ref, w2o_ref)
    out_ref[...] = _rms(h3, nf_ref[...])


def _mix_ffn(h1, oa, ob, ga, gb, wa, wb, wo, n2, w2i, w2o, nf, tile):
    rows = h1.shape[0]

    def row_spec(cols):
        return pl.BlockSpec((tile, cols), lambda i: (i, 0))

    in_specs = [row_spec(D_MODEL), row_spec(A_WIDTH), row_spec(B_WIDTH), row_spec(D_MODEL), row_spec(D_MODEL)]
    in_specs += [_const_spec(w.shape) for w in (wa, wb, wo, n2, w2i, w2o, nf)]
    return pl.pallas_call(
        _mix_ffn_kernel,
        grid=(rows // tile,),
        in_specs=in_specs,
        out_specs=row_spec(D_MODEL),
        out_shape=jax.ShapeDtypeStruct((rows, D_MODEL), F32),
        compiler_params=pltpu.CompilerParams(
            dimension_semantics=("arbitrary",), vmem_limit_bytes=VMEM_LIMIT_BYTES),
        name="mix_ffn2",
    )(h1, oa, ob, ga, gb, wa, wb, wo, n2, w2i, w2o, nf)


def _prep_proj_weight(w):
    qa = w[:, 0:512]
    ka = w[:, 512:640]
    va = w[:, 640:768]
    qb, kb, vb = w[:, 768:1280], w[:, 1280:1792], w[:, 1792:2304]
    wf = w[:, 2304:2312]
    g_a, g_b = w[:, 2312:3336], w[:, 3336:4360]
    k0, k1 = ka[:, :HEAD_DIM], ka[:, HEAD_DIM:]
    ka_dup = jnp.concatenate([k0, k0, k1, k1], axis=1)
    wf_pad = jnp.pad(wf, ((0, 0), (0, LANES - B_HEADS)))
    wp = jnp.concatenate([qa, ka_dup, qb, kb, g_a, g_b, wf_pad], axis=1).astype(BF16)
    wv = jnp.concatenate([vb, va], axis=1).astype(BF16)
    return wp, wv


def _attention_tables():
    slopes = 2.0 ** -(np.arange(1, A_HEADS + 1, dtype=np.float64))
    k = np.arange(BLOCK)[:, None]
    r = np.arange(BLOCK)[None, :]
    band = np.empty((2, A_KV_HEADS, 2 * BLOCK, A_GROUP * BLOCK), np.float32)
    meta = np.empty((A_KV_HEADS, N_META, A_GROUP * BLOCK), np.float32)
    slope_rows = np.empty((A_KV_HEADS, 1, A_GROUP * BLOCK), np.float32)
    mk = np.arange(N_META)[:, None]
    for h in range(A_HEADS):
        g, j = divmod(h, A_GROUP)
        cols = slice(j * BLOCK, (j + 1) * BLOCK)
        cur = np.where(k <= r, -slopes[h] * (r - k), NEG)
        prev = np.where(k > r, -slopes[h] * (r - k + BLOCK), NEG)
        band[0, g, :, cols] = np.concatenate([cur, np.full_like(cur, NEG)], axis=0)
        band[1, g, :, cols] = np.concatenate([prev, cur], axis=0)
        meta[g, :, cols] = -slopes[h] * (r + N_META - mk)
        slope_rows[g, :, cols] = slopes[h] * BLOCK
    qext = np.zeros((B_HEADS // 2, 2 * Q_TILE, LANES), np.float32)
    for h in range(B_HEADS):
        pr, hh = divmod(h, 2)
        qext[pr, hh * Q_TILE:(hh + 1) * Q_TILE, BIAS_PIECES * h:BIAS_PIECES * (h + 1)] = 1.0
    band = np.where(band > NEG, band * LOG2E, NEG)
    tables = [t.astype(np.float32) for t in (band, meta * LOG2E, slope_rows * LOG2E)]
    return tuple(jnp.asarray(t) for t in tables) + (jnp.asarray(qext, BF16),)


def _bias_placement():
    place = np.zeros((BIAS_PIECES, LANES, LANES), np.float32)
    for h in range(B_HEADS):
        for p in range(BIAS_PIECES):
            place[p, h, BIAS_PIECES * h + p] = 1.0
    return jnp.asarray(place, BF16)


def kernel(x, meta_tokens, ffn1_norm, ffn1_w_in, ffn1_w_out, mix_norm, w_in, b_forget, attn_sinks,
           w_branch_a, w_branch_b, w_out, ffn2_norm, ffn2_w_in, ffn2_w_out, final_norm):
    batch, seq, d = x.shape
    assert d == D_MODEL and seq % TOKEN_TILE == 0 and seq % Q_TILE == 0
    assert ffn1_norm.shape[0] == 1, "single layer"

    n1 = ffn1_norm[0][None].astype(F32)
    nm = mix_norm[0][None].astype(F32)
    n2 = ffn2_norm[0][None].astype(F32)
    nf = final_norm[None].astype(F32)
    w1i, w1o = ffn1_w_in[0].astype(BF16), ffn1_w_out[0].astype(BF16)
    w2i, w2o = ffn2_w_in[0].astype(BF16), ffn2_w_out[0].astype(BF16)
    wp, wv = _prep_proj_weight(w_in[0])
    bfg = jnp.pad(b_forget[0].astype(F32), (0, LANES - B_HEADS))[None]
    wa, wb, wo = w_branch_a[0].astype(BF16), w_branch_b[0].astype(BF16), w_out[0].astype(BF16)
    place = _bias_placement()

    x2d = x.reshape(batch * seq, D_MODEL)
    h1, qa, ka, qb, kb, ga, gb, cb, vt = _ffn_proj(
        x2d, n1, w1i, w1o, nm, wp, wv.T, bfg, place, TOKEN_TILE, seq // TOKEN_TILE, False)
    _, _, kam, _, kbm, _, _, cbm, vm = _ffn_proj(
        meta_tokens.astype(F32), n1, w1i, w1o, nm, wp, wv, bfg, place, N_META, 1, True)

    oa, ob = _attention(attn_sinks[0].astype(F32), qa, ka, qb, kb, cb, vt, kam, kbm, cbm, vm.T,
                        _attention_tables(), batch, seq)
    out = _mix_ffn(h1, oa, ob, ga, gb, wa, wb, wo, n2, w2i, w2o, nf, TOKEN_TILE)
    return out.reshape(batch, seq, D_MODEL)
```

```python
import jax
import jax.numpy as jnp
import numpy as np
from jax import lax
from jax.experimental import pallas as pl
from jax.experimental.pallas import tpu as pltpu

F32 = jnp.float32
BF16 = jnp.bfloat16

D_MODEL = 1024
N_META = 16
BLOCK = 128
HEAD_DIM = 64
A_HEADS = 8
A_KV_HEADS = 2
A_GROUP = A_HEADS // A_KV_HEADS
B_HEADS = 8
A_WIDTH = A_HEADS * HEAD_DIM
B_WIDTH = B_HEADS * HEAD_DIM
V_ROWS = B_WIDTH + A_KV_HEADS * HEAD_DIM
D_FF = 2816
EPS = 1e-6
NEG = -1e30
LOG2E = 1.4426950408889634
Q_SCALE = HEAD_DIM ** -0.5 * LOG2E

LANES = 128
TOKEN_TILE = 512
Q_TILE = 256
KV_TILE = 256
FF_CHUNKS = ((0, 1024), (1024, 1024), (2048, 768))
FUSED_FF_CHUNKS = ((0, 512), (512, 512), (1024, 512), (1536, 512), (2048, 512), (2560, 256))
FUSED_FF_SPLIT = 2
VMEM_LIMIT_BYTES = 56 * 1024 * 1024
BIAS_PIECES = 3
ONES_ROWS = 16

_QA, _KA, _QB, _KB, _GA, _GB, _FG = 0, 512, 768, 1280, 1792, 2816, 3840
PROJ_COLS = 3968


def _dot(a, b):
    return jnp.dot(a, b, preferred_element_type=F32)


def _dot_nt(a, b):
    return lax.dot_general(a, b, (((1,), (1,)), ((), ())), preferred_element_type=F32)


def _rms(x, g):
    ms = jnp.mean(x * x, axis=-1, keepdims=True)
    return x * lax.rsqrt(ms + EPS) * g


def _swiglu(xn, w_in_ref, w_out_ref):
    acc = None
    for lo, width in FF_CHUNKS:
        g = _dot(xn, w_in_ref[:, lo:lo + width])
        u = _dot(xn, w_in_ref[:, D_FF + lo:D_FF + lo + width])
        a = (g * jax.nn.sigmoid(g) * u).astype(BF16)
        part = _dot(a, w_out_ref[lo:lo + width, :])
        acc = part if acc is None else acc + part
    return acc


def _split3(x):
    hi = x.astype(BF16)
    r = x - hi.astype(F32)
    mid = r.astype(BF16)
    lo = (r - mid.astype(F32)).astype(BF16)
    return hi, mid, lo


def _ffn_proj_kernel(tiles_per_seq, is_meta,
                     x_ref, n1_ref, w1i_ref, w1o_ref, nm_ref, wp_ref, wv_ref, bf_ref, tri_ref, place_ref,
                     h1_ref, qa_ref, ka_ref, qb_ref, kb_ref, ga_ref, gb_ref, cb_ref, v_ref, carry_ref):
    x = x_ref[...]
    tile = x.shape[0]
    xn = _rms(x, n1_ref[...]).astype(BF16)
    h1 = x + 0.5 * _swiglu(xn, w1i_ref, w1o_ref)
    h1_ref[...] = h1
    u = _rms(h1, nm_ref[...]).astype(BF16)

    def proj(lo, width):
        return _dot(u, wp_ref[:, lo:lo + width])

    qa_ref[...] = (proj(_QA, 512) * Q_SCALE).astype(BF16)
    ka_ref[...] = proj(_KA, 256).astype(BF16)
    qb_ref[...] = (proj(_QB, 512) * Q_SCALE).astype(BF16)
    kb_ref[...] = proj(_KB, 512).astype(BF16)
    ga_ref[...] = jax.nn.sigmoid(proj(_GA, 1024)).astype(BF16)
    gb_ref[...] = jax.nn.sigmoid(proj(_GB, 1024)).astype(BF16)

    if is_meta:
        v_ref[...] = _dot(u, wv_ref[...]).astype(BF16)
    else:
        vt = _dot_nt(wv_ref[...], u).astype(BF16)
        for c in range(tile // LANES):
            v_ref[c] = vt[:, c * LANES:(c + 1) * LANES]

    z = proj(_FG, LANES) + bf_ref[...]
    lane = lax.broadcasted_iota(jnp.int32, (1, LANES), 1)
    lf = jnp.where(lane < B_HEADS, jnp.minimum(z, 0.0) - jnp.log1p(jnp.exp(-jnp.abs(z))), 0.0)

    @pl.when(pl.program_id(0) % tiles_per_seq == 0)
    def _():
        carry_ref[...] = jnp.zeros_like(carry_ref)

    tri = tri_ref[...]
    c = carry_ref[...] + sum(_dot(tri, piece) for piece in _split3(lf))
    carry_ref[...] = c[tile - 1:tile, :]
    bias = ((c[tile - 1:tile, :] - c) if is_meta else -c) * LOG2E
    cb_ref[...] = sum(_dot(piece, place_ref[p]) for p, piece in enumerate(_split3(bias))).astype(BF16)


def _const_spec(shape):
    zeros = (0,) * len(shape)
    return pl.BlockSpec(shape, lambda *_: zeros, pipeline_mode=pl.Buffered(1))


def _ffn_proj(x2d, n1, w1i, w1o, nm, wp, wv, bfg, place, tile, tiles_per_seq, is_meta):
    rows = x2d.shape[0]
    grid = (rows // tile,)
    tri = jnp.asarray(np.tril(np.ones((tile, tile), np.float32)), BF16)

    def row_spec(cols):
        return pl.BlockSpec((tile, cols), lambda i: (i, 0))

    out_cols = (D_MODEL, 512, 256, 512, 512, 1024, 1024, LANES)
    out_dtypes = (F32,) + (BF16,) * 7
    out_shape = [jax.ShapeDtypeStruct((rows, c), dt) for c, dt in zip(out_cols, out_dtypes)]
    out_specs = [row_spec(c) for c in out_cols]
    if is_meta:
        out_shape.append(jax.ShapeDtypeStruct((rows, V_ROWS), BF16))
        out_specs.append(row_spec(V_ROWS))
    else:
        out_shape.append(jax.ShapeDtypeStruct((rows // LANES, V_ROWS, LANES), BF16))
        out_specs.append(pl.BlockSpec((tile // LANES, V_ROWS, LANES), lambda i: (i, 0, 0)))
    consts = (n1, w1i, w1o, nm, wp, wv, bfg, tri, place)
    in_specs = [row_spec(D_MODEL)] + [_const_spec(a.shape) for a in consts]
    return pl.pallas_call(
        lambda *refs: _ffn_proj_kernel(tiles_per_seq, is_meta, *refs),
        grid=grid,
        in_specs=in_specs,
        out_specs=out_specs,
        out_shape=out_shape,
        scratch_shapes=[pltpu.VMEM((1, LANES), F32)],
        compiler_params=pltpu.CompilerParams(
            dimension_semantics=("arbitrary",), vmem_limit_bytes=VMEM_LIMIT_BYTES),
        name="ffn1_proj_meta" if is_meta else "ffn1_proj",
    )(x2d, *consts)


def _split_heads(q, low):
    zero = jnp.zeros_like(q)
    return jnp.concatenate([jnp.where(low, q, zero), jnp.where(low, zero, q)], axis=0)


def _merge(*pieces):
    pieces = list(pieces)
    while pieces:
        for gen in list(pieces):
            if next(gen, pieces) is pieces:
                pieces.remove(gen)
            else:
                yield


def _run(pieces):
    for _ in pieces:
        pass


def _attn_mix_ffn_kernel(n_tiles, tiles_per_seq,
                         sinks_ref, qa_ref, qb_ref, ka_ref, kb_ref, cb_ref, vt_ref,
                         kam_ref, kbm_ref, cbm_ref, vtm_ref, band_ref, mtab_ref, slope_ref, qext_ref,
                         h1_ref, ga_ref, gb_ref, wa_ref, wb_ref, wo_ref, n2_ref, w2i_ref, w2o_ref, nf_ref,
                         out_ref, m_ref, acc_ref, s_ref, p_ref, alpha_ref, o_ref, h2_ref, xn_ref, facc_ref):
    t = pl.program_id(0)
    i = lax.rem(jnp.minimum(t, n_tiles - 1), tiles_per_seq)

    @pl.when(t == 0)
    def _():
        o_ref[...] = jnp.zeros_like(o_ref)

    def swiglu_pieces(xn, chunks):
        acc = None
        for lo, width in chunks:
            g = _dot(xn, w2i_ref[:, lo:lo + width])
            yield
            u = _dot(xn, w2i_ref[:, D_FF + lo:D_FF + lo + width])
            yield
            a = (g * jax.nn.sigmoid(g) * u).astype(BF16)
            part = _dot(a, w2o_ref[lo:lo + width, :])
            yield
            acc = part if acc is None else acc + part
        return acc

    def mix_ffn_head():
        o = o_ref[...]
        ya = _dot(o[:, :A_WIDTH], wa_ref[...])
        yield
        yb = _dot(o[:, A_WIDTH:], wb_ref[...])
        yield
        mixed = (ga_ref[...].astype(F32) * ya + gb_ref[...].astype(F32) * yb).astype(BF16)
        h2 = h1_ref[...] + _dot(mixed, wo_ref[...])
        yield
        h2_ref[...] = h2
        xn = _rms(h2, n2_ref[...]).astype(BF16)
        xn_ref[...] = xn
        facc_ref[...] = yield from swiglu_pieces(xn, FUSED_FF_CHUNKS[:FUSED_FF_SPLIT])

    def mix_ffn_tail():
        part = yield from swiglu_pieces(xn_ref[...], FUSED_FF_CHUNKS[FUSED_FF_SPLIT:])
        h3 = h2_ref[...] + 0.5 * (facc_ref[...] + part)
        out_ref[...] = _rms(h3, nf_ref[...])

    low = lax.broadcasted_iota(jnp.int32, (1, LANES), 1) < HEAD_DIM
    sub = Q_TILE // BLOCK
    assert sub == 2
    ones_tile = jnp.ones((ONES_ROWS, KV_TILE), BF16)
    ones_meta = jnp.ones((ONES_ROWS, N_META), BF16)

    def v_tile(blk, row0):
        return jnp.concatenate(
            [jnp.concatenate([vt_ref[blk, row0:row0 + HEAD_DIM, :], vt_ref[blk + 1, row0:row0 + HEAD_DIM, :]], axis=1),
             ones_tile], axis=0)

    def v_meta(row0):
        return jnp.concatenate([vtm_ref[row0:row0 + HEAD_DIM, :], ones_meta], axis=0)

    def branch_a(sb):
        units = range(A_KV_HEADS)
        n = i * sub + sb
        blk0 = jnp.maximum(n - 1, 0)
        start = pl.multiple_of(blk0 * BLOCK, BLOCK)
        q = qa_ref[sb * BLOCK:(sb + 1) * BLOCK, :]
        s_bs, s_ms = [], []
        for g in units:
            qq = jnp.concatenate(
                [_split_heads(q[:, (2 * g + p) * LANES:(2 * g + p + 1) * LANES], low) for p in range(2)], axis=0)
            kband = ka_ref[pl.ds(start, 2 * BLOCK), g * LANES:(g + 1) * LANES]
            kmeta = kam_ref[:, g * LANES:(g + 1) * LANES]
            s_bs.append(_dot_nt(kband, qq) + band_ref[jnp.minimum(n, 1), g])
            s_ms.append(_dot_nt(kmeta, qq) + mtab_ref[g] - n.astype(F32) * slope_ref[g])
            yield
        p_bs, p_ms, sink_ps = [], [], []
        for g in units:
            sink = jnp.concatenate(
                [jnp.full((1, BLOCK), sinks_ref[g * A_GROUP + j] * LOG2E, F32) for j in range(A_GROUP)], axis=1)
            m = jnp.maximum(jnp.max(s_bs[g], axis=0, keepdims=True), jnp.max(s_ms[g], axis=0, keepdims=True))
            m = jnp.maximum(m, sink)
            p_bs.append(jnp.exp2(s_bs[g] - m).astype(BF16))
            p_ms.append(jnp.exp2(s_ms[g] - m).astype(BF16))
            sink_ps.append(jnp.exp2(sink - m))
            yield
        for g in units:
            row0 = B_WIDTH + g * HEAD_DIM
            o_aug = _dot(v_tile(blk0, row0), p_bs[g]) + _dot(v_meta(row0), p_ms[g])
            o_t = o_aug[:HEAD_DIM] / (o_aug[HEAD_DIM:HEAD_DIM + 1] + sink_ps[g])
            for p in range(2):
                pair_t = jnp.concatenate([o_t[:, (2 * p) * BLOCK:(2 * p + 1) * BLOCK],
                                          o_t[:, (2 * p + 1) * BLOCK:(2 * p + 2) * BLOCK]], axis=0)
                o_ref[sb * BLOCK:(sb + 1) * BLOCK, (2 * g + p) * LANES:(2 * g + p + 1) * LANES] = pair_t.T.astype(BF16)
            yield

    n_pairs = B_HEADS // 2
    q_aug = []
    for pr in range(n_pairs):
        qq = _split_heads(qb_ref[:, pr * LANES:(pr + 1) * LANES], low)
        q_aug.append(jnp.concatenate([qq, qext_ref[pr]], axis=1))

    row = lax.broadcasted_iota(jnp.int32, (KV_TILE, Q_TILE), 0)
    col = lax.broadcasted_iota(jnp.int32, (KV_TILE, Q_TILE), 1)
    causal1 = jnp.where(row <= col, 0.0, NEG)
    causal = jnp.concatenate([causal1, causal1], axis=1)

    def scores_stage(t, slot):
        k0 = pl.multiple_of(t * KV_TILE, KV_TILE)
        cb = cb_ref[pl.ds(k0, KV_TILE), :]
        for pr in range(n_pairs):
            k_aug = jnp.concatenate([kb_ref[pl.ds(k0, KV_TILE), pr * LANES:(pr + 1) * LANES], cb], axis=1)
            s_ref[slot, pr] = _dot_nt(k_aug, q_aug[pr])
            yield

    def softmax_stage(slot, mask):
        for pr in range(n_pairs):
            s = s_ref[slot, pr]
            if mask is not None:
                s = s + mask
            m_old = m_ref[pr]
            m_new = jnp.maximum(m_old, jnp.max(s, axis=0, keepdims=True))
            m_ref[pr] = m_new
            alpha_ref[slot, pr] = jnp.exp2(m_old - m_new)
            p_ref[slot, pr] = jnp.exp2(s - m_new).astype(BF16)
            yield

    def pv_stage(t, slot):
        blk = 2 * jnp.maximum(t, 0)
        for pr in range(n_pairs):
            for hh in range(2):
                h = 2 * pr + hh
                qs = slice(hh * Q_TILE, (hh + 1) * Q_TILE)
                acc_ref[h] = (alpha_ref[slot, pr][:, qs] * acc_ref[h]
                              + _dot(v_tile(blk, h * HEAD_DIM), p_ref[slot, pr][:, qs]))
            yield

    def prologue_b():
        scores = []
        for pr in range(n_pairs):
            k_aug = jnp.concatenate([kbm_ref[:, pr * LANES:(pr + 1) * LANES], cbm_ref[...]], axis=1)
            scores.append(_dot_nt(k_aug, q_aug[pr]))
        yield from scores_stage(0, 0)
        probs = []
        for pr in range(n_pairs):
            m0 = jnp.max(scores[pr], axis=0, keepdims=True)
            m_ref[pr] = m0
            probs.append(jnp.exp2(scores[pr] - m0).astype(BF16))
        p_ref[1] = jnp.zeros(p_ref.shape[1:], BF16)
        alpha_ref[1] = jnp.ones(alpha_ref.shape[1:], F32)
        yield
        for pr in range(n_pairs):
            for hh in range(2):
                h = 2 * pr + hh
                acc_ref[h] = _dot(v_meta(h * HEAD_DIM), probs[pr][:, hh * Q_TILE:(hh + 1) * Q_TILE])

    def body(k, carry):
        for slot in range(2):
            @pl.when(lax.rem(k, 2) == slot)
            def _():
                _run(softmax_stage(slot, None))
                _run(scores_stage(k + 1, 1 - slot))
                _run(pv_stage(k - 1, 1 - slot))
        return carry

    def epilogue_b(slot):
        yield from _merge(pv_stage(i - 1, 1 - slot), softmax_stage(slot, causal))
        yield from pv_stage(i, slot)
        for pr in range(n_pairs):
            acc0, acc1 = acc_ref[2 * pr], acc_ref[2 * pr + 1]
            o_t = jnp.concatenate([acc0[:HEAD_DIM] / acc0[HEAD_DIM:HEAD_DIM + 1],
                                   acc1[:HEAD_DIM] / acc1[HEAD_DIM:HEAD_DIM + 1]], axis=0)
            o_ref[:, A_WIDTH + pr * LANES:A_WIDTH + (pr + 1) * LANES] = o_t.T.astype(BF16)
            yield

    _run(_merge(mix_ffn_head(), prologue_b(), branch_a(0)))
    lax.fori_loop(0, i, body, 0)
    for slot in range(2):
        @pl.when(lax.rem(i, 2) == slot)
        def _():
            _run(_merge(mix_ffn_tail(), epilogue_b(slot), branch_a(1)))


def _attn_mix_ffn(sinks, attn_tiles, attn_seqs, attn_consts, rows_prev, weights, batch, seq):
    nq = seq // Q_TILE
    n_tiles = batch * nq

    def attn_tile(t):
        return jnp.minimum(t, n_tiles - 1)

    def prev_tile(t):
        return jnp.maximum(t - 1, 0)

    def seq_spec(a):
        block = (a.shape[0] // batch,) + a.shape[1:]
        zeros = (0,) * (a.ndim - 1)
        return pl.BlockSpec(block, lambda t: (attn_tile(t) // nq,) + zeros, pipeline_mode=pl.Buffered(1))

    in_specs = [pl.BlockSpec(memory_space=pltpu.SMEM)]
    in_specs += [pl.BlockSpec((Q_TILE, a.shape[1]), lambda t: (attn_tile(t), 0)) for a in attn_tiles]
    in_specs += [seq_spec(a) for a in attn_seqs]
    in_specs += [_const_spec(a.shape) for a in attn_consts]
    in_specs += [pl.BlockSpec((Q_TILE, a.shape[1]), lambda t: (prev_tile(t), 0)) for a in rows_prev]
    in_specs += [_const_spec(a.shape) for a in weights]
    scratch = [
        pltpu.VMEM((B_HEADS // 2, 1, 2 * Q_TILE), F32),
        pltpu.VMEM((B_HEADS, HEAD_DIM + ONES_ROWS, Q_TILE), F32),
        pltpu.VMEM((2, B_HEADS // 2, KV_TILE, 2 * Q_TILE), F32),
        pltpu.VMEM((2, B_HEADS // 2, KV_TILE, 2 * Q_TILE), BF16),
        pltpu.VMEM((2, B_HEADS // 2, 1, 2 * Q_TILE), F32),
        pltpu.VMEM((Q_TILE, A_WIDTH + B_WIDTH), BF16),
        pltpu.VMEM((Q_TILE, D_MODEL), F32),
        pltpu.VMEM((Q_TILE, D_MODEL), BF16),
        pltpu.VMEM((Q_TILE, D_MODEL), F32),
    ]
    return pl.pallas_call(
        lambda *refs: _attn_mix_ffn_kernel(n_tiles, nq, *refs),
        grid=(n_tiles + 1,),
        in_specs=in_specs,
        out_specs=pl.BlockSpec((Q_TILE, D_MODEL), lambda t: (prev_tile(t), 0)),
        out_shape=jax.ShapeDtypeStruct((batch * seq, D_MODEL), F32),
        scratch_shapes=scratch,
        compiler_params=pltpu.CompilerParams(
            dimension_semantics=("arbitrary",), vmem_limit_bytes=VMEM_LIMIT_BYTES),
        name="attn_mix_ffn2",
    )(sinks, *attn_tiles, *attn_seqs, *attn_consts, *rows_prev, *weights)


def _prep_proj_weight(w):
    qa = w[:, 0:512]
    ka = w[:, 512:640]
    va = w[:, 640:768]
    qb, kb, vb = w[:, 768:1280], w[:, 1280:1792], w[:, 1792:2304]
    wf = w[:, 2304:2312]
    g_a, g_b = w[:, 2312:3336], w[:, 3336:4360]
    k0, k1 = ka[:, :HEAD_DIM], ka[:, HEAD_DIM:]
    ka_dup = jnp.concatenate([k0, k0, k1, k1], axis=1)
    wf_pad = jnp.pad(wf, ((0, 0), (0, LANES - B_HEADS)))
    wp = jnp.concatenate([qa, ka_dup, qb, kb, g_a, g_b, wf_pad], axis=1).astype(BF16)
    wv = jnp.concatenate([vb, va], axis=1).astype(BF16)
    return wp, wv


def _attention_tables():
    slopes = 2.0 ** -(np.arange(1, A_HEADS + 1, dtype=np.float64))
    k = np.arange(BLOCK)[:, None]
    r = np.arange(BLOCK)[None, :]
    band = np.empty((2, A_KV_HEADS, 2 * BLOCK, A_GROUP * BLOCK), np.float32)
    meta = np.empty((A_KV_HEADS, N_META, A_GROUP * BLOCK), np.float32)
    slope_rows = np.empty((A_KV_HEADS, 1, A_GROUP * BLOCK), np.float32)
    mk = np.arange(N_META)[:, None]
    for h in range(A_HEADS):
        g, j = divmod(h, A_GROUP)
        cols = slice(j * BLOCK, (j + 1) * BLOCK)
        cur = np.where(k <= r, -slopes[h] * (r - k), NEG)
        prev = np.where(k > r, -slopes[h] * (r - k + BLOCK), NEG)
        band[0, g, :, cols] = np.concatenate([cur, np.full_like(cur, NEG)], axis=0)
        band[1, g, :, cols] = np.concatenate([prev, cur], axis=0)
        meta[g, :, cols] = -slopes[h] * (r + N_META - mk)
        slope_rows[g, :, cols] = slopes[h] * BLOCK
    qext = np.zeros((B_HEADS // 2, 2 * Q_TILE, LANES), np.float32)
    for h in range(B_HEADS):
        pr, hh = divmod(h, 2)
        qext[pr, hh * Q_TILE:(hh + 1) * Q_TILE, BIAS_PIECES * h:BIAS_PIECES * (h + 1)] = 1.0
    band = np.where(band > NEG, band * LOG2E, NEG)
    tables = [t.astype(np.float32) for t in (band, meta * LOG2E, slope_rows * LOG2E)]
    return tuple(jnp.asarray(t) for t in tables) + (jnp.asarray(qext, BF16),)


def _bias_placement():
    place = np.zeros((BIAS_PIECES, LANES, LANES), np.float32)
    for h in range(B_HEADS):
        for p in range(BIAS_PIECES):
            place[p, h, BIAS_PIECES * h + p] = 1.0
    return jnp.asarray(place, BF16)


def kernel(x, meta_tokens, ffn1_norm, ffn1_w_in, ffn1_w_out, mix_norm, w_in, b_forget, attn_sinks,
           w_branch_a, w_branch_b, w_out, ffn2_norm, ffn2_w_in, ffn2_w_out, final_norm):
    batch, seq, d = x.shape
    assert d == D_MODEL and seq % TOKEN_TILE == 0 and seq % Q_TILE == 0
    assert ffn1_norm.shape[0] == 1, "single layer"

    n1 = ffn1_norm[0][None].astype(F32)
    nm = mix_norm[0][None].astype(F32)
    n2 = ffn2_norm[0][None].astype(F32)
    nf = final_norm[None].astype(F32)
    w1i, w1o = ffn1_w_in[0].astype(BF16), ffn1_w_out[0].astype(BF16)
    w2i, w2o = ffn2_w_in[0].astype(BF16), ffn2_w_out[0].astype(BF16)
    wp, wv = _prep_proj_weight(w_in[0])
    bfg = jnp.pad(b_forget[0].astype(F32), (0, LANES - B_HEADS))[None]
    wa, wb, wo = w_branch_a[0].astype(BF16), w_branch_b[0].astype(BF16), w_out[0].astype(BF16)
    place = _bias_placement()

    x2d = x.reshape(batch * seq, D_MODEL)
    h1, qa, ka, qb, kb, ga, gb, cb, vt = _ffn_proj(
        x2d, n1, w1i, w1o, nm, wp, wv.T, bfg, place, TOKEN_TILE, seq // TOKEN_TILE, False)
    _, _, kam, _, kbm, _, _, cbm, vm = _ffn_proj(
        meta_tokens.astype(F32), n1, w1i, w1o, nm, wp, wv, bfg, place, N_META, 1, True)

    out = _attn_mix_ffn(attn_sinks[0].astype(F32), (qa, qb), (ka, kb, cb, vt),
                        (kam, kbm, cbm, vm.T) + _attention_tables(), (h1, ga, gb),
                        (wa, wb, wo, n2, w2i, w2o, nf), batch, seq)
    return out.reshape(batch, seq, D_MODEL)
```

```python
import jax
import jax.numpy as jnp
import numpy as np
from jax import lax
from jax.experimental import pallas as pl
from jax.experimental.pallas import tpu as pltpu

F32 = jnp.float32
BF16 = jnp.bfloat16

D_MODEL = 1024
N_META = 16
BLOCK = 128
HEAD_DIM = 64
A_HEADS = 8
A_KV_HEADS = 2
A_GROUP = A_HEADS // A_KV_HEADS
B_HEADS = 8
A_WIDTH = A_HEADS * HEAD_DIM
B_WIDTH = B_HEADS * HEAD_DIM
V_ROWS = B_WIDTH + A_KV_HEADS * HEAD_DIM
D_FF = 2816
EPS = 1e-6
NEG = -1e30
LOG2E = 1.4426950408889634
Q_SCALE = HEAD_DIM ** -0.5 * LOG2E

LANES = 128
TOKEN_TILE = 512
Q_TILE = 256
KV_TILE = 256
FF_CHUNKS = ((0, 1024), (1024, 1024), (2048, 768))
VMEM_LIMIT_BYTES = 56 * 1024 * 1024
BIAS_PIECES = 3
ONES_ROWS = 16

_QA, _KA, _QB, _KB, _GA, _GB, _FG = 0, 512, 768, 1280, 1792, 2816, 3840
PROJ_COLS = 3968


def _dot(a, b):
    return jnp.dot(a, b, preferred_element_type=F32)


def _dot_nt(a, b):
    return lax.dot_general(a, b, (((1,), (1,)), ((), ())), preferred_element_type=F32)


def _rms(x, g):
    ms = jnp.mean(x * x, axis=-1, keepdims=True)
    return x * lax.rsqrt(ms + EPS) * g


def _swiglu(xn, w_in_ref, w_out_ref):
    acc = None
    for lo, width in FF_CHUNKS:
        g = _dot(xn, w_in_ref[:, lo:lo + width])
        u = _dot(xn, w_in_ref[:, D_FF + lo:D_FF + lo + width])
        a = (g * jax.nn.sigmoid(g) * u).astype(BF16)
        part = _dot(a, w_out_ref[lo:lo + width, :])
        acc = part if acc is None else acc + part
    return acc


def _split3(x):
    hi = x.astype(BF16)
    r = x - hi.astype(F32)
    mid = r.astype(BF16)
    lo = (r - mid.astype(F32)).astype(BF16)
    return hi, mid, lo


def _ffn_proj_kernel(tiles_per_seq, is_meta,
                     x_ref, n1_ref, w1i_ref, w1o_ref, nm_ref, wp_ref, wv_ref, bf_ref, tri_ref, place_ref,
                     h1_ref, qa_ref, ka_ref, qb_ref, kb_ref, ga_ref, gb_ref, cb_ref, v_ref, carry_ref):
    x = x_ref[...]
    tile = x.shape[0]
    xn = _rms(x, n1_ref[...]).astype(BF16)
    h1 = x + 0.5 * _swiglu(xn, w1i_ref, w1o_ref)
    h1_ref[...] = h1
    u = _rms(h1, nm_ref[...]).astype(BF16)

    def proj(lo, width):
        return _dot(u, wp_ref[:, lo:lo + width])

    qa_ref[...] = (proj(_QA, 512) * Q_SCALE).astype(BF16)
    ka_ref[...] = proj(_KA, 256).astype(BF16)
    qb_ref[...] = (proj(_QB, 512) * Q_SCALE).astype(BF16)
    kb_ref[...] = proj(_KB, 512).astype(BF16)
    ga_ref[...] = jax.nn.sigmoid(proj(_GA, 1024)).astype(BF16)
    gb_ref[...] = jax.nn.sigmoid(proj(_GB, 1024)).astype(BF16)

    if is_meta:
        v_ref[...] = _dot(u, wv_ref[...]).astype(BF16)
    else:
        vt = _dot_nt(wv_ref[...], u).astype(BF16)
        for c_blk in range(tile // LANES):
            v_ref[c_blk] = vt[:, c_blk * LANES:(c_blk + 1) * LANES]

    z = proj(_FG, LANES) + bf_ref[...]
    lane = lax.broadcasted_iota(jnp.int32, (1, LANES), 1)
    lf = jnp.where(lane < B_HEADS, jnp.minimum(z, 0.0) - jnp.log1p(jnp.exp(-jnp.abs(z))), 0.0)

    @pl.when(pl.program_id(0) % tiles_per_seq == 0)
    def _():
        carry_ref[...] = jnp.zeros_like(carry_ref)

    tri = tri_ref[...]
    c = carry_ref[...] + sum(_dot(tri, piece) for piece in _split3(lf))
    carry_ref[...] = c[tile - 1:tile, :]
    bias = ((c[tile - 1:tile, :] - c) if is_meta else -c) * LOG2E
    cb_ref[...] = sum(_dot(piece, place_ref[p]) for p, piece in enumerate(_split3(bias))).astype(BF16)


def _const_spec(shape):
    zeros = (0,) * len(shape)
    return pl.BlockSpec(shape, lambda *_: zeros, pipeline_mode=pl.Buffered(1))


def _ffn_proj(x2d, n1, w1i, w1o, nm, wp, wv, bfg, place, tile, tiles_per_seq, is_meta):
    rows = x2d.shape[0]
    grid = (rows // tile,)
    tri = jnp.asarray(np.tril(np.ones((tile, tile), np.float32)), BF16)

    def row_spec(cols):
        return pl.BlockSpec((tile, cols), lambda i: (i, 0))

    out_cols = (D_MODEL, 512, 256, 512, 512, 1024, 1024, LANES)
    out_dtypes = (F32,) + (BF16,) * 7
    out_shape = [jax.ShapeDtypeStruct((rows, c), dt) for c, dt in zip(out_cols, out_dtypes)]
    out_specs = [row_spec(c) for c in out_cols]
    if is_meta:
        out_shape.append(jax.ShapeDtypeStruct((rows, V_ROWS), BF16))
        out_specs.append(row_spec(V_ROWS))
    else:
        out_shape.append(jax.ShapeDtypeStruct((rows // LANES, V_ROWS, LANES), BF16))
        out_specs.append(pl.BlockSpec((tile // LANES, V_ROWS, LANES), lambda i: (i, 0, 0)))
    consts = (n1, w1i, w1o, nm, wp, wv, bfg, tri, place)
    in_specs = [row_spec(D_MODEL)] + [_const_spec(a.shape) for a in consts]
    return pl.pallas_call(
        lambda *refs: _ffn_proj_kernel(tiles_per_seq, is_meta, *refs),
        grid=grid,
        in_specs=in_specs,
        out_specs=out_specs,
        out_shape=out_shape,
        scratch_shapes=[pltpu.VMEM((1, LANES), F32)],
        compiler_params=pltpu.CompilerParams(
            dimension_semantics=("arbitrary",), vmem_limit_bytes=VMEM_LIMIT_BYTES),
        name="ffn1_proj_meta" if is_meta else "ffn1_proj",
    )(x2d, *consts)


def _split_heads(q, low):
    zero = jnp.zeros_like(q)
    return jnp.concatenate([jnp.where(low, q, zero), jnp.where(low, zero, q)], axis=0)


def _attn_kernel(sinks_ref, qa_ref, ka_ref, qb_ref, kb_ref, cb_ref, vt_ref,
                 kam_ref, kbm_ref, cbm_ref, vtm_ref, band_ref, mtab_ref, slope_ref, qext_ref,
                 oa_ref, ob_ref, m_ref, acc_ref):
    i = pl.program_id(1)
    low = lax.broadcasted_iota(jnp.int32, (1, LANES), 1) < HEAD_DIM
    sub = Q_TILE // BLOCK
    ones_meta = jnp.ones((ONES_ROWS, N_META), BF16)

    def v_tile(blk, n_blk, row0):
        vals = jnp.concatenate([vt_ref[blk + b, row0:row0 + HEAD_DIM, :] for b in range(n_blk)], axis=1)
        return jnp.concatenate([vals, jnp.ones((ONES_ROWS, n_blk * BLOCK), BF16)], axis=0)

    def v_meta(row0):
        return jnp.concatenate([vtm_ref[row0:row0 + HEAD_DIM, :], ones_meta], axis=0)

    units = [(sb, g) for sb in range(sub) for g in range(A_KV_HEADS)]
    blk0s, s_bs, s_ms = {}, {}, {}
    for sb, g in units:
        n = i * sub + sb
        blk0s[sb] = jnp.maximum(n - 1, 0)
        start = pl.multiple_of(blk0s[sb] * BLOCK, BLOCK)
        q = qa_ref[sb * BLOCK:(sb + 1) * BLOCK, :]
        qq = jnp.concatenate(
            [_split_heads(q[:, (2 * g + p) * LANES:(2 * g + p + 1) * LANES], low) for p in range(2)], axis=0)
        kband = ka_ref[pl.ds(start, 2 * BLOCK), g * LANES:(g + 1) * LANES]
        kmeta = kam_ref[:, g * LANES:(g + 1) * LANES]
        s_bs[sb, g] = _dot_nt(kband, qq) + band_ref[jnp.minimum(n, 1), g]
        s_ms[sb, g] = _dot_nt(kmeta, qq) + mtab_ref[g] - n.astype(F32) * slope_ref[g]
    p_bs, p_ms, sink_ps = {}, {}, {}
    for sb, g in units:
        s_b, s_m = s_bs[sb, g], s_ms[sb, g]
        sink = jnp.concatenate(
            [jnp.full((1, BLOCK), sinks_ref[g * A_GROUP + j] * LOG2E, F32) for j in range(A_GROUP)], axis=1)
        m = jnp.maximum(jnp.max(s_b, axis=0, keepdims=True), jnp.max(s_m, axis=0, keepdims=True))
        m = jnp.maximum(m, sink)
        p_bs[sb, g] = jnp.exp2(s_b - m).astype(BF16)
        p_ms[sb, g] = jnp.exp2(s_m - m).astype(BF16)
        sink_ps[sb, g] = jnp.exp2(sink - m)
    for sb, g in units:
        row0 = B_WIDTH + g * HEAD_DIM
        o_aug = _dot(v_tile(blk0s[sb], 2, row0), p_bs[sb, g]) + _dot(v_meta(row0), p_ms[sb, g])
        o_t = o_aug[:HEAD_DIM] / (o_aug[HEAD_DIM:HEAD_DIM + 1] + sink_ps[sb, g])
        for p in range(2):
            pair_t = jnp.concatenate([o_t[:, (2 * p) * BLOCK:(2 * p + 1) * BLOCK],
                                      o_t[:, (2 * p + 1) * BLOCK:(2 * p + 2) * BLOCK]], axis=0)
            oa_ref[sb * BLOCK:(sb + 1) * BLOCK, (2 * g + p) * LANES:(2 * g + p + 1) * LANES] = pair_t.T.astype(BF16)

    n_pairs = B_HEADS // 2
    q_aug = []
    for pr in range(n_pairs):
        qq = _split_heads(qb_ref[:, pr * LANES:(pr + 1) * LANES], low)
        q_aug.append(jnp.concatenate([qq, qext_ref[pr]], axis=1))

    row = lax.broadcasted_iota(jnp.int32, (KV_TILE, Q_TILE), 0)
    col = lax.broadcasted_iota(jnp.int32, (KV_TILE, Q_TILE), 1)
    causal1 = jnp.where(row <= col, 0.0, NEG)
    causal = jnp.concatenate([causal1, causal1], axis=1)

    scores = []
    for pr in range(n_pairs):
        k_aug = jnp.concatenate([kbm_ref[:, pr * LANES:(pr + 1) * LANES], cbm_ref[...]], axis=1)
        scores.append(_dot_nt(k_aug, q_aug[pr]))
    probs = []
    for pr in range(n_pairs):
        m0 = jnp.max(scores[pr], axis=0, keepdims=True)
        m_ref[pr] = m0
        probs.append(jnp.exp2(scores[pr] - m0).astype(BF16))
    for pr in range(n_pairs):
        for hh in range(2):
            h = 2 * pr + hh
            acc_ref[h] = _dot(v_meta(h * HEAD_DIM), probs[pr][:, hh * Q_TILE:(hh + 1) * Q_TILE])

    def update(blk, n_blk, mask):
        k0 = pl.multiple_of(blk * BLOCK, KV_TILE)
        rows = pl.ds(k0, n_blk * BLOCK)
        cb = cb_ref[rows, :]
        scores = []
        for pr in range(n_pairs):
            k_aug = jnp.concatenate([kb_ref[rows, pr * LANES:(pr + 1) * LANES], cb], axis=1)
            scores.append(_dot_nt(k_aug, q_aug[pr]))
        probs, alphas = [], []
        for pr in range(n_pairs):
            s = scores[pr]
            if mask is not None:
                s = s + mask
            m_old = m_ref[pr]
            m_new = jnp.maximum(m_old, jnp.max(s, axis=0, keepdims=True))
            m_ref[pr] = m_new
            alphas.append(jnp.exp2(m_old - m_new))
            probs.append(jnp.exp2(s - m_new).astype(BF16))
        for pr in range(n_pairs):
            pb, alpha = probs[pr], alphas[pr]
            for hh in range(2):
                h = 2 * pr + hh
                qs = slice(hh * Q_TILE, (hh + 1) * Q_TILE)
                acc_ref[h] = alpha[:, qs] * acc_ref[h] + _dot(v_tile(blk, n_blk, h * HEAD_DIM), pb[:, qs])

    per_tile = KV_TILE // BLOCK

    def body(j, carry):
        update(2 * per_tile * j, 2 * per_tile, None)
        return carry

    lax.fori_loop(0, lax.shift_right_logical(i, 1), body, 0)

    @pl.when(lax.rem(i, 2) == 1)
    def _():
        update(per_tile * (i - 1), 2 * per_tile, jnp.concatenate([jnp.zeros_like(causal), causal], axis=0))

    @pl.when(lax.rem(i, 2) == 0)
    def _():
        update(per_tile * i, per_tile, causal)

    for pr in range(n_pairs):
        acc0, acc1 = acc_ref[2 * pr], acc_ref[2 * pr + 1]
        o_t = jnp.concatenate([acc0[:HEAD_DIM] / acc0[HEAD_DIM:HEAD_DIM + 1],
                               acc1[:HEAD_DIM] / acc1[HEAD_DIM:HEAD_DIM + 1]], axis=0)
        ob_ref[:, pr * LANES:(pr + 1) * LANES] = o_t.T.astype(BF16)


def _attention(sinks, qa, ka, qb, kb, cb, vt, kam, kbm, cbm, vtm, tables, batch, seq):
    nq = seq // Q_TILE
    grid = (batch, nq)

    def q_spec(cols):
        return pl.BlockSpec((Q_TILE, cols), lambda b, i: (b * nq + i, 0))

    def kv_spec(cols):
        return pl.BlockSpec((seq, cols), lambda b, i: (b, 0))

    in_specs = [
        pl.BlockSpec(memory_space=pltpu.SMEM),
        q_spec(A_WIDTH), kv_spec(256), q_spec(B_WIDTH), kv_spec(B_WIDTH), kv_spec(LANES),
        pl.BlockSpec((seq // LANES, V_ROWS, LANES), lambda b, i: (b, 0, 0)),
    ]
    in_specs += [_const_spec(a.shape) for a in (kam, kbm, cbm, vtm) + tuple(tables)]
    out_shape = [jax.ShapeDtypeStruct((batch * seq, A_WIDTH), BF16),
                 jax.ShapeDtypeStruct((batch * seq, B_WIDTH), BF16)]
    out_specs = [q_spec(A_WIDTH), q_spec(B_WIDTH)]
    scratch = [
        pltpu.VMEM((B_HEADS // 2, 1, 2 * Q_TILE), F32),
        pltpu.VMEM((B_HEADS, HEAD_DIM + ONES_ROWS, Q_TILE), F32),
    ]
    return pl.pallas_call(
        _attn_kernel,
        grid=grid,
        in_specs=in_specs,
        out_specs=out_specs,
        out_shape=out_shape,
        scratch_shapes=scratch,
        compiler_params=pltpu.CompilerParams(
            dimension_semantics=("arbitrary", "arbitrary"), vmem_limit_bytes=VMEM_LIMIT_BYTES),
        name="attention",
    )(sinks, qa, ka, qb, kb, cb, vt, kam, kbm, cbm, vtm, *tables)


def _mix_ffn_kernel(h1_ref, oa_ref, ob_ref, ga_ref, gb_ref, wa_ref, wb_ref, wo_ref,
                    n2_ref, w2i_ref, w2o_ref, nf_ref, out_ref):
    ya = _dot(oa_ref[...], wa_ref[...])
    yb = _dot(ob_ref[...], wb_ref[...])
    mixed = (ga_ref[...].astype(F32) * ya + gb_ref[...].astype(F32) * yb).astype(BF16)
    h2 = h1_ref[...] + _dot(mixed, wo_ref[...])
    xn = _rms(h2, n2_ref[...]).astype(BF16)
    h3 = h2 + 0.5 * _swiglu(xn, w2i_ref, w2o_ref)
    out_ref[...] = _rms(h3, nf_ref[...])


def _mix_ffn(h1, oa, ob, ga, gb, wa, wb, wo, n2, w2i, w2o, nf, tile):
    rows = h1.shape[0]

    def row_spec(cols):
        return pl.BlockSpec((tile, cols), lambda i: (i, 0))

    in_specs = [row_spec(D_MODEL), row_spec(A_WIDTH), row_spec(B_WIDTH), row_spec(D_MODEL), row_spec(D_MODEL)]
    in_specs += [_const_spec(w.shape) for w in (wa, wb, wo, n2, w2i, w2o, nf)]
    return pl.pallas_call(
        _mix_ffn_kernel,
        grid=(rows // tile,),
        in_specs=in_specs,
        out_specs=row_spec(D_MODEL),
        out_shape=jax.ShapeDtypeStruct((rows, D_MODEL), F32),
        compiler_params=pltpu.CompilerParams(
            dimension_semantics=("arbitrary",), vmem_limit_bytes=VMEM_LIMIT_BYTES),
        name="mix_ffn2",
    )(h1, oa, ob, ga, gb, wa, wb, wo, n2, w2i, w2o, nf)


def _prep_proj_weight(w):
    qa = w[:, 0:512]
    ka = w[:, 512:640]
    va = w[:, 640:768]
    qb, kb, vb = w[:, 768:1280], w[:, 1280:1792], w[:, 1792:2304]
    wf = w[:, 2304:2312]
    g_a, g_b = w[:, 2312:3336], w[:, 3336:4360]
    k0, k1 = ka[:, :HEAD_DIM], ka[:, HEAD_DIM:]
    ka_dup = jnp.concatenate([k0, k0, k1, k1], axis=1)
    wf_pad = jnp.pad(wf, ((0, 0), (0, LANES - B_HEADS)))
    wp = jnp.concatenate([qa, ka_dup, qb, kb, g_a, g_b, wf_pad], axis=1).astype(BF16)
    wv = jnp.concatenate([vb, va], axis=1).astype(BF16)
    return wp, wv


def _attention_tables():
    slopes = 2.0 ** -(np.arange(1, A_HEADS + 1, dtype=np.float64))
    k = np.arange(BLOCK)[:, None]
    r = np.arange(BLOCK)[None, :]
    band = np.empty((2, A_KV_HEADS, 2 * BLOCK, A_GROUP * BLOCK), np.float32)
    meta = np.empty((A_KV_HEADS, N_META, A_GROUP * BLOCK), np.float32)
    slope_rows = np.empty((A_KV_HEADS, 1, A_GROUP * BLOCK), np.float32)
    mk = np.arange(N_META)[:, None]
    for h in range(A_HEADS):
        g, j = divmod(h, A_GROUP)
        cols = slice(j * BLOCK, (j + 1) * BLOCK)
        cur = np.where(k <= r, -slopes[h] * (r - k), NEG)
        prev = np.where(k > r, -slopes[h] * (r - k + BLOCK), NEG)
        band[0, g, :, cols] = np.concatenate([cur, np.full_like(cur, NEG)], axis=0)
        band[1, g, :, cols] = np.concatenate([prev, cur], axis=0)
        meta[g, :, cols] = -slopes[h] * (r + N_META - mk)
        slope_rows[g, :, cols] = slopes[h] * BLOCK
    qext = np.zeros((B_HEADS // 2, 2 * Q_TILE, LANES), np.float32)
    for h in range(B_HEADS):
        pr, hh = divmod(h, 2)
        qext[pr, hh * Q_TILE:(hh + 1) * Q_TILE, BIAS_PIECES * h:BIAS_PIECES * (h + 1)] = 1.0
    band = np.where(band > NEG, band * LOG2E, NEG)
    tables = [t.astype(np.float32) for t in (band, meta * LOG2E, slope_rows * LOG2E)]
    return tuple(jnp.asarray(t) for t in tables) + (jnp.asarray(qext, BF16),)


def _bias_placement():
    place = np.zeros((BIAS_PIECES, LANES, LANES), np.float32)
    for h in range(B_HEADS):
        for p in range(BIAS_PIECES):
            place[p, h, BIAS_PIECES * h + p] = 1.0
    return jnp.asarray(place, BF16)


def kernel(x, meta_tokens, ffn1_norm, ffn1_w_in, ffn1_w_out, mix_norm, w_in, b_forget, attn_sinks,
           w_branch_a, w_branch_b, w_out, ffn2_norm, ffn2_w_in, ffn2_w_out, final_norm):
    batch, seq, d = x.shape
    assert d == D_MODEL and seq % TOKEN_TILE == 0 and seq % Q_TILE == 0
    assert ffn1_norm.shape[0] == 1, "single layer"

    n1 = ffn1_norm[0][None].astype(F32)
    nm = mix_norm[0][None].astype(F32)
    n2 = ffn2_norm[0][None].astype(F32)
    nf = final_norm[None].astype(F32)
    w1i, w1o = ffn1_w_in[0].astype(BF16), ffn1_w_out[0].astype(BF16)
    w2i, w2o = ffn2_w_in[0].astype(BF16), ffn2_w_out[0].astype(BF16)
    wp, wv = _prep_proj_weight(w_in[0])
    bfg = jnp.pad(b_forget[0].astype(F32), (0, LANES - B_HEADS))[None]
    wa, wb, wo = w_branch_a[0].astype(BF16), w_branch_b[0].astype(BF16), w_out[0].astype(BF16)
    place = _bias_placement()

    x2d = x.reshape(batch * seq, D_MODEL)
    h1, qa, ka, qb, kb, ga, gb, cb, vt = _ffn_proj(
        x2d, n1, w1i, w1o, nm, wp, wv.T, bfg, place, TOKEN_TILE, seq // TOKEN_TILE, False)
    _, _, kam, _, kbm, _, _, cbm, vm = _ffn_proj(
        meta_tokens.astype(F32), n1, w1i, w1o, nm, wp, wv, bfg, place, N_META, 1, True)

    oa, ob = _attention(attn_sinks[0].astype(F32), qa, ka, qb, kb, cb, vt, kam, kbm, cbm, vm.T,
                        _attention_tables(), batch, seq)
    out = _mix_ffn(h1, oa, ob, ga, gb, wa, wb, wo, n2, w2i, w2o, nf, TOKEN_TILE)
    return out.reshape(batch, seq, D_MODEL)
```

```python
import jax
import jax.numpy as jnp
import numpy as np
from jax import lax
from jax.experimental import pallas as pl
from jax.experimental.pallas import tpu as pltpu

F32 = jnp.float32
BF16 = jnp.bfloat16

D_MODEL = 1024
N_META = 16
BLOCK = 128
HEAD_DIM = 64
A_HEADS = 8
A_KV_HEADS = 2
A_GROUP = A_HEADS // A_KV_HEADS
B_HEADS = 8
A_WIDTH = A_HEADS * HEAD_DIM
B_WIDTH = B_HEADS * HEAD_DIM
V_ROWS = B_WIDTH + A_KV_HEADS * HEAD_DIM
D_FF = 2816
EPS = 1e-6
NEG = -1e30
LOG2E = 1.4426950408889634
Q_SCALE = HEAD_DIM ** -0.5 * LOG2E

LANES = 128
TOKEN_TILE = 512
Q_TILE = 256
KV_TILE = 256
TILES_PER_TRIP = 2
FF_CHUNKS = ((0, 1024), (1024, 1024), (2048, 768))
VMEM_LIMIT_BYTES = 56 * 1024 * 1024
BIAS_PIECES = 3
ONES_ROWS = 16

_QA, _KA, _QB, _KB, _GA, _GB, _FG = 0, 512, 768, 1280, 1792, 2816, 3840
PROJ_COLS = 3968


def _dot(a, b):
    return jnp.dot(a, b, preferred_element_type=F32)


def _dot_nt(a, b):
    return lax.dot_general(a, b, (((1,), (1,)), ((), ())), preferred_element_type=F32)


def _rms(x, g):
    ms = jnp.mean(x * x, axis=-1, keepdims=True)
    return x * lax.rsqrt(ms + EPS) * g


def _swiglu(xn, w_in_ref, w_out_ref):
    acc = None
    for lo, width in FF_CHUNKS:
        g = _dot(xn, w_in_ref[:, lo:lo + width])
        u = _dot(xn, w_in_ref[:, D_FF + lo:D_FF + lo + width])
        a = (g * jax.nn.sigmoid(g) * u).astype(BF16)
        part = _dot(a, w_out_ref[lo:lo + width, :])
        acc = part if acc is None else acc + part
    return acc


def _split3(x):
    hi = x.astype(BF16)
    r = x - hi.astype(F32)
    mid = r.astype(BF16)
    lo = (r - mid.astype(F32)).astype(BF16)
    return hi, mid, lo


def _ffn_proj_kernel(tiles_per_seq, is_meta,
                     x_ref, n1_ref, w1i_ref, w1o_ref, nm_ref, wp_ref, wv_ref, bf_ref, tri_ref, place_ref,
                     h1_ref, qa_ref, ka_ref, qb_ref, kb_ref, ga_ref, gb_ref, cb_ref, v_ref, carry_ref):
    @pl.when(pl.program_id(0) % tiles_per_seq == 0)
    def _():
        carry_ref[...] = jnp.zeros_like(carry_ref)

    x = x_ref[...]
    tile = x.shape[0]
    xn = _rms(x, n1_ref[...]).astype(BF16)
    h1 = x + 0.5 * _swiglu(xn, w1i_ref, w1o_ref)
    h1_ref[...] = h1
    u = _rms(h1, nm_ref[...]).astype(BF16)

    def proj(lo, width):
        return _dot(u, wp_ref[:, lo:lo + width])

    qa_ref[...] = (proj(_QA, 512) * Q_SCALE).astype(BF16)
    ka_ref[...] = proj(_KA, 256).astype(BF16)
    qb_ref[...] = (proj(_QB, 512) * Q_SCALE).astype(BF16)
    kb_ref[...] = proj(_KB, 512).astype(BF16)

    z = proj(_FG, LANES) + bf_ref[...]
    lane = lax.broadcasted_iota(jnp.int32, (1, LANES), 1)
    lf = jnp.where(lane < B_HEADS, jnp.minimum(z, 0.0) - jnp.log1p(jnp.exp(-jnp.abs(z))), 0.0)
    tri = tri_ref[...]
    c = carry_ref[...] + sum(_dot(tri, piece) for piece in _split3(lf))
    carry_ref[...] = c[tile - 1:tile, :]
    bias = ((c[tile - 1:tile, :] - c) if is_meta else -c) * LOG2E
    cb_ref[...] = sum(_dot(piece, place_ref[p]) for p, piece in enumerate(_split3(bias))).astype(BF16)

    ga_ref[...] = jax.nn.sigmoid(proj(_GA, 1024)).astype(BF16)
    gb_ref[...] = jax.nn.sigmoid(proj(_GB, 1024)).astype(BF16)

    if is_meta:
        v_ref[...] = _dot(u, wv_ref[...]).astype(BF16)
    else:
        vt = _dot_nt(wv_ref[...], u).astype(BF16)
        for c_blk in range(tile // LANES):
            v_ref[c_blk] = vt[:, c_blk * LANES:(c_blk + 1) * LANES]


def _const_spec(shape):
    zeros = (0,) * len(shape)
    return pl.BlockSpec(shape, lambda *_: zeros, pipeline_mode=pl.Buffered(1))


def _ffn_proj(x2d, n1, w1i, w1o, nm, wp, wv, bfg, place, tile, tiles_per_seq, is_meta):
    rows = x2d.shape[0]
    grid = (rows // tile,)
    tri = jnp.asarray(np.tril(np.ones((tile, tile), np.float32)), BF16)

    def row_spec(cols):
        return pl.BlockSpec((tile, cols), lambda i: (i, 0))

    out_cols = (D_MODEL, 512, 256, 512, 512, 1024, 1024, LANES)
    out_dtypes = (F32,) + (BF16,) * 7
    out_shape = [jax.ShapeDtypeStruct((rows, c), dt) for c, dt in zip(out_cols, out_dtypes)]
    out_specs = [row_spec(c) for c in out_cols]
    if is_meta:
        out_shape.append(jax.ShapeDtypeStruct((rows, V_ROWS), BF16))
        out_specs.append(row_spec(V_ROWS))
    else:
        out_shape.append(jax.ShapeDtypeStruct((rows // LANES, V_ROWS, LANES), BF16))
        out_specs.append(pl.BlockSpec((tile // LANES, V_ROWS, LANES), lambda i: (i, 0, 0)))
    consts = (n1, w1i, w1o, nm, wp, wv, bfg, tri, place)
    in_specs = [row_spec(D_MODEL)] + [_const_spec(a.shape) for a in consts]
    return pl.pallas_call(
        lambda *refs: _ffn_proj_kernel(tiles_per_seq, is_meta, *refs),
        grid=grid,
        in_specs=in_specs,
        out_specs=out_specs,
        out_shape=out_shape,
        scratch_shapes=[pltpu.VMEM((1, LANES), F32)],
        compiler_params=pltpu.CompilerParams(
            dimension_semantics=("arbitrary",), vmem_limit_bytes=VMEM_LIMIT_BYTES),
        name="ffn1_proj_meta" if is_meta else "ffn1_proj",
    )(x2d, *consts)


def _split_heads(q, low):
    zero = jnp.zeros_like(q)
    return jnp.concatenate([jnp.where(low, q, zero), jnp.where(low, zero, q)], axis=0)


def _attn_kernel(sinks_ref, qa_ref, ka_ref, qb_ref, kb_ref, cb_ref, vt_ref,
                 kam_ref, kbm_ref, cbm_ref, vtm_ref, band_ref, mtab_ref, slope_ref,
                 oa_ref, ob_ref, m_ref, acc_ref):
    i = pl.program_id(1)
    low = lax.broadcasted_iota(jnp.int32, (1, LANES), 1) < HEAD_DIM
    sub = Q_TILE // BLOCK
    ones_meta = jnp.ones((ONES_ROWS, N_META), BF16)

    def v_tile(blk, n_blk, row0):
        vals = jnp.concatenate([vt_ref[blk + b, row0:row0 + HEAD_DIM, :] for b in range(n_blk)], axis=1)
        return jnp.concatenate([vals, jnp.ones((ONES_ROWS, n_blk * BLOCK), BF16)], axis=0)

    def v_meta(row0):
        return jnp.concatenate([vtm_ref[row0:row0 + HEAD_DIM, :], ones_meta], axis=0)

    units = [(sb, g) for sb in range(sub) for g in range(A_KV_HEADS)]
    blk0s, s_bs, s_ms = {}, {}, {}
    for sb, g in units:
        n = i * sub + sb
        blk0s[sb] = jnp.maximum(n - 1, 0)
        start = pl.multiple_of(blk0s[sb] * BLOCK, BLOCK)
        q = qa_ref[sb * BLOCK:(sb + 1) * BLOCK, :]
        qq = jnp.concatenate(
            [_split_heads(q[:, (2 * g + p) * LANES:(2 * g + p + 1) * LANES], low) for p in range(2)], axis=0)
        kband = ka_ref[pl.ds(start, 2 * BLOCK), g * LANES:(g + 1) * LANES]
        kmeta = kam_ref[:, g * LANES:(g + 1) * LANES]
        s_bs[sb, g] = _dot_nt(kband, qq) + band_ref[jnp.minimum(n, 1), g]
        s_ms[sb, g] = _dot_nt(kmeta, qq) + mtab_ref[g] - n.astype(F32) * slope_ref[g]
    p_bs, p_ms, sink_ps = {}, {}, {}
    for sb, g in units:
        s_b, s_m = s_bs[sb, g], s_ms[sb, g]
        sink = jnp.concatenate(
            [jnp.full((1, BLOCK), sinks_ref[g * A_GROUP + j] * LOG2E, F32) for j in range(A_GROUP)], axis=1)
        m = jnp.maximum(jnp.max(s_b, axis=0, keepdims=True), jnp.max(s_m, axis=0, keepdims=True))
        m = jnp.maximum(m, sink)
        p_bs[sb, g] = jnp.exp2(s_b - m).astype(BF16)
        p_ms[sb, g] = jnp.exp2(s_m - m).astype(BF16)
        sink_ps[sb, g] = jnp.exp2(sink - m)
    for sb, g in units:
        row0 = B_WIDTH + g * HEAD_DIM
        o_aug = _dot(v_tile(blk0s[sb], 2, row0), p_bs[sb, g]) + _dot(v_meta(row0), p_ms[sb, g])
        o_t = o_aug[:HEAD_DIM] / (o_aug[HEAD_DIM:HEAD_DIM + 1] + sink_ps[sb, g])
        for p in range(2):
            pair_t = jnp.concatenate([o_t[:, (2 * p) * BLOCK:(2 * p + 1) * BLOCK],
                                      o_t[:, (2 * p + 1) * BLOCK:(2 * p + 2) * BLOCK]], axis=0)
            oa_ref[sb * BLOCK:(sb + 1) * BLOCK, (2 * g + p) * LANES:(2 * g + p + 1) * LANES] = pair_t.T.astype(BF16)

    lane_row = lax.broadcasted_iota(jnp.int32, (1, LANES), 1)
    heads = range(B_HEADS)

    def own_half(h):
        return low if h % 2 == 0 else ~low

    def pair_cols(h):
        return slice((h // 2) * LANES, (h // 2 + 1) * LANES)

    def aug(own, other, h):
        return jnp.where(own_half(h), own, other)

    q_aug = []
    for h in heads:
        first = _bias_lane(h, 0)
        bias_ones = jnp.where((lane_row >= first) & (lane_row < first + BIAS_PIECES), 1.0, 0.0).astype(BF16)
        q_aug.append(aug(qb_ref[:, pair_cols(h)], bias_ones, h))

    row = lax.broadcasted_iota(jnp.int32, (KV_TILE, Q_TILE), 0)
    col = lax.broadcasted_iota(jnp.int32, (KV_TILE, Q_TILE), 1)
    causal = jnp.where(row <= col, 0.0, NEG)

    cbm = cbm_ref[...]
    scores = [_dot_nt(aug(kbm_ref[:, pair_cols(h)], cbm, h), q_aug[h]) for h in heads]
    probs = []
    for h in heads:
        m0 = jnp.max(scores[h], axis=0, keepdims=True)
        m_ref[h] = m0
        probs.append(jnp.exp2(scores[h] - m0).astype(BF16))
    for h in heads:
        acc_ref[h] = _dot(v_meta(h * HEAD_DIM), probs[h])

    def update(blk, n_blk, mask):
        k0 = pl.multiple_of(blk * BLOCK, KV_TILE)
        rows = pl.ds(k0, n_blk * BLOCK)
        cb = cb_ref[rows, :]
        scores = [_dot_nt(aug(kb_ref[rows, pair_cols(h)], cb, h), q_aug[h]) for h in heads]
        probs, alphas = [], []
        for h in heads:
            s = scores[h]
            if mask is not None:
                s = s + mask
            m_old = m_ref[h]
            m_new = jnp.maximum(m_old, jnp.max(s, axis=0, keepdims=True))
            m_ref[h] = m_new
            alphas.append(jnp.exp2(m_old - m_new))
            probs.append(jnp.exp2(s - m_new).astype(BF16))
        for h in heads:
            acc_ref[h] = alphas[h] * acc_ref[h] + _dot(v_tile(blk, n_blk, h * HEAD_DIM), probs[h])

    per_tile = KV_TILE // BLOCK

    def body(j, carry):
        update(TILES_PER_TRIP * per_tile * j, TILES_PER_TRIP * per_tile, None)
        return carry

    lax.fori_loop(0, lax.div(i, TILES_PER_TRIP), body, 0)

    def finalize():
        for pr in range(B_HEADS // 2):
            acc0, acc1 = acc_ref[2 * pr], acc_ref[2 * pr + 1]
            o_t = jnp.concatenate([acc0[:HEAD_DIM] / acc0[HEAD_DIM:HEAD_DIM + 1],
                                   acc1[:HEAD_DIM] / acc1[HEAD_DIM:HEAD_DIM + 1]], axis=0)
            ob_ref[:, pr * LANES:(pr + 1) * LANES] = o_t.T.astype(BF16)

    for left in range(TILES_PER_TRIP):
        @pl.when(lax.rem(i, TILES_PER_TRIP) == left)
        def _():
            mask = jnp.concatenate([jnp.zeros((left * KV_TILE, Q_TILE), F32), causal], axis=0) if left else causal
            update(per_tile * (i - left), (left + 1) * per_tile, mask)
            finalize()


def _attention(sinks, qa, ka, qb, kb, cb, vt, kam, kbm, cbm, vtm, tables, batch, seq):
    nq = seq // Q_TILE
    grid = (batch, nq)

    def q_spec(cols):
        return pl.BlockSpec((Q_TILE, cols), lambda b, i: (b * nq + i, 0))

    def kv_spec(cols):
        return pl.BlockSpec((seq, cols), lambda b, i: (b, 0))

    in_specs = [
        pl.BlockSpec(memory_space=pltpu.SMEM),
        q_spec(A_WIDTH), kv_spec(256), q_spec(B_WIDTH), kv_spec(B_WIDTH), kv_spec(LANES),
        pl.BlockSpec((seq // LANES, V_ROWS, LANES), lambda b, i: (b, 0, 0)),
    ]
    in_specs += [_const_spec(a.shape) for a in (kam, kbm, cbm, vtm) + tuple(tables)]
    out_shape = [jax.ShapeDtypeStruct((batch * seq, A_WIDTH), BF16),
                 jax.ShapeDtypeStruct((batch * seq, B_WIDTH), BF16)]
    out_specs = [q_spec(A_WIDTH), q_spec(B_WIDTH)]
    scratch = [
        pltpu.VMEM((B_HEADS, 1, Q_TILE), F32),
        pltpu.VMEM((B_HEADS, HEAD_DIM + ONES_ROWS, Q_TILE), F32),
    ]
    return pl.pallas_call(
        _attn_kernel,
        grid=grid,
        in_specs=in_specs,
        out_specs=out_specs,
        out_shape=out_shape,
        scratch_shapes=scratch,
        compiler_params=pltpu.CompilerParams(
            dimension_semantics=("arbitrary", "arbitrary"), vmem_limit_bytes=VMEM_LIMIT_BYTES),
        name="attention",
    )(sinks, qa, ka, qb, kb, cb, vt, kam, kbm, cbm, vtm, *tables)


def _mix_ffn_kernel(h1_ref, oa_ref, ob_ref, ga_ref, gb_ref, wa_ref, wb_ref, wo_ref,
                    n2_ref, w2i_ref, w2o_ref, nf_ref, out_ref):
    ya = _dot(oa_ref[...], wa_ref[...])
    yb = _dot(ob_ref[...], wb_ref[...])
    mixed = (ga_ref[...].astype(F32) * ya + gb_ref[...].astype(F32) * yb).astype(BF16)
    h2 = h1_ref[...] + _dot(mixed, wo_ref[...])
    xn = _rms(h2, n2_ref[...]).astype(BF16)
    h3 = h2 + 0.5 * _swiglu(xn, w2i_ref, w2o_ref)
    out_ref[...] = _rms(h3, nf_ref[...])


def _mix_ffn(h1, oa, ob, ga, gb, wa, wb, wo, n2, w2i, w2o, nf, tile):
    rows = h1.shape[0]

    def row_spec(cols):
        return pl.BlockSpec((tile, cols), lambda i: (i, 0))

    in_specs = [row_spec(D_MODEL), row_spec(A_WIDTH), row_spec(B_WIDTH), row_spec(D_MODEL), row_spec(D_MODEL)]
    in_specs += [_const_spec(w.shape) for w in (wa, wb, wo, n2, w2i, w2o, nf)]
    return pl.pallas_call(
        _mix_ffn_kernel,
        grid=(rows // tile,),
        in_specs=in_specs,
        out_specs=row_spec(D_MODEL),
        out_shape=jax.ShapeDtypeStruct((rows, D_MODEL), F32),
        compiler_params=pltpu.CompilerParams(
            dimension_semantics=("arbitrary",), vmem_limit_bytes=VMEM_LIMIT_BYTES),
        name="mix_ffn2",
    )(h1, oa, ob, ga, gb, wa, wb, wo, n2, w2i, w2o, nf)


def _prep_proj_weight(w):
    qa = w[:, 0:512]
    ka = w[:, 512:640]
    va = w[:, 640:768]
    qb, kb, vb = w[:, 768:1280], w[:, 1280:1792], w[:, 1792:2304]
    wf = w[:, 2304:2312]
    g_a, g_b = w[:, 2312:3336], w[:, 3336:4360]
    k0, k1 = ka[:, :HEAD_DIM], ka[:, HEAD_DIM:]
    ka_dup = jnp.concatenate([k0, k0, k1, k1], axis=1)
    wf_pad = jnp.pad(wf, ((0, 0), (0, LANES - B_HEADS)))
    wp = jnp.concatenate([qa, ka_dup, qb, kb, g_a, g_b, wf_pad], axis=1).astype(BF16)
    wv = jnp.concatenate([vb, va], axis=1).astype(BF16)
    return wp, wv


def _attention_tables():
    slopes = 2.0 ** -(np.arange(1, A_HEADS + 1, dtype=np.float64))
    k = np.arange(BLOCK)[:, None]
    r = np.arange(BLOCK)[None, :]
    band = np.empty((2, A_KV_HEADS, 2 * BLOCK, A_GROUP * BLOCK), np.float32)
    meta = np.empty((A_KV_HEADS, N_META, A_GROUP * BLOCK), np.float32)
    slope_rows = np.empty((A_KV_HEADS, 1, A_GROUP * BLOCK), np.float32)
    mk = np.arange(N_META)[:, None]
    for h in range(A_HEADS):
        g, j = divmod(h, A_GROUP)
        cols = slice(j * BLOCK, (j + 1) * BLOCK)
        cur = np.where(k <= r, -slopes[h] * (r - k), NEG)
        prev = np.where(k > r, -slopes[h] * (r - k + BLOCK), NEG)
        band[0, g, :, cols] = np.concatenate([cur, np.full_like(cur, NEG)], axis=0)
        band[1, g, :, cols] = np.concatenate([prev, cur], axis=0)
        meta[g, :, cols] = -slopes[h] * (r + N_META - mk)
        slope_rows[g, :, cols] = slopes[h] * BLOCK
    band = np.where(band > NEG, band * LOG2E, NEG)
    tables = [t.astype(np.float32) for t in (band, meta * LOG2E, slope_rows * LOG2E)]
    return tuple(jnp.asarray(t) for t in tables)


def _bias_lane(h, p):
    return (HEAD_DIM if h % 2 == 0 else 0) + BIAS_PIECES * h + p


def _bias_placement():
    place = np.zeros((BIAS_PIECES, LANES, LANES), np.float32)
    for h in range(B_HEADS):
        for p in range(BIAS_PIECES):
            place[p, h, _bias_lane(h, p)] = 1.0
    return jnp.asarray(place, BF16)


def kernel(x, meta_tokens, ffn1_norm, ffn1_w_in, ffn1_w_out, mix_norm, w_in, b_forget, attn_sinks,
           w_branch_a, w_branch_b, w_out, ffn2_norm, ffn2_w_in, ffn2_w_out, final_norm):
    batch, seq, d = x.shape
    assert d == D_MODEL and seq % TOKEN_TILE == 0 and seq % Q_TILE == 0
    assert ffn1_norm.shape[0] == 1, "single layer"

    n1 = ffn1_norm[0][None].astype(F32)
    nm = mix_norm[0][None].astype(F32)
    n2 = ffn2_norm[0][None].astype(F32)
    nf = final_norm[None].astype(F32)
    w1i, w1o = ffn1_w_in[0].astype(BF16), ffn1_w_out[0].astype(BF16)
    w2i, w2o = ffn2_w_in[0].astype(BF16), ffn2_w_out[0].astype(BF16)
    wp, wv = _prep_proj_weight(w_in[0])
    bfg = jnp.pad(b_forget[0].astype(F32), (0, LANES - B_HEADS))[None]
    wa, wb, wo = w_branch_a[0].astype(BF16), w_branch_b[0].astype(BF16), w_out[0].astype(BF16)
    place = _bias_placement()

    x2d = x.reshape(batch * seq, D_MODEL)
    h1, qa, ka, qb, kb, ga, gb, cb, vt = _ffn_proj(
        x2d, n1, w1i, w1o, nm, wp, wv.T, bfg, place, TOKEN_TILE, seq // TOKEN_TILE, False)
    _, _, kam, _, kbm, _, _, cbm, vm = _ffn_proj(
        meta_tokens.astype(F32), n1, w1i, w1o, nm, wp, wv, bfg, place, N_META, 1, True)

    oa, ob = _attention(attn_sinks[0].astype(F32), qa, ka, qb, kb, cb, vt, kam, kbm, cbm, vm.T,
                        _attention_tables(), batch, seq)
    out = _mix_ffn(h1, oa, ob, ga, gb, wa, wb, wo, n2, w2i, w2o, nf, TOKEN_TILE)
    return out.reshape(batch, seq, D_MODEL)
```

```python
import jax
import jax.numpy as jnp
import numpy as np
from jax import lax
from jax.experimental import pallas as pl
from jax.experimental.pallas import tpu as pltpu

F32 = jnp.float32
BF16 = jnp.bfloat16

D_MODEL = 1024
N_META = 16
BLOCK = 128
HEAD_DIM = 64
A_HEADS = 8
A_KV_HEADS = 2
A_GROUP = A_HEADS // A_KV_HEADS
B_HEADS = 8
A_WIDTH = A_HEADS * HEAD_DIM
B_WIDTH = B_HEADS * HEAD_DIM
V_ROWS = B_WIDTH + A_KV_HEADS * HEAD_DIM
D_FF = 2816
EPS = 1e-6
NEG = -1e30
LOG2E = 1.4426950408889634
Q_SCALE = HEAD_DIM ** -0.5 * LOG2E

LANES = 128
TOKEN_TILE = 512
Q_TILE = 256
KV_TILE = 256
TILES_PER_TRIP = 2
FF_CHUNKS = ((0, 1024), (1024, 1024), (2048, 768))
VMEM_LIMIT_BYTES = 56 * 1024 * 1024
BIAS_PIECES = 3
ONES_ROWS = 16

_QA, _KA, _QB, _KB, _GA, _GB, _FG = 0, 512, 768, 1280, 1792, 2816, 3840
PROJ_COLS = 3968


def _dot(a, b):
    return jnp.dot(a, b, preferred_element_type=F32)


def _dot_nt(a, b):
    return lax.dot_general(a, b, (((1,), (1,)), ((), ())), preferred_element_type=F32)


def _rms(x, g):
    ms = jnp.mean(x * x, axis=-1, keepdims=True)
    return x * lax.rsqrt(ms + EPS) * g


def _swiglu(xn, w_in_ref, w_out_ref):
    acc = None
    for lo, width in FF_CHUNKS:
        g = _dot(xn, w_in_ref[:, lo:lo + width])
        u = _dot(xn, w_in_ref[:, D_FF + lo:D_FF + lo + width])
        a = (g * jax.nn.sigmoid(g) * u).astype(BF16)
        part = _dot(a, w_out_ref[lo:lo + width, :])
        acc = part if acc is None else acc + part
    return acc


def _split3(x):
    hi = x.astype(BF16)
    r = x - hi.astype(F32)
    mid = r.astype(BF16)
    lo = (r - mid.astype(F32)).astype(BF16)
    return hi, mid, lo


def _ffn_proj_kernel(tiles_per_seq, is_meta,
                     x_ref, n1_ref, w1i_ref, w1o_ref, nm_ref, wp_ref, wv_ref, bf_ref, tri_ref, place_ref,
                     h1_ref, qa_ref, ka_ref, qb_ref, kb_ref, ga_ref, gb_ref, v_ref, carry_ref):
    @pl.when(pl.program_id(0) % tiles_per_seq == 0)
    def _():
        carry_ref[...] = jnp.zeros_like(carry_ref)

    x = x_ref[...]
    tile = x.shape[0]
    xn = _rms(x, n1_ref[...]).astype(BF16)
    h1 = x + 0.5 * _swiglu(xn, w1i_ref, w1o_ref)
    h1_ref[...] = h1
    u = _rms(h1, nm_ref[...]).astype(BF16)

    def proj(lo, width):
        return _dot(u, wp_ref[:, lo:lo + width])

    qa_ref[...] = (proj(_QA, 512) * Q_SCALE).astype(BF16)
    ka_ref[...] = proj(_KA, 256).astype(BF16)
    qb_ref[...] = (proj(_QB, 512) * Q_SCALE).astype(BF16)
    kb = proj(_KB, 512).astype(BF16)

    z = proj(_FG, LANES) + bf_ref[...]
    lane = lax.broadcasted_iota(jnp.int32, (1, LANES), 1)
    lf = jnp.where(lane < B_HEADS, jnp.minimum(z, 0.0) - jnp.log1p(jnp.exp(-jnp.abs(z))), 0.0)
    tri = tri_ref[...]
    c = carry_ref[...] + sum(_dot(tri, piece) for piece in _split3(lf))
    carry_ref[...] = c[tile - 1:tile, :]
    bias = ((c[tile - 1:tile, :] - c) if is_meta else -c) * LOG2E
    cb = sum(_dot(piece, place_ref[p]) for p, piece in enumerate(_split3(bias))).astype(BF16)
    low = lane < HEAD_DIM
    for h in range(B_HEADS):
        k_pair = kb[:, (h // 2) * LANES:(h // 2 + 1) * LANES]
        kb_ref[:, h * LANES:(h + 1) * LANES] = jnp.where(low if h % 2 == 0 else ~low, k_pair, cb)

    ga_ref[...] = jax.nn.sigmoid(proj(_GA, 1024)).astype(BF16)
    gb_ref[...] = jax.nn.sigmoid(proj(_GB, 1024)).astype(BF16)

    if is_meta:
        v_ref[...] = _dot(u, wv_ref[...]).astype(BF16)
    else:
        vt = _dot_nt(wv_ref[...], u).astype(BF16)
        for c_blk in range(tile // LANES):
            v_ref[c_blk] = vt[:, c_blk * LANES:(c_blk + 1) * LANES]


def _const_spec(shape):
    zeros = (0,) * len(shape)
    return pl.BlockSpec(shape, lambda *_: zeros, pipeline_mode=pl.Buffered(1))


def _ffn_proj(x2d, n1, w1i, w1o, nm, wp, wv, bfg, place, tile, tiles_per_seq, is_meta):
    rows = x2d.shape[0]
    grid = (rows // tile,)
    tri = jnp.asarray(np.tril(np.ones((tile, tile), np.float32)), BF16)

    def row_spec(cols):
        return pl.BlockSpec((tile, cols), lambda i: (i, 0))

    out_cols = (D_MODEL, 512, 256, 512, B_HEADS * LANES, 1024, 1024)
    out_dtypes = (F32,) + (BF16,) * 6
    out_shape = [jax.ShapeDtypeStruct((rows, c), dt) for c, dt in zip(out_cols, out_dtypes)]
    out_specs = [row_spec(c) for c in out_cols]
    if is_meta:
        out_shape.append(jax.ShapeDtypeStruct((rows, V_ROWS), BF16))
        out_specs.append(row_spec(V_ROWS))
    else:
        out_shape.append(jax.ShapeDtypeStruct((rows // LANES, V_ROWS, LANES), BF16))
        out_specs.append(pl.BlockSpec((tile // LANES, V_ROWS, LANES), lambda i: (i, 0, 0)))
    consts = (n1, w1i, w1o, nm, wp, wv, bfg, tri, place)
    in_specs = [row_spec(D_MODEL)] + [_const_spec(a.shape) for a in consts]
    return pl.pallas_call(
        lambda *refs: _ffn_proj_kernel(tiles_per_seq, is_meta, *refs),
        grid=grid,
        in_specs=in_specs,
        out_specs=out_specs,
        out_shape=out_shape,
        scratch_shapes=[pltpu.VMEM((1, LANES), F32)],
        compiler_params=pltpu.CompilerParams(
            dimension_semantics=("arbitrary",), vmem_limit_bytes=VMEM_LIMIT_BYTES),
        name="ffn1_proj_meta" if is_meta else "ffn1_proj",
    )(x2d, *consts)


def _split_heads(q, low):
    zero = jnp.zeros_like(q)
    return jnp.concatenate([jnp.where(low, q, zero), jnp.where(low, zero, q)], axis=0)


def _attn_kernel(sinks_ref, qa_ref, ka_ref, qb_ref, kb_ref, vt_ref,
                 kam_ref, kbm_ref, vtm_ref, band_ref, mtab_ref, slope_ref,
                 oa_ref, ob_ref, m_ref, acc_ref):
    i = pl.program_id(1)
    low = lax.broadcasted_iota(jnp.int32, (1, LANES), 1) < HEAD_DIM
    sub = Q_TILE // BLOCK
    ones_meta = jnp.ones((ONES_ROWS, N_META), BF16)

    def v_tile(blk, n_blk, row0):
        vals = jnp.concatenate([vt_ref[blk + b, row0:row0 + HEAD_DIM, :] for b in range(n_blk)], axis=1)
        return jnp.concatenate([vals, jnp.ones((ONES_ROWS, n_blk * BLOCK), BF16)], axis=0)

    def v_meta(row0):
        return jnp.concatenate([vtm_ref[row0:row0 + HEAD_DIM, :], ones_meta], axis=0)

    units = [(sb, g) for sb in range(sub) for g in range(A_KV_HEADS)]
    blk0s, s_bs, s_ms = {}, {}, {}
    for sb, g in units:
        n = i * sub + sb
        blk0s[sb] = jnp.maximum(n - 1, 0)
        start = pl.multiple_of(blk0s[sb] * BLOCK, BLOCK)
        q = qa_ref[sb * BLOCK:(sb + 1) * BLOCK, :]
        qq = jnp.concatenate(
            [_split_heads(q[:, (2 * g + p) * LANES:(2 * g + p + 1) * LANES], low) for p in range(2)], axis=0)
        kband = ka_ref[pl.ds(start, 2 * BLOCK), g * LANES:(g + 1) * LANES]
        kmeta = kam_ref[:, g * LANES:(g + 1) * LANES]
        s_bs[sb, g] = _dot_nt(kband, qq) + band_ref[jnp.minimum(n, 1), g]
        s_ms[sb, g] = _dot_nt(kmeta, qq) + mtab_ref[g] - n.astype(F32) * slope_ref[g]
    p_bs, p_ms, sink_ps = {}, {}, {}
    for sb, g in units:
        s_b, s_m = s_bs[sb, g], s_ms[sb, g]
        sink = jnp.concatenate(
            [jnp.full((1, BLOCK), sinks_ref[g * A_GROUP + j] * LOG2E, F32) for j in range(A_GROUP)], axis=1)
        m = jnp.maximum(jnp.max(s_b, axis=0, keepdims=True), jnp.max(s_m, axis=0, keepdims=True))
        m = jnp.maximum(m, sink)
        p_bs[sb, g] = jnp.exp2(s_b - m).astype(BF16)
        p_ms[sb, g] = jnp.exp2(s_m - m).astype(BF16)
        sink_ps[sb, g] = jnp.exp2(sink - m)
    for sb, g in units:
        row0 = B_WIDTH + g * HEAD_DIM
        o_aug = _dot(v_tile(blk0s[sb], 2, row0), p_bs[sb, g]) + _dot(v_meta(row0), p_ms[sb, g])
        o_t = o_aug[:HEAD_DIM] / (o_aug[HEAD_DIM:HEAD_DIM + 1] + sink_ps[sb, g])
        for p in range(2):
            pair_t = jnp.concatenate([o_t[:, (2 * p) * BLOCK:(2 * p + 1) * BLOCK],
                                      o_t[:, (2 * p + 1) * BLOCK:(2 * p + 2) * BLOCK]], axis=0)
            oa_ref[sb * BLOCK:(sb + 1) * BLOCK, (2 * g + p) * LANES:(2 * g + p + 1) * LANES] = pair_t.T.astype(BF16)

    lane_row = lax.broadcasted_iota(jnp.int32, (1, LANES), 1)
    heads = range(B_HEADS)

    def own_half(h):
        return low if h % 2 == 0 else ~low

    def head_cols(h):
        return slice(h * LANES, (h + 1) * LANES)

    q_aug = []
    for h in heads:
        first = _bias_lane(h, 0)
        bias_ones = jnp.where((lane_row >= first) & (lane_row < first + BIAS_PIECES), 1.0, 0.0).astype(BF16)
        q_pair = qb_ref[:, (h // 2) * LANES:(h // 2 + 1) * LANES]
        q_aug.append(jnp.where(own_half(h), q_pair, bias_ones))

    row = lax.broadcasted_iota(jnp.int32, (KV_TILE, Q_TILE), 0)
    col = lax.broadcasted_iota(jnp.int32, (KV_TILE, Q_TILE), 1)
    causal = jnp.where(row <= col, 0.0, NEG)

    scores = [_dot_nt(kbm_ref[:, head_cols(h)], q_aug[h]) for h in heads]
    probs = []
    for h in heads:
        m0 = jnp.max(scores[h], axis=0, keepdims=True)
        m_ref[h] = m0
        probs.append(jnp.exp2(scores[h] - m0).astype(BF16))
    for h in heads:
        acc_ref[h] = _dot(v_meta(h * HEAD_DIM), probs[h])

    def update(blk, n_blk, mask):
        k0 = pl.multiple_of(blk * BLOCK, KV_TILE)
        rows = pl.ds(k0, n_blk * BLOCK)
        scores = [_dot_nt(kb_ref[rows, head_cols(h)], q_aug[h]) for h in heads]
        probs, alphas = [], []
        for h in heads:
            s = scores[h]
            if mask is not None:
                s = s + mask
            m_old = m_ref[h]
            m_new = jnp.maximum(m_old, jnp.max(s, axis=0, keepdims=True))
            m_ref[h] = m_new
            alphas.append(jnp.exp2(m_old - m_new))
            probs.append(jnp.exp2(s - m_new).astype(BF16))
        for h in heads:
            acc_ref[h] = alphas[h] * acc_ref[h] + _dot(v_tile(blk, n_blk, h * HEAD_DIM), probs[h])

    per_tile = KV_TILE // BLOCK

    def body(j, carry):
        update(TILES_PER_TRIP * per_tile * j, TILES_PER_TRIP * per_tile, None)
        return carry

    lax.fori_loop(0, lax.div(i, TILES_PER_TRIP), body, 0)

    def finalize():
        for pr in range(B_HEADS // 2):
            acc0, acc1 = acc_ref[2 * pr], acc_ref[2 * pr + 1]
            o_t = jnp.concatenate([acc0[:HEAD_DIM] / acc0[HEAD_DIM:HEAD_DIM + 1],
                                   acc1[:HEAD_DIM] / acc1[HEAD_DIM:HEAD_DIM + 1]], axis=0)
            ob_ref[:, pr * LANES:(pr + 1) * LANES] = o_t.T.astype(BF16)

    for left in range(TILES_PER_TRIP):
        @pl.when(lax.rem(i, TILES_PER_TRIP) == left)
        def _():
            mask = jnp.concatenate([jnp.zeros((left * KV_TILE, Q_TILE), F32), causal], axis=0) if left else causal
            update(per_tile * (i - left), (left + 1) * per_tile, mask)
            finalize()


def _attention(sinks, qa, ka, qb, kb, vt, kam, kbm, vtm, tables, batch, seq):
    nq = seq // Q_TILE
    grid = (batch, nq)

    def q_spec(cols):
        return pl.BlockSpec((Q_TILE, cols), lambda b, i: (b * nq + i, 0))

    def kv_spec(cols):
        return pl.BlockSpec((seq, cols), lambda b, i: (b, 0))

    in_specs = [
        pl.BlockSpec(memory_space=pltpu.SMEM),
        q_spec(A_WIDTH), kv_spec(256), q_spec(B_WIDTH), kv_spec(B_HEADS * LANES),
        pl.BlockSpec((seq // LANES, V_ROWS, LANES), lambda b, i: (b, 0, 0)),
    ]
    in_specs += [_const_spec(a.shape) for a in (kam, kbm, vtm) + tuple(tables)]
    out_shape = [jax.ShapeDtypeStruct((batch * seq, A_WIDTH), BF16),
                 jax.ShapeDtypeStruct((batch * seq, B_WIDTH), BF16)]
    out_specs = [q_spec(A_WIDTH), q_spec(B_WIDTH)]
    scratch = [
        pltpu.VMEM((B_HEADS, 1, Q_TILE), F32),
        pltpu.VMEM((B_HEADS, HEAD_DIM + ONES_ROWS, Q_TILE), F32),
    ]
    return pl.pallas_call(
        _attn_kernel,
        grid=grid,
        in_specs=in_specs,
        out_specs=out_specs,
        out_shape=out_shape,
        scratch_shapes=scratch,
        compiler_params=pltpu.CompilerParams(
            dimension_semantics=("arbitrary", "arbitrary"), vmem_limit_bytes=VMEM_LIMIT_BYTES),
        name="attention",
    )(sinks, qa, ka, qb, kb, vt, kam, kbm, vtm, *tables)


def _mix_ffn_kernel(h1_ref, oa_ref, ob_ref, ga_ref, gb_ref, wa_ref, wb_ref, wo_ref,
                    n2_ref, w2i_ref, w2o_ref, nf_ref, out_ref):
    ya = _dot(oa_ref[...], wa_ref[...])
    yb = _dot(ob_ref[...], wb_ref[...])
    mixed = (ga_ref[...].astype(F32) * ya + gb_ref[...].astype(F32) * yb).astype(BF16)
    h2 = h1_ref[...] + _dot(mixed, wo_ref[...])
    xn = _rms(h2, n2_ref[...]).astype(BF16)
    h3 = h2 + 0.5 * _swiglu(xn, w2i_ref, w2o_ref)
    out_ref[...] = _rms(h3, nf_ref[...])


def _mix_ffn(h1, oa, ob, ga, gb, wa, wb, wo, n2, w2i, w2o, nf, tile):
    rows = h1.shape[0]

    def row_spec(cols):
        return pl.BlockSpec((tile, cols), lambda i: (i, 0))

    in_specs = [row_spec(D_MODEL), row_spec(A_WIDTH), row_spec(B_WIDTH), row_spec(D_MODEL), row_spec(D_MODEL)]
    in_specs += [_const_spec(w.shape) for w in (wa, wb, wo, n2, w2i, w2o, nf)]
    return pl.pallas_call(
        _mix_ffn_kernel,
        grid=(rows // tile,),
        in_specs=in_specs,
        out_specs=row_spec(D_MODEL),
        out_shape=jax.ShapeDtypeStruct((rows, D_MODEL), F32),
        compiler_params=pltpu.CompilerParams(
            dimension_semantics=("arbitrary",), vmem_limit_bytes=VMEM_LIMIT_BYTES),
        name="mix_ffn2",
    )(h1, oa, ob, ga, gb, wa, wb, wo, n2, w2i, w2o, nf)


def _prep_proj_weight(w):
    qa = w[:, 0:512]
    ka = w[:, 512:640]
    va = w[:, 640:768]
    qb, kb, vb = w[:, 768:1280], w[:, 1280:1792], w[:, 1792:2304]
    wf = w[:, 2304:2312]
    g_a, g_b = w[:, 2312:3336], w[:, 3336:4360]
    k0, k1 = ka[:, :HEAD_DIM], ka[:, HEAD_DIM:]
    ka_dup = jnp.concatenate([k0, k0, k1, k1], axis=1)
    wf_pad = jnp.pad(wf, ((0, 0), (0, LANES - B_HEADS)))
    wp = jnp.concatenate([qa, ka_dup, qb, kb, g_a, g_b, wf_pad], axis=1).astype(BF16)
    wv = jnp.concatenate([vb, va], axis=1).astype(BF16)
    return wp, wv


def _attention_tables():
    slopes = 2.0 ** -(np.arange(1, A_HEADS + 1, dtype=np.float64))
    k = np.arange(BLOCK)[:, None]
    r = np.arange(BLOCK)[None, :]
    band = np.empty((2, A_KV_HEADS, 2 * BLOCK, A_GROUP * BLOCK), np.float32)
    meta = np.empty((A_KV_HEADS, N_META, A_GROUP * BLOCK), np.float32)
    slope_rows = np.empty((A_KV_HEADS, 1, A_GROUP * BLOCK), np.float32)
    mk = np.arange(N_META)[:, None]
    for h in range(A_HEADS):
        g, j = divmod(h, A_GROUP)
        cols = slice(j * BLOCK, (j + 1) * BLOCK)
        cur = np.where(k <= r, -slopes[h] * (r - k), NEG)
        prev = np.where(k > r, -slopes[h] * (r - k + BLOCK), NEG)
        band[0, g, :, cols] = np.concatenate([cur, np.full_like(cur, NEG)], axis=0)
        band[1, g, :, cols] = np.concatenate([prev, cur], axis=0)
        meta[g, :, cols] = -slopes[h] * (r + N_META - mk)
        slope_rows[g, :, cols] = slopes[h] * BLOCK
    band = np.where(band > NEG, band * LOG2E, NEG)
    tables = [t.astype(np.float32) for t in (band, meta * LOG2E, slope_rows * LOG2E)]
    return tuple(jnp.asarray(t) for t in tables)


def _bias_lane(h, p):
    return (HEAD_DIM if h % 2 == 0 else 0) + BIAS_PIECES * h + p


def _bias_placement():
    place = np.zeros((BIAS_PIECES, LANES, LANES), np.float32)
    for h in range(B_HEADS):
        for p in range(BIAS_PIECES):
            place[p, h, _bias_lane(h, p)] = 1.0
    return jnp.asarray(place, BF16)


def kernel(x, meta_tokens, ffn1_norm, ffn1_w_in, ffn1_w_out, mix_norm, w_in, b_forget, attn_sinks,
           w_branch_a, w_branch_b, w_out, ffn2_norm, ffn2_w_in, ffn2_w_out, final_norm):
    batch, seq, d = x.shape
    assert d == D_MODEL and seq % TOKEN_TILE == 0 and seq % Q_TILE == 0
    assert ffn1_norm.shape[0] == 1, "single layer"

    n1 = ffn1_norm[0][None].astype(F32)
    nm = mix_norm[0][None].astype(F32)
    n2 = ffn2_norm[0][None].astype(F32)
    nf = final_norm[None].astype(F32)
    w1i, w1o = ffn1_w_in[0].astype(BF16), ffn1_w_out[0].astype(BF16)
    w2i, w2o = ffn2_w_in[0].astype(BF16), ffn2_w_out[0].astype(BF16)
    wp, wv = _prep_proj_weight(w_in[0])
    bfg = jnp.pad(b_forget[0].astype(F32), (0, LANES - B_HEADS))[None]
    wa, wb, wo = w_branch_a[0].astype(BF16), w_branch_b[0].astype(BF16), w_out[0].astype(BF16)
    place = _bias_placement()

    x2d = x.reshape(batch * seq, D_MODEL)
    h1, qa, ka, qb, kb, ga, gb, vt = _ffn_proj(
        x2d, n1, w1i, w1o, nm, wp, wv.T, bfg, place, TOKEN_TILE, seq // TOKEN_TILE, False)
    _, _, kam, _, kbm, _, _, vm = _ffn_proj(
        meta_tokens.astype(F32), n1, w1i, w1o, nm, wp, wv, bfg, place, N_META, 1, True)

    oa, ob = _attention(attn_sinks[0].astype(F32), qa, ka, qb, kb, vt, kam, kbm, vm.T,
                        _attention_tables(), batch, seq)
    out = _mix_ffn(h1, oa, ob, ga, gb, wa, wb, wo, n2, w2i, w2o, nf, TOKEN_TILE)
    return out.reshape(batch, seq, D_MODEL)
```

```python
import jax
import jax.numpy as jnp
import numpy as np
from jax import lax
from jax.experimental import pallas as pl
from jax.experimental.pallas import tpu as pltpu

F32 = jnp.float32
BF16 = jnp.bfloat16

D_MODEL = 1024
N_META = 16
BLOCK = 128
HEAD_DIM = 64
A_HEADS = 8
A_KV_HEADS = 2
A_GROUP = A_HEADS // A_KV_HEADS
B_HEADS = 8
A_WIDTH = A_HEADS * HEAD_DIM
B_WIDTH = B_HEADS * HEAD_DIM
V_ROWS = B_WIDTH + A_KV_HEADS * HEAD_DIM
D_FF = 2816
EPS = 1e-6
NEG = -1e30
LOG2E = 1.4426950408889634
Q_SCALE = HEAD_DIM ** -0.5 * LOG2E

LANES = 128
TOKEN_TILE = 512
Q_TILE = 256
KV_TILE = 256
TILES_PER_TRIP = 2
FF_CHUNKS = ((0, 1024), (1024, 1024), (2048, 768))
VMEM_LIMIT_BYTES = 56 * 1024 * 1024
BIAS_PIECES = 3
ONES_ROWS = 16

_QA, _KA, _QB, _KB, _GA, _GB, _FG = 0, 512, 768, 1280, 1792, 2816, 3840
PROJ_COLS = 3968


def _dot(a, b):
    return jnp.dot(a, b, preferred_element_type=F32)


def _dot_nt(a, b):
    return lax.dot_general(a, b, (((1,), (1,)), ((), ())), preferred_element_type=F32)


def _rms(x, g):
    ms = jnp.mean(x * x, axis=-1, keepdims=True)
    return x * lax.rsqrt(ms + EPS) * g


def _swiglu(xn, w_in_ref, w_out_ref):
    acc = None
    for lo, width in FF_CHUNKS:
        g = _dot(xn, w_in_ref[:, lo:lo + width])
        u = _dot(xn, w_in_ref[:, D_FF + lo:D_FF + lo + width])
        a = (g * jax.nn.sigmoid(g) * u).astype(BF16)
        part = _dot(a, w_out_ref[lo:lo + width, :])
        acc = part if acc is None else acc + part
    return acc


def _split3(x):
    hi = x.astype(BF16)
    r = x - hi.astype(F32)
    mid = r.astype(BF16)
    lo = (r - mid.astype(F32)).astype(BF16)
    return hi, mid, lo


def _ffn_proj_kernel(tiles_per_seq, is_meta,
                     x_ref, n1_ref, w1i_ref, w1o_ref, nm_ref, wp_ref, wv_ref, bf_ref, tri_ref, place_ref,
                     h1_ref, qa_ref, ka_ref, qb_ref, kb_ref, ga_ref, gb_ref, v_ref, carry_ref):
    @pl.when(pl.program_id(0) % tiles_per_seq == 0)
    def _():
        carry_ref[...] = jnp.zeros_like(carry_ref)

    x = x_ref[...]
    tile = x.shape[0]
    xn = _rms(x, n1_ref[...]).astype(BF16)
    h1 = x + 0.5 * _swiglu(xn, w1i_ref, w1o_ref)
    h1_ref[...] = h1
    u = _rms(h1, nm_ref[...]).astype(BF16)

    def proj(lo, width):
        return _dot(u, wp_ref[:, lo:lo + width])

    qa_ref[...] = (proj(_QA, 512) * Q_SCALE).astype(BF16)
    ka_ref[...] = proj(_KA, 256).astype(BF16)
    qb_ref[...] = (proj(_QB, 512) * Q_SCALE).astype(BF16)
    kb = proj(_KB, 512).astype(BF16)

    z = proj(_FG, LANES) + bf_ref[...]
    lane = lax.broadcasted_iota(jnp.int32, (1, LANES), 1)
    lf = jnp.where(lane < B_HEADS, jnp.minimum(z, 0.0) - jnp.log1p(jnp.exp(-jnp.abs(z))), 0.0)
    tri = tri_ref[...]
    c = carry_ref[...] + sum(_dot(tri, piece) for piece in _split3(lf))
    carry_ref[...] = c[tile - 1:tile, :]
    bias = ((c[tile - 1:tile, :] - c) if is_meta else -c) * LOG2E
    cb = sum(_dot(piece, place_ref[p]) for p, piece in enumerate(_split3(bias))).astype(BF16)
    low = lane < HEAD_DIM
    for h in range(B_HEADS):
        k_pair = kb[:, (h // 2) * LANES:(h // 2 + 1) * LANES]
        kb_ref[:, h * LANES:(h + 1) * LANES] = jnp.where(low if h % 2 == 0 else ~low, k_pair, cb)

    ga_ref[...] = jax.nn.sigmoid(proj(_GA, 1024)).astype(BF16)
    gb_ref[...] = jax.nn.sigmoid(proj(_GB, 1024)).astype(BF16)

    if is_meta:
        v_ref[...] = _dot(u, wv_ref[...]).astype(BF16)
    else:
        vt = _dot_nt(wv_ref[...], u).astype(BF16)
        for c_blk in range(tile // LANES):
            v_ref[c_blk] = vt[:, c_blk * LANES:(c_blk + 1) * LANES]


def _const_spec(shape):
    zeros = (0,) * len(shape)
    return pl.BlockSpec(shape, lambda *_: zeros, pipeline_mode=pl.Buffered(1))


def _ffn_proj(x2d, n1, w1i, w1o, nm, wp, wv, bfg, place, tile, tiles_per_seq, is_meta):
    rows = x2d.shape[0]
    grid = (rows // tile,)
    tri = jnp.asarray(np.tril(np.ones((tile, tile), np.float32)), BF16)

    def row_spec(cols):
        return pl.BlockSpec((tile, cols), lambda i: (i, 0))

    out_cols = (D_MODEL, 512, 256, 512, B_HEADS * LANES, 1024, 1024)
    out_dtypes = (F32,) + (BF16,) * 6
    out_shape = [jax.ShapeDtypeStruct((rows, c), dt) for c, dt in zip(out_cols, out_dtypes)]
    out_specs = [row_spec(c) for c in out_cols]
    if is_meta:
        out_shape.append(jax.ShapeDtypeStruct((rows, V_ROWS), BF16))
        out_specs.append(row_spec(V_ROWS))
    else:
        out_shape.append(jax.ShapeDtypeStruct((rows // LANES, V_ROWS, LANES), BF16))
        out_specs.append(pl.BlockSpec((tile // LANES, V_ROWS, LANES), lambda i: (i, 0, 0)))
    consts = (n1, w1i, w1o, nm, wp, wv, bfg, tri, place)
    in_specs = [row_spec(D_MODEL)] + [_const_spec(a.shape) for a in consts]
    return pl.pallas_call(
        lambda *refs: _ffn_proj_kernel(tiles_per_seq, is_meta, *refs),
        grid=grid,
        in_specs=in_specs,
        out_specs=out_specs,
        out_shape=out_shape,
        scratch_shapes=[pltpu.VMEM((1, LANES), F32)],
        compiler_params=pltpu.CompilerParams(
            dimension_semantics=("arbitrary",), vmem_limit_bytes=VMEM_LIMIT_BYTES),
        name="ffn1_proj_meta" if is_meta else "ffn1_proj",
    )(x2d, *consts)


def _split_heads(q, low):
    zero = jnp.zeros_like(q)
    return jnp.concatenate([jnp.where(low, q, zero), jnp.where(low, zero, q)], axis=0)


def _attn_kernel(sinks_ref, qa_ref, ka_ref, qb_ref, kb_ref, vt_ref,
                 kam_ref, kbm_ref, vtm_ref, band_ref, mtab_ref, slope_ref,
                 oa_ref, ob_ref, m_ref, acc_ref):
    i = pl.program_id(1)
    low = lax.broadcasted_iota(jnp.int32, (1, LANES), 1) < HEAD_DIM
    sub = Q_TILE // BLOCK
    ones_meta = jnp.ones((ONES_ROWS, N_META), BF16)

    def v_tile(blk, n_blk, row0):
        vals = jnp.concatenate([vt_ref[blk + b, row0:row0 + HEAD_DIM, :] for b in range(n_blk)], axis=1)
        return jnp.concatenate([vals, jnp.ones((ONES_ROWS, n_blk * BLOCK), BF16)], axis=0)

    def v_meta(row0):
        return jnp.concatenate([vtm_ref[row0:row0 + HEAD_DIM, :], ones_meta], axis=0)

    own_key = (lax.broadcasted_iota(jnp.int32, (BLOCK, A_GROUP * BLOCK), 0)
               <= (lax.broadcasted_iota(jnp.int32, (BLOCK, A_GROUP * BLOCK), 1) & (BLOCK - 1)))
    units = [(sb, g) for sb in range(sub) for g in range(A_KV_HEADS)]
    blks, s_bs, s_ms = {}, {}, {}
    for sb, g in units:
        n = i * sub + sb
        blks[sb] = (jnp.maximum(n - 1, 0), n)
        q = qa_ref[sb * BLOCK:(sb + 1) * BLOCK, :]
        qq = jnp.concatenate(
            [_split_heads(q[:, (2 * g + p) * LANES:(2 * g + p + 1) * LANES], low) for p in range(2)], axis=0)
        kband = jnp.concatenate(
            [ka_ref[pl.ds(pl.multiple_of(b * BLOCK, BLOCK), BLOCK), g * LANES:(g + 1) * LANES] for b in blks[sb]], axis=0)
        kmeta = kam_ref[:, g * LANES:(g + 1) * LANES]
        s2 = _dot_nt(kband, qq)
        s_bs[sb, g] = jnp.where(own_key, s2[BLOCK:], s2[:BLOCK]) + band_ref[jnp.minimum(n, 1), g]
        s_ms[sb, g] = _dot_nt(kmeta, qq) + mtab_ref[g] - n.astype(F32) * slope_ref[g]
    p_bs, p_ms, sink_ps = {}, {}, {}
    for sb, g in units:
        s_b, s_m = s_bs[sb, g], s_ms[sb, g]
        sink = jnp.concatenate(
            [jnp.full((1, BLOCK), sinks_ref[g * A_GROUP + j] * LOG2E, F32) for j in range(A_GROUP)], axis=1)
        m = jnp.maximum(jnp.max(s_b, axis=0, keepdims=True), jnp.max(s_m, axis=0, keepdims=True))
        m = jnp.maximum(m, sink)
        p_bs[sb, g] = jnp.exp2(s_b - m).astype(BF16)
        p_ms[sb, g] = jnp.exp2(s_m - m).astype(BF16)
        sink_ps[sb, g] = jnp.exp2(sink - m)
    for sb, g in units:
        row0 = B_WIDTH + g * HEAD_DIM
        prev_blk, own_blk = blks[sb]
        p_b = p_bs[sb, g]
        zero = jnp.zeros_like(p_b)
        o_aug = (_dot(v_tile(own_blk, 1, row0), jnp.where(own_key, p_b, zero))
                 + _dot(v_tile(prev_blk, 1, row0), jnp.where(own_key, zero, p_b))
                 + _dot(v_meta(row0), p_ms[sb, g]))
        o_t = o_aug[:HEAD_DIM] / (o_aug[HEAD_DIM:HEAD_DIM + 1] + sink_ps[sb, g])
        for p in range(2):
            pair_t = jnp.concatenate([o_t[:, (2 * p) * BLOCK:(2 * p + 1) * BLOCK],
                                      o_t[:, (2 * p + 1) * BLOCK:(2 * p + 2) * BLOCK]], axis=0)
            oa_ref[sb * BLOCK:(sb + 1) * BLOCK, (2 * g + p) * LANES:(2 * g + p + 1) * LANES] = pair_t.T.astype(BF16)

    lane_row = lax.broadcasted_iota(jnp.int32, (1, LANES), 1)
    heads = range(B_HEADS)

    def own_half(h):
        return low if h % 2 == 0 else ~low

    def head_cols(h):
        return slice(h * LANES, (h + 1) * LANES)

    q_aug = []
    for h in heads:
        first = _bias_lane(h, 0)
        bias_ones = jnp.where((lane_row >= first) & (lane_row < first + BIAS_PIECES), 1.0, 0.0).astype(BF16)
        q_pair = qb_ref[:, (h // 2) * LANES:(h // 2 + 1) * LANES]
        q_aug.append(jnp.where(own_half(h), q_pair, bias_ones))

    row = lax.broadcasted_iota(jnp.int32, (KV_TILE, Q_TILE), 0)
    col = lax.broadcasted_iota(jnp.int32, (KV_TILE, Q_TILE), 1)
    causal = jnp.where(row <= col, 0.0, NEG)

    scores = [_dot_nt(kbm_ref[:, head_cols(h)], q_aug[h]) for h in heads]
    probs = []
    for h in heads:
        m0 = jnp.max(scores[h], axis=0, keepdims=True)
        m_ref[h] = m0
        probs.append(jnp.exp2(scores[h] - m0).astype(BF16))
    for h in heads:
        acc_ref[h] = _dot(v_meta(h * HEAD_DIM), probs[h])

    def update(blk, n_blk, mask):
        k0 = pl.multiple_of(blk * BLOCK, KV_TILE)
        rows = pl.ds(k0, n_blk * BLOCK)
        scores = [_dot_nt(kb_ref[rows, head_cols(h)], q_aug[h]) for h in heads]
        probs, alphas = [], []
        for h in heads:
            s = scores[h]
            if mask is not None:
                s = s + mask
            m_old = m_ref[h]
            m_new = jnp.maximum(m_old, jnp.max(s, axis=0, keepdims=True))
            m_ref[h] = m_new
            alphas.append(jnp.exp2(m_old - m_new))
            probs.append(jnp.exp2(s - m_new).astype(BF16))
        for h in heads:
            acc_ref[h] = alphas[h] * acc_ref[h] + _dot(v_tile(blk, n_blk, h * HEAD_DIM), probs[h])

    per_tile = KV_TILE // BLOCK

    def body(j, carry):
        update(TILES_PER_TRIP * per_tile * j, TILES_PER_TRIP * per_tile, None)
        return carry

    lax.fori_loop(0, lax.div(i, TILES_PER_TRIP), body, 0)

    def finalize():
        for pr in range(B_HEADS // 2):
            acc0, acc1 = acc_ref[2 * pr], acc_ref[2 * pr + 1]
            o_t = jnp.concatenate([acc0[:HEAD_DIM] / acc0[HEAD_DIM:HEAD_DIM + 1],
                                   acc1[:HEAD_DIM] / acc1[HEAD_DIM:HEAD_DIM + 1]], axis=0)
            ob_ref[:, pr * LANES:(pr + 1) * LANES] = o_t.T.astype(BF16)

    for left in range(TILES_PER_TRIP):
        @pl.when(lax.rem(i, TILES_PER_TRIP) == left)
        def _():
            mask = jnp.concatenate([jnp.zeros((left * KV_TILE, Q_TILE), F32), causal], axis=0) if left else causal
            update(per_tile * (i - left), (left + 1) * per_tile, mask)
            finalize()


def _attention(sinks, qa, ka, qb, kb, vt, kam, kbm, vtm, tables, batch, seq):
    nq = seq // Q_TILE
    grid = (batch, nq)

    def q_spec(cols):
        return pl.BlockSpec((Q_TILE, cols), lambda b, i: (b * nq + i, 0))

    def kv_spec(cols):
        return pl.BlockSpec((seq, cols), lambda b, i: (b, 0))

    in_specs = [
        pl.BlockSpec(memory_space=pltpu.SMEM),
        q_spec(A_WIDTH), kv_spec(256), q_spec(B_WIDTH), kv_spec(B_HEADS * LANES),
        pl.BlockSpec((seq // LANES, V_ROWS, LANES), lambda b, i: (b, 0, 0)),
    ]
    in_specs += [_const_spec(a.shape) for a in (kam, kbm, vtm) + tuple(tables)]
    out_shape = [jax.ShapeDtypeStruct((batch * seq, A_WIDTH), BF16),
                 jax.ShapeDtypeStruct((batch * seq, B_WIDTH), BF16)]
    out_specs = [q_spec(A_WIDTH), q_spec(B_WIDTH)]
    scratch = [
        pltpu.VMEM((B_HEADS, 1, Q_TILE), F32),
        pltpu.VMEM((B_HEADS, HEAD_DIM + ONES_ROWS, Q_TILE), F32),
    ]
    return pl.pallas_call(
        _attn_kernel,
        grid=grid,
        in_specs=in_specs,
        out_specs=out_specs,
        out_shape=out_shape,
        scratch_shapes=scratch,
        compiler_params=pltpu.CompilerParams(
            dimension_semantics=("arbitrary", "arbitrary"), vmem_limit_bytes=VMEM_LIMIT_BYTES),
        name="attention",
    )(sinks, qa, ka, qb, kb, vt, kam, kbm, vtm, *tables)


def _mix_ffn_kernel(h1_ref, oa_ref, ob_ref, ga_ref, gb_ref, wa_ref, wb_ref, wo_ref,
                    n2_ref, w2i_ref, w2o_ref, nf_ref, out_ref):
    ya = _dot(oa_ref[...], wa_ref[...])
    yb = _dot(ob_ref[...], wb_ref[...])
    mixed = (ga_ref[...].astype(F32) * ya + gb_ref[...].astype(F32) * yb).astype(BF16)
    h2 = h1_ref[...] + _dot(mixed, wo_ref[...])
    xn = _rms(h2, n2_ref[...]).astype(BF16)
    h3 = h2 + 0.5 * _swiglu(xn, w2i_ref, w2o_ref)
    out_ref[...] = _rms(h3, nf_ref[...])


def _mix_ffn(h1, oa, ob, ga, gb, wa, wb, wo, n2, w2i, w2o, nf, tile):
    rows = h1.shape[0]

    def row_spec(cols):
        return pl.BlockSpec((tile, cols), lambda i: (i, 0))

    in_specs = [row_spec(D_MODEL), row_spec(A_WIDTH), row_spec(B_WIDTH), row_spec(D_MODEL), row_spec(D_MODEL)]
    in_specs += [_const_spec(w.shape) for w in (wa, wb, wo, n2, w2i, w2o, nf)]
    return pl.pallas_call(
        _mix_ffn_kernel,
        grid=(rows // tile,),
        in_specs=in_specs,
        out_specs=row_spec(D_MODEL),
        out_shape=jax.ShapeDtypeStruct((rows, D_MODEL), F32),
        compiler_params=pltpu.CompilerParams(
            dimension_semantics=("arbitrary",), vmem_limit_bytes=VMEM_LIMIT_BYTES),
        name="mix_ffn2",
    )(h1, oa, ob, ga, gb, wa, wb, wo, n2, w2i, w2o, nf)


def _prep_proj_weight(w):
    qa = w[:, 0:512]
    ka = w[:, 512:640]
    va = w[:, 640:768]
    qb, kb, vb = w[:, 768:1280], w[:, 1280:1792], w[:, 1792:2304]
    wf = w[:, 2304:2312]
    g_a, g_b = w[:, 2312:3336], w[:, 3336:4360]
    k0, k1 = ka[:, :HEAD_DIM], ka[:, HEAD_DIM:]
    ka_dup = jnp.concatenate([k0, k0, k1, k1], axis=1)
    wf_pad = jnp.pad(wf, ((0, 0), (0, LANES - B_HEADS)))
    wp = jnp.concatenate([qa, ka_dup, qb, kb, g_a, g_b, wf_pad], axis=1).astype(BF16)
    wv = jnp.concatenate([vb, va], axis=1).astype(BF16)
    return wp, wv


def _attention_tables():
    slopes = 2.0 ** -(np.arange(1, A_HEADS + 1, dtype=np.float64))
    k = np.arange(BLOCK)[:, None]
    r = np.arange(BLOCK)[None, :]
    band = np.empty((2, A_KV_HEADS, BLOCK, A_GROUP * BLOCK), np.float32)
    meta = np.empty((A_KV_HEADS, N_META, A_GROUP * BLOCK), np.float32)
    slope_rows = np.empty((A_KV_HEADS, 1, A_GROUP * BLOCK), np.float32)
    mk = np.arange(N_META)[:, None]
    for h in range(A_HEADS):
        g, j = divmod(h, A_GROUP)
        cols = slice(j * BLOCK, (j + 1) * BLOCK)
        dist = np.where(k <= r, r - k, r - k + BLOCK)
        band[1, g, :, cols] = -slopes[h] * dist
        band[0, g, :, cols] = np.where(k <= r, -slopes[h] * dist, NEG)
        meta[g, :, cols] = -slopes[h] * (r + N_META - mk)
        slope_rows[g, :, cols] = slopes[h] * BLOCK
    band = np.where(band > NEG, band * LOG2E, NEG)
    tables = [t.astype(np.float32) for t in (band, meta * LOG2E, slope_rows * LOG2E)]
    return tuple(jnp.asarray(t) for t in tables)


def _bias_lane(h, p):
    return (HEAD_DIM if h % 2 == 0 else 0) + BIAS_PIECES * h + p


def _bias_placement():
    place = np.zeros((BIAS_PIECES, LANES, LANES), np.float32)
    for h in range(B_HEADS):
        for p in range(BIAS_PIECES):
            place[p, h, _bias_lane(h, p)] = 1.0
    return jnp.asarray(place, BF16)


def kernel(x, meta_tokens, ffn1_norm, ffn1_w_in, ffn1_w_out, mix_norm, w_in, b_forget, attn_sinks,
           w_branch_a, w_branch_b, w_out, ffn2_norm, ffn2_w_in, ffn2_w_out, final_norm):
    batch, seq, d = x.shape
    assert d == D_MODEL and seq % TOKEN_TILE == 0 and seq % Q_TILE == 0
    assert ffn1_norm.shape[0] == 1, "single layer"

    n1 = ffn1_norm[0][None].astype(F32)
    nm = mix_norm[0][None].astype(F32)
    n2 = ffn2_norm[0][None].astype(F32)
    nf = final_norm[None].astype(F32)
    w1i, w1o = ffn1_w_in[0].astype(BF16), ffn1_w_out[0].astype(BF16)
    w2i, w2o = ffn2_w_in[0].astype(BF16), ffn2_w_out[0].astype(BF16)
    wp, wv = _prep_proj_weight(w_in[0])
    bfg = jnp.pad(b_forget[0].astype(F32), (0, LANES - B_HEADS))[None]
    wa, wb, wo = w_branch_a[0].astype(BF16), w_branch_b[0].astype(BF16), w_out[0].astype(BF16)
    place = _bias_placement()

    x2d = x.reshape(batch * seq, D_MODEL)
    h1, qa, ka, qb, kb, ga, gb, vt = _ffn_proj(
        x2d, n1, w1i, w1o, nm, wp, wv.T, bfg, place, TOKEN_TILE, seq // TOKEN_TILE, False)
    _, _, kam, _, kbm, _, _, vm = _ffn_proj(
        meta_tokens.astype(F32), n1, w1i, w1o, nm, wp, wv, bfg, place, N_META, 1, True)

    oa, ob = _attention(attn_sinks[0].astype(F32), qa, ka, qb, kb, vt, kam, kbm, vm.T,
                        _attention_tables(), batch, seq)
    out = _mix_ffn(h1, oa, ob, ga, gb, wa, wb, wo, n2, w2i, w2o, nf, TOKEN_TILE)
    return out.reshape(batch, seq, D_MODEL)
```

```python
import jax
import jax.numpy as jnp
import numpy as np
from jax import lax
from jax.experimental import pallas as pl
from jax.experimental.pallas import tpu as pltpu

F32 = jnp.float32
BF16 = jnp.bfloat16

D_MODEL = 1024
N_META = 16
BLOCK = 128
HEAD_DIM = 64
A_HEADS = 8
A_KV_HEADS = 2
A_GROUP = A_HEADS // A_KV_HEADS
B_HEADS = 8
A_WIDTH = A_HEADS * HEAD_DIM
B_WIDTH = B_HEADS * HEAD_DIM
V_ROWS = B_WIDTH + A_KV_HEADS * HEAD_DIM
D_FF = 2816
EPS = 1e-6
NEG = -1e30
LOG2E = 1.4426950408889634
Q_SCALE = HEAD_DIM ** -0.5 * LOG2E

LANES = 128
TOKEN_TILE = 512
Q_TILE = 256
KV_TILE = 256
TILES_PER_TRIP = 2
SEQS_PER_STEP = 2
FF_CHUNKS = ((0, 1024), (1024, 1024), (2048, 768))
VMEM_LIMIT_BYTES = 56 * 1024 * 1024
BIAS_PIECES = 3
ONES_ROWS = 16

_QA, _KA, _QB, _KB, _GA, _GB, _FG = 0, 512, 768, 1280, 1792, 2816, 3840
PROJ_COLS = 3968


def _dot(a, b):
    return jnp.dot(a, b, preferred_element_type=F32)


def _dot_nt(a, b):
    return lax.dot_general(a, b, (((1,), (1,)), ((), ())), preferred_element_type=F32)


def _rms(x, g):
    ms = jnp.mean(x * x, axis=-1, keepdims=True)
    return x * lax.rsqrt(ms + EPS) * g


def _swiglu(xn, w_in_ref, w_out_ref):
    acc = None
    for lo, width in FF_CHUNKS:
        g = _dot(xn, w_in_ref[:, lo:lo + width])
        u = _dot(xn, w_in_ref[:, D_FF + lo:D_FF + lo + width])
        a = (g * jax.nn.sigmoid(g) * u).astype(BF16)
        part = _dot(a, w_out_ref[lo:lo + width, :])
        acc = part if acc is None else acc + part
    return acc


def _split3(x):
    hi = x.astype(BF16)
    r = x - hi.astype(F32)
    mid = r.astype(BF16)
    lo = (r - mid.astype(F32)).astype(BF16)
    return hi, mid, lo


def _ffn_proj_kernel(tiles_per_seq, is_meta,
                     x_ref, n1_ref, w1i_ref, w1o_ref, nm_ref, wp_ref, wv_ref, bf_ref, tri_ref, place_ref,
                     h1_ref, qa_ref, ka_ref, qb_ref, kb_ref, ga_ref, gb_ref, v_ref, carry_ref):
    @pl.when(pl.program_id(0) % tiles_per_seq == 0)
    def _():
        carry_ref[...] = jnp.zeros_like(carry_ref)

    x = x_ref[...]
    tile = x.shape[0]
    xn = _rms(x, n1_ref[...]).astype(BF16)
    h1 = x + 0.5 * _swiglu(xn, w1i_ref, w1o_ref)
    h1_ref[...] = h1
    u = _rms(h1, nm_ref[...]).astype(BF16)

    def proj(lo, width):
        return _dot(u, wp_ref[:, lo:lo + width])

    qa_ref[...] = (proj(_QA, 512) * Q_SCALE).astype(BF16)
    ka_ref[...] = proj(_KA, 256).astype(BF16)
    qb_ref[...] = (proj(_QB, 512) * Q_SCALE).astype(BF16)
    kb = proj(_KB, 512).astype(BF16)

    z = proj(_FG, LANES) + bf_ref[...]
    lane = lax.broadcasted_iota(jnp.int32, (1, LANES), 1)
    lf = jnp.where(lane < B_HEADS, jnp.minimum(z, 0.0) - jnp.log1p(jnp.exp(-jnp.abs(z))), 0.0)
    tri = tri_ref[...]
    c = carry_ref[...] + sum(_dot(tri, piece) for piece in _split3(lf))
    carry_ref[...] = c[tile - 1:tile, :]
    bias = ((c[tile - 1:tile, :] - c) if is_meta else -c) * LOG2E
    cb = sum(_dot(piece, place_ref[p]) for p, piece in enumerate(_split3(bias))).astype(BF16)
    low = lane < HEAD_DIM
    for h in range(B_HEADS):
        k_pair = kb[:, (h // 2) * LANES:(h // 2 + 1) * LANES]
        kb_ref[:, h * LANES:(h + 1) * LANES] = jnp.where(low if h % 2 == 0 else ~low, k_pair, cb)

    ga_ref[...] = jax.nn.sigmoid(proj(_GA, 1024)).astype(BF16)
    gb_ref[...] = jax.nn.sigmoid(proj(_GB, 1024)).astype(BF16)

    if is_meta:
        v_ref[...] = _dot(u, wv_ref[...]).astype(BF16)
    else:
        vt = _dot_nt(wv_ref[...], u).astype(BF16)
        for c_blk in range(tile // LANES):
            v_ref[c_blk] = vt[:, c_blk * LANES:(c_blk + 1) * LANES]


def _const_spec(shape):
    zeros = (0,) * len(shape)
    return pl.BlockSpec(shape, lambda *_: zeros, pipeline_mode=pl.Buffered(1))


def _ffn_proj(x2d, n1, w1i, w1o, nm, wp, wv, bfg, place, tile, tiles_per_seq, is_meta):
    rows = x2d.shape[0]
    grid = (rows // tile,)
    tri = jnp.asarray(np.tril(np.ones((tile, tile), np.float32)), BF16)

    def row_spec(cols):
        return pl.BlockSpec((tile, cols), lambda i: (i, 0))

    out_cols = (D_MODEL, 512, 256, 512, B_HEADS * LANES, 1024, 1024)
    out_dtypes = (F32,) + (BF16,) * 6
    out_shape = [jax.ShapeDtypeStruct((rows, c), dt) for c, dt in zip(out_cols, out_dtypes)]
    out_specs = [row_spec(c) for c in out_cols]
    if is_meta:
        out_shape.append(jax.ShapeDtypeStruct((rows, V_ROWS), BF16))
        out_specs.append(row_spec(V_ROWS))
    else:
        out_shape.append(jax.ShapeDtypeStruct((rows // LANES, V_ROWS, LANES), BF16))
        out_specs.append(pl.BlockSpec((tile // LANES, V_ROWS, LANES), lambda i: (i, 0, 0)))
    consts = (n1, w1i, w1o, nm, wp, wv, bfg, tri, place)
    in_specs = [row_spec(D_MODEL)] + [_const_spec(a.shape) for a in consts]
    return pl.pallas_call(
        lambda *refs: _ffn_proj_kernel(tiles_per_seq, is_meta, *refs),
        grid=grid,
        in_specs=in_specs,
        out_specs=out_specs,
        out_shape=out_shape,
        scratch_shapes=[pltpu.VMEM((1, LANES), F32)],
        compiler_params=pltpu.CompilerParams(
            dimension_semantics=("arbitrary",), vmem_limit_bytes=VMEM_LIMIT_BYTES),
        name="ffn1_proj_meta" if is_meta else "ffn1_proj",
    )(x2d, *consts)


def _split_heads(q, low):
    zero = jnp.zeros_like(q)
    return jnp.concatenate([jnp.where(low, q, zero), jnp.where(low, zero, q)], axis=0)


def _attn_kernel(sinks_ref, qa_ref, ka_ref, qb_ref, kb_ref, vt_ref,
                 kam_ref, kbm_ref, vtm_ref, band_ref, mtab_ref, slope_ref,
                 oa_ref, ob_ref, m_ref, acc_ref):
    i = pl.program_id(1)
    low = lax.broadcasted_iota(jnp.int32, (1, LANES), 1) < HEAD_DIM
    sub = Q_TILE // BLOCK
    seqs = range(SEQS_PER_STEP)
    ones_meta = jnp.ones((ONES_ROWS, N_META), BF16)

    def v_tile(sq, blk, n_blk, row0):
        vals = jnp.concatenate([vt_ref[sq, blk + b, row0:row0 + HEAD_DIM, :] for b in range(n_blk)], axis=1)
        return jnp.concatenate([vals, jnp.ones((ONES_ROWS, n_blk * BLOCK), BF16)], axis=0)

    def v_meta(row0):
        return jnp.concatenate([vtm_ref[row0:row0 + HEAD_DIM, :], ones_meta], axis=0)

    own_key = (lax.broadcasted_iota(jnp.int32, (BLOCK, A_GROUP * BLOCK), 0)
               <= (lax.broadcasted_iota(jnp.int32, (BLOCK, A_GROUP * BLOCK), 1) & (BLOCK - 1)))
    units = [(sq, sb, g) for sq in seqs for sb in range(sub) for g in range(A_KV_HEADS)]
    blks, s_bs, s_ms = {}, {}, {}
    for sq, sb, g in units:
        n = i * sub + sb
        blks[sb] = (jnp.maximum(n - 1, 0), n)
        q = qa_ref[sq, sb * BLOCK:(sb + 1) * BLOCK, :]
        qq = jnp.concatenate(
            [_split_heads(q[:, (2 * g + p) * LANES:(2 * g + p + 1) * LANES], low) for p in range(2)], axis=0)
        kband = jnp.concatenate(
            [ka_ref[sq, pl.ds(pl.multiple_of(b * BLOCK, BLOCK), BLOCK), g * LANES:(g + 1) * LANES]
             for b in blks[sb]], axis=0)
        kmeta = kam_ref[:, g * LANES:(g + 1) * LANES]
        s2 = _dot_nt(kband, qq)
        s_bs[sq, sb, g] = jnp.where(own_key, s2[BLOCK:], s2[:BLOCK]) + band_ref[jnp.minimum(n, 1), g]
        s_ms[sq, sb, g] = _dot_nt(kmeta, qq) + mtab_ref[g] - n.astype(F32) * slope_ref[g]
    p_bs, p_ms, sink_ps = {}, {}, {}
    for unit in units:
        g = unit[2]
        s_b, s_m = s_bs[unit], s_ms[unit]
        sink = jnp.concatenate(
            [jnp.full((1, BLOCK), sinks_ref[g * A_GROUP + j] * LOG2E, F32) for j in range(A_GROUP)], axis=1)
        m = jnp.maximum(jnp.max(s_b, axis=0, keepdims=True), jnp.max(s_m, axis=0, keepdims=True))
        m = jnp.maximum(m, sink)
        p_bs[unit] = jnp.exp2(s_b - m).astype(BF16)
        p_ms[unit] = jnp.exp2(s_m - m).astype(BF16)
        sink_ps[unit] = jnp.exp2(sink - m)
    for unit in units:
        sq, sb, g = unit
        row0 = B_WIDTH + g * HEAD_DIM
        prev_blk, own_blk = blks[sb]
        p_b = p_bs[unit]
        zero = jnp.zeros_like(p_b)
        o_aug = (_dot(v_tile(sq, own_blk, 1, row0), jnp.where(own_key, p_b, zero))
                 + _dot(v_tile(sq, prev_blk, 1, row0), jnp.where(own_key, zero, p_b))
                 + _dot(v_meta(row0), p_ms[unit]))
        o_t = o_aug[:HEAD_DIM] / (o_aug[HEAD_DIM:HEAD_DIM + 1] + sink_ps[unit])
        for p in range(2):
            pair_t = jnp.concatenate([o_t[:, (2 * p) * BLOCK:(2 * p + 1) * BLOCK],
                                      o_t[:, (2 * p + 1) * BLOCK:(2 * p + 2) * BLOCK]], axis=0)
            oa_ref[sq, sb * BLOCK:(sb + 1) * BLOCK, (2 * g + p) * LANES:(2 * g + p + 1) * LANES] = (
                pair_t.T.astype(BF16))

    lane_row = lax.broadcasted_iota(jnp.int32, (1, LANES), 1)
    chains = [(sq, h) for sq in seqs for h in range(B_HEADS)]

    def head_cols(h):
        return slice(h * LANES, (h + 1) * LANES)

    q_aug = {}
    for sq, h in chains:
        first = _bias_lane(h, 0)
        bias_ones = jnp.where((lane_row >= first) & (lane_row < first + BIAS_PIECES), 1.0, 0.0).astype(BF16)
        q_pair = qb_ref[sq, :, (h // 2) * LANES:(h // 2 + 1) * LANES]
        q_aug[sq, h] = jnp.where(low if h % 2 == 0 else ~low, q_pair, bias_ones)

    row = lax.broadcasted_iota(jnp.int32, (KV_TILE, Q_TILE), 0)
    col = lax.broadcasted_iota(jnp.int32, (KV_TILE, Q_TILE), 1)
    causal = jnp.where(row <= col, 0.0, NEG)

    scores = {c: _dot_nt(kbm_ref[:, head_cols(c[1])], q_aug[c]) for c in chains}
    probs = {}
    for c in chains:
        m0 = jnp.max(scores[c], axis=0, keepdims=True)
        m_ref[c] = m0
        probs[c] = jnp.exp2(scores[c] - m0).astype(BF16)
    for c in chains:
        acc_ref[c] = _dot(v_meta(c[1] * HEAD_DIM), probs[c])

    def update(blk, n_blk, mask):
        k0 = pl.multiple_of(blk * BLOCK, KV_TILE)
        rows = pl.ds(k0, n_blk * BLOCK)
        scores = {c: _dot_nt(kb_ref[c[0], rows, head_cols(c[1])], q_aug[c]) for c in chains}
        probs, alphas = {}, {}
        for c in chains:
            s = scores[c]
            if mask is not None:
                s = s + mask
            m_old = m_ref[c]
            m_new = jnp.maximum(m_old, jnp.max(s, axis=0, keepdims=True))
            m_ref[c] = m_new
            alphas[c] = jnp.exp2(m_old - m_new)
            probs[c] = jnp.exp2(s - m_new).astype(BF16)
        for c in chains:
            acc_ref[c] = alphas[c] * acc_ref[c] + _dot(v_tile(c[0], blk, n_blk, c[1] * HEAD_DIM), probs[c])

    per_tile = KV_TILE // BLOCK

    def body(j, carry):
        update(TILES_PER_TRIP * per_tile * j, TILES_PER_TRIP * per_tile, None)
        return carry

    lax.fori_loop(0, lax.div(i, TILES_PER_TRIP), body, 0)

    def finalize():
        for sq in seqs:
            for pr in range(B_HEADS // 2):
                acc0, acc1 = acc_ref[sq, 2 * pr], acc_ref[sq, 2 * pr + 1]
                o_t = jnp.concatenate([acc0[:HEAD_DIM] / acc0[HEAD_DIM:HEAD_DIM + 1],
                                       acc1[:HEAD_DIM] / acc1[HEAD_DIM:HEAD_DIM + 1]], axis=0)
                ob_ref[sq, :, pr * LANES:(pr + 1) * LANES] = o_t.T.astype(BF16)

    for left in range(TILES_PER_TRIP):
        @pl.when(lax.rem(i, TILES_PER_TRIP) == left)
        def _():
            mask = jnp.concatenate([jnp.zeros((left * KV_TILE, Q_TILE), F32), causal], axis=0) if left else causal
            update(per_tile * (i - left), (left + 1) * per_tile, mask)
            finalize()


def _attention(sinks, qa, ka, qb, kb, vt, kam, kbm, vtm, tables, batch, seq):
    nq = seq // Q_TILE
    group = batch // SEQS_PER_STEP
    grid = (group, nq)

    def grouped(a):
        return a.reshape((SEQS_PER_STEP, a.shape[0] // SEQS_PER_STEP) + a.shape[1:])

    def q_spec(cols):
        return pl.BlockSpec((SEQS_PER_STEP, Q_TILE, cols), lambda b, i: (0, b * nq + i, 0))

    def kv_spec(cols):
        return pl.BlockSpec((SEQS_PER_STEP, seq, cols), lambda b, i: (0, b, 0))

    in_specs = [
        pl.BlockSpec(memory_space=pltpu.SMEM),
        q_spec(A_WIDTH), kv_spec(256), q_spec(B_WIDTH), kv_spec(B_HEADS * LANES),
        pl.BlockSpec((SEQS_PER_STEP, seq // LANES, V_ROWS, LANES), lambda b, i: (0, b, 0, 0)),
    ]
    in_specs += [_const_spec(a.shape) for a in (kam, kbm, vtm) + tuple(tables)]
    out_shape = [jax.ShapeDtypeStruct((SEQS_PER_STEP, group * seq, A_WIDTH), BF16),
                 jax.ShapeDtypeStruct((SEQS_PER_STEP, group * seq, B_WIDTH), BF16)]
    out_specs = [q_spec(A_WIDTH), q_spec(B_WIDTH)]
    scratch = [
        pltpu.VMEM((SEQS_PER_STEP, B_HEADS, 1, Q_TILE), F32),
        pltpu.VMEM((SEQS_PER_STEP, B_HEADS, HEAD_DIM + ONES_ROWS, Q_TILE), F32),
    ]
    oa, ob = pl.pallas_call(
        _attn_kernel,
        grid=grid,
        in_specs=in_specs,
        out_specs=out_specs,
        out_shape=out_shape,
        scratch_shapes=scratch,
        compiler_params=pltpu.CompilerParams(
            dimension_semantics=("arbitrary", "arbitrary"), vmem_limit_bytes=VMEM_LIMIT_BYTES),
        name="attention",
    )(sinks, grouped(qa), grouped(ka), grouped(qb), grouped(kb), grouped(vt), kam, kbm, vtm, *tables)
    return oa.reshape(batch * seq, A_WIDTH), ob.reshape(batch * seq, B_WIDTH)


def _mix_ffn_kernel(h1_ref, oa_ref, ob_ref, ga_ref, gb_ref, wa_ref, wb_ref, wo_ref,
                    n2_ref, w2i_ref, w2o_ref, nf_ref, out_ref):
    ya = _dot(oa_ref[...], wa_ref[...])
    yb = _dot(ob_ref[...], wb_ref[...])
    mixed = (ga_ref[...].astype(F32) * ya + gb_ref[...].astype(F32) * yb).astype(BF16)
    h2 = h1_ref[...] + _dot(mixed, wo_ref[...])
    xn = _rms(h2, n2_ref[...]).astype(BF16)
    h3 = h2 + 0.5 * _swiglu(xn, w2i_ref, w2o_ref)
    out_ref[...] = _rms(h3, nf_ref[...])


def _mix_ffn(h1, oa, ob, ga, gb, wa, wb, wo, n2, w2i, w2o, nf, tile):
    rows = h1.shape[0]

    def row_spec(cols):
        return pl.BlockSpec((tile, cols), lambda i: (i, 0))

    in_specs = [row_spec(D_MODEL), row_spec(A_WIDTH), row_spec(B_WIDTH), row_spec(D_MODEL), row_spec(D_MODEL)]
    in_specs += [_const_spec(w.shape) for w in (wa, wb, wo, n2, w2i, w2o, nf)]
    return pl.pallas_call(
        _mix_ffn_kernel,
        grid=(rows // tile,),
        in_specs=in_specs,
        out_specs=row_spec(D_MODEL),
        out_shape=jax.ShapeDtypeStruct((rows, D_MODEL), F32),
        compiler_params=pltpu.CompilerParams(
            dimension_semantics=("arbitrary",), vmem_limit_bytes=VMEM_LIMIT_BYTES),
        name="mix_ffn2",
    )(h1, oa, ob, ga, gb, wa, wb, wo, n2, w2i, w2o, nf)


def _prep_proj_weight(w):
    qa = w[:, 0:512]
    ka = w[:, 512:640]
    va = w[:, 640:768]
    qb, kb, vb = w[:, 768:1280], w[:, 1280:1792], w[:, 1792:2304]
    wf = w[:, 2304:2312]
    g_a, g_b = w[:, 2312:3336], w[:, 3336:4360]
    k0, k1 = ka[:, :HEAD_DIM], ka[:, HEAD_DIM:]
    ka_dup = jnp.concatenate([k0, k0, k1, k1], axis=1)
    wf_pad = jnp.pad(wf, ((0, 0), (0, LANES - B_HEADS)))
    wp = jnp.concatenate([qa, ka_dup, qb, kb, g_a, g_b, wf_pad], axis=1).astype(BF16)
    wv = jnp.concatenate([vb, va], axis=1).astype(BF16)
    return wp, wv


def _attention_tables():
    slopes = 2.0 ** -(np.arange(1, A_HEADS + 1, dtype=np.float64))
    k = np.arange(BLOCK)[:, None]
    r = np.arange(BLOCK)[None, :]
    band = np.empty((2, A_KV_HEADS, BLOCK, A_GROUP * BLOCK), np.float32)
    meta = np.empty((A_KV_HEADS, N_META, A_GROUP * BLOCK), np.float32)
    slope_rows = np.empty((A_KV_HEADS, 1, A_GROUP * BLOCK), np.float32)
    mk = np.arange(N_META)[:, None]
    for h in range(A_HEADS):
        g, j = divmod(h, A_GROUP)
        cols = slice(j * BLOCK, (j + 1) * BLOCK)
        dist = np.where(k <= r, r - k, r - k + BLOCK)
        band[1, g, :, cols] = -slopes[h] * dist
        band[0, g, :, cols] = np.where(k <= r, -slopes[h] * dist, NEG)
        meta[g, :, cols] = -slopes[h] * (r + N_META - mk)
        slope_rows[g, :, cols] = slopes[h] * BLOCK
    band = np.where(band > NEG, band * LOG2E, NEG)
    tables = [t.astype(np.float32) for t in (band, meta * LOG2E, slope_rows * LOG2E)]
    return tuple(jnp.asarray(t) for t in tables)


def _bias_lane(h, p):
    return (HEAD_DIM if h % 2 == 0 else 0) + BIAS_PIECES * h + p


def _bias_placement():
    place = np.zeros((BIAS_PIECES, LANES, LANES), np.float32)
    for h in range(B_HEADS):
        for p in range(BIAS_PIECES):
            place[p, h, _bias_lane(h, p)] = 1.0
    return jnp.asarray(place, BF16)


def kernel(x, meta_tokens, ffn1_norm, ffn1_w_in, ffn1_w_out, mix_norm, w_in, b_forget, attn_sinks,
           w_branch_a, w_branch_b, w_out, ffn2_norm, ffn2_w_in, ffn2_w_out, final_norm):
    batch, seq, d = x.shape
    assert d == D_MODEL and seq % TOKEN_TILE == 0 and seq % Q_TILE == 0
    assert ffn1_norm.shape[0] == 1, "single layer"

    n1 = ffn1_norm[0][None].astype(F32)
    nm = mix_norm[0][None].astype(F32)
    n2 = ffn2_norm[0][None].astype(F32)
    nf = final_norm[None].astype(F32)
    w1i, w1o = ffn1_w_in[0].astype(BF16), ffn1_w_out[0].astype(BF16)
    w2i, w2o = ffn2_w_in[0].astype(BF16), ffn2_w_out[0].astype(BF16)
    wp, wv = _prep_proj_weight(w_in[0])
    bfg = jnp.pad(b_forget[0].astype(F32), (0, LANES - B_HEADS))[None]
    wa, wb, wo = w_branch_a[0].astype(BF16), w_branch_b[0].astype(BF16), w_out[0].astype(BF16)
    place = _bias_placement()

    x2d = x.reshape(batch * seq, D_MODEL)
    h1, qa, ka, qb, kb, ga, gb, vt = _ffn_proj(
        x2d, n1, w1i, w1o, nm, wp, wv.T, bfg, place, TOKEN_TILE, seq // TOKEN_TILE, False)
    _, _, kam, _, kbm, _, _, vm = _ffn_proj(
        meta_tokens.astype(F32), n1, w1i, w1o, nm, wp, wv, bfg, place, N_META, 1, True)

    oa, ob = _attention(attn_sinks[0].astype(F32), qa, ka, qb, kb, vt, kam, kbm, vm.T,
                        _attention_tables(), batch, seq)
    out = _mix_ffn(h1, oa, ob, ga, gb, wa, wb, wo, n2, w2i, w2o, nf, TOKEN_TILE)
    return out.reshape(batch, seq, D_MODEL)
```

```python
import jax
import jax.numpy as jnp
import numpy as np
from jax import lax
from jax.experimental import pallas as pl
from jax.experimental.pallas import tpu as pltpu

F32 = jnp.float32
BF16 = jnp.bfloat16

D_MODEL = 1024
N_META = 16
BLOCK = 128
HEAD_DIM = 64
A_HEADS = 8
A_KV_HEADS = 2
A_GROUP = A_HEADS // A_KV_HEADS
B_HEADS = 8
A_WIDTH = A_HEADS * HEAD_DIM
B_WIDTH = B_HEADS * HEAD_DIM
V_ROWS = B_WIDTH + A_KV_HEADS * HEAD_DIM
D_FF = 2816
EPS = 1e-6
NEG = -1e30
LOG2E = 1.4426950408889634
Q_SCALE = HEAD_DIM ** -0.5 * LOG2E

LANES = 128
TOKEN_TILE = 512
Q_TILE = 256
KV_TILE = 256
TILES_PER_TRIP = 2
SEQS_PER_STEP = 2
WINDOW_SEQS = 8
FF_CHUNKS = ((0, 1024), (1024, 1024), (2048, 768))
VMEM_LIMIT_BYTES = 56 * 1024 * 1024
BIAS_PIECES = 3
ONES_ROWS = 16

_QA, _KA, _QB, _KB, _GA, _GB, _FG = 0, 512, 768, 1280, 1792, 2816, 3840
PROJ_COLS = 3968


def _dot(a, b):
    return jnp.dot(a, b, preferred_element_type=F32)


def _dot_nt(a, b):
    return lax.dot_general(a, b, (((1,), (1,)), ((), ())), preferred_element_type=F32)


def _rms(x, g):
    ms = jnp.mean(x * x, axis=-1, keepdims=True)
    return x * lax.rsqrt(ms + EPS) * g


def _swiglu(xn, w_in_ref, w_out_ref):
    acc = None
    for lo, width in FF_CHUNKS:
        g = _dot(xn, w_in_ref[:, lo:lo + width])
        u = _dot(xn, w_in_ref[:, D_FF + lo:D_FF + lo + width])
        a = (g * jax.nn.sigmoid(g) * u).astype(BF16)
        part = _dot(a, w_out_ref[lo:lo + width, :])
        acc = part if acc is None else acc + part
    return acc


def _split3(x):
    hi = x.astype(BF16)
    r = x - hi.astype(F32)
    mid = r.astype(BF16)
    lo = (r - mid.astype(F32)).astype(BF16)
    return hi, mid, lo


def _ffn_proj_kernel(tiles_per_seq, is_meta,
                     x_ref, n1_ref, w1i_ref, w1o_ref, nm_ref, wp_ref, wv_ref, bf_ref, tri_ref, place_ref,
                     h1_ref, qa_ref, ka_ref, qb_ref, kb_ref, ga_ref, gb_ref, v_ref, carry_ref):
    @pl.when(pl.program_id(0) % tiles_per_seq == 0)
    def _():
        carry_ref[...] = jnp.zeros_like(carry_ref)

    x = x_ref[...]
    tile = x.shape[0]
    xn = _rms(x, n1_ref[...]).astype(BF16)
    h1 = x + 0.5 * _swiglu(xn, w1i_ref, w1o_ref)
    h1_ref[...] = h1
    u = _rms(h1, nm_ref[...]).astype(BF16)

    def proj(lo, width):
        return _dot(u, wp_ref[:, lo:lo + width])

    qa_ref[...] = (proj(_QA, 512) * Q_SCALE).astype(BF16)
    ka_ref[...] = proj(_KA, 256).astype(BF16)
    qb_ref[...] = (proj(_QB, 512) * Q_SCALE).astype(BF16)
    kb = proj(_KB, 512).astype(BF16)

    z = proj(_FG, LANES) + bf_ref[...]
    lane = lax.broadcasted_iota(jnp.int32, (1, LANES), 1)
    lf = jnp.where(lane < B_HEADS, jnp.minimum(z, 0.0) - jnp.log1p(jnp.exp(-jnp.abs(z))), 0.0)
    tri = tri_ref[...]
    c = carry_ref[...] + sum(_dot(tri, piece) for piece in _split3(lf))
    carry_ref[...] = c[tile - 1:tile, :]
    bias = ((c[tile - 1:tile, :] - c) if is_meta else -c) * LOG2E
    cb = sum(_dot(piece, place_ref[p]) for p, piece in enumerate(_split3(bias))).astype(BF16)
    low = lane < HEAD_DIM
    for h in range(B_HEADS):
        k_pair = kb[:, (h // 2) * LANES:(h // 2 + 1) * LANES]
        kb_ref[:, h * LANES:(h + 1) * LANES] = jnp.where(low if h % 2 == 0 else ~low, k_pair, cb)

    ga_ref[...] = jax.nn.sigmoid(proj(_GA, 1024)).astype(BF16)
    gb_ref[...] = jax.nn.sigmoid(proj(_GB, 1024)).astype(BF16)

    if is_meta:
        v_ref[...] = _dot(u, wv_ref[...]).astype(BF16)
    else:
        vt = _dot_nt(wv_ref[...], u).astype(BF16)
        for c_blk in range(tile // LANES):
            v_ref[c_blk] = vt[:, c_blk * LANES:(c_blk + 1) * LANES]


def _const_spec(shape):
    zeros = (0,) * len(shape)
    return pl.BlockSpec(shape, lambda *_: zeros, pipeline_mode=pl.Buffered(1))


def _ffn_proj(x2d, n1, w1i, w1o, nm, wp, wv, bfg, place, tile, tiles_per_seq, is_meta):
    rows = x2d.shape[0]
    grid = (rows // tile,)
    tri = jnp.asarray(np.tril(np.ones((tile, tile), np.float32)), BF16)

    def row_spec(cols):
        return pl.BlockSpec((tile, cols), lambda i: (i, 0))

    out_cols = (D_MODEL, 512, 256, 512, B_HEADS * LANES, 1024, 1024)
    out_dtypes = (F32,) + (BF16,) * 6
    out_shape = [jax.ShapeDtypeStruct((rows, c), dt) for c, dt in zip(out_cols, out_dtypes)]
    out_specs = [row_spec(c) for c in out_cols]
    if is_meta:
        out_shape.append(jax.ShapeDtypeStruct((rows, V_ROWS), BF16))
        out_specs.append(row_spec(V_ROWS))
    else:
        out_shape.append(jax.ShapeDtypeStruct((rows // LANES, V_ROWS, LANES), BF16))
        out_specs.append(pl.BlockSpec((tile // LANES, V_ROWS, LANES), lambda i: (i, 0, 0)))
    consts = (n1, w1i, w1o, nm, wp, wv, bfg, tri, place)
    in_specs = [row_spec(D_MODEL)] + [_const_spec(a.shape) for a in consts]
    return pl.pallas_call(
        lambda *refs: _ffn_proj_kernel(tiles_per_seq, is_meta, *refs),
        grid=grid,
        in_specs=in_specs,
        out_specs=out_specs,
        out_shape=out_shape,
        scratch_shapes=[pltpu.VMEM((1, LANES), F32)],
        compiler_params=pltpu.CompilerParams(
            dimension_semantics=("arbitrary",), vmem_limit_bytes=VMEM_LIMIT_BYTES),
        name="ffn1_proj_meta" if is_meta else "ffn1_proj",
    )(x2d, *consts)


def _split_heads(q, low):
    zero = jnp.zeros_like(q)
    return jnp.concatenate([jnp.where(low, q, zero), jnp.where(low, zero, q)], axis=0)


def _forget_attn_kernel(qb_ref, kb_ref, vt_ref, kbm_ref, vtm_ref, ob_ref, m_ref, acc_ref):
    i = pl.program_id(1)
    low = lax.broadcasted_iota(jnp.int32, (1, LANES), 1) < HEAD_DIM
    seqs = range(SEQS_PER_STEP)

    def v_tile(sq, blk, n_blk, row0):
        vals = jnp.concatenate([vt_ref[sq, blk + b, row0:row0 + HEAD_DIM, :] for b in range(n_blk)], axis=1)
        return jnp.concatenate([vals, jnp.ones((ONES_ROWS, n_blk * BLOCK), BF16)], axis=0)

    def v_meta(row0):
        return jnp.concatenate([vtm_ref[row0:row0 + HEAD_DIM, :], jnp.ones((ONES_ROWS, N_META), BF16)], axis=0)

    lane_row = lax.broadcasted_iota(jnp.int32, (1, LANES), 1)
    chains = [(sq, h) for sq in seqs for h in range(B_HEADS)]

    def head_cols(h):
        return slice(h * LANES, (h + 1) * LANES)

    q_aug = {}
    for sq, h in chains:
        first = _bias_lane(h, 0)
        bias_ones = jnp.where((lane_row >= first) & (lane_row < first + BIAS_PIECES), 1.0, 0.0).astype(BF16)
        q_pair = qb_ref[sq, :, (h // 2) * LANES:(h // 2 + 1) * LANES]
        q_aug[sq, h] = jnp.where(low if h % 2 == 0 else ~low, q_pair, bias_ones)

    row = lax.broadcasted_iota(jnp.int32, (KV_TILE, Q_TILE), 0)
    col = lax.broadcasted_iota(jnp.int32, (KV_TILE, Q_TILE), 1)
    causal = jnp.where(row <= col, 0.0, NEG)

    scores = {c: _dot_nt(kbm_ref[:, head_cols(c[1])], q_aug[c]) for c in chains}
    probs = {}
    for c in chains:
        m0 = jnp.max(scores[c], axis=0, keepdims=True)
        m_ref[c] = m0
        probs[c] = jnp.exp2(scores[c] - m0).astype(BF16)
    for c in chains:
        acc_ref[c] = _dot(v_meta(c[1] * HEAD_DIM), probs[c])

    def update(blk, n_blk, mask):
        k0 = pl.multiple_of(blk * BLOCK, KV_TILE)
        rows = pl.ds(k0, n_blk * BLOCK)
        scores = {c: _dot_nt(kb_ref[c[0], rows, head_cols(c[1])], q_aug[c]) for c in chains}
        probs, alphas = {}, {}
        for c in chains:
            s = scores[c]
            if mask is not None:
                s = s + mask
            m_old = m_ref[c]
            m_new = jnp.maximum(m_old, jnp.max(s, axis=0, keepdims=True))
            m_ref[c] = m_new
            alphas[c] = jnp.exp2(m_old - m_new)
            probs[c] = jnp.exp2(s - m_new).astype(BF16)
        for c in chains:
            acc_ref[c] = alphas[c] * acc_ref[c] + _dot(v_tile(c[0], blk, n_blk, c[1] * HEAD_DIM), probs[c])

    per_tile = KV_TILE // BLOCK

    def body(j, carry):
        update(TILES_PER_TRIP * per_tile * j, TILES_PER_TRIP * per_tile, None)
        return carry

    lax.fori_loop(0, lax.div(i, TILES_PER_TRIP), body, 0)

    def finalize():
        for sq in seqs:
            for pr in range(B_HEADS // 2):
                acc0, acc1 = acc_ref[sq, 2 * pr], acc_ref[sq, 2 * pr + 1]
                o_t = jnp.concatenate([acc0[:HEAD_DIM] / acc0[HEAD_DIM:HEAD_DIM + 1],
                                       acc1[:HEAD_DIM] / acc1[HEAD_DIM:HEAD_DIM + 1]], axis=0)
                ob_ref[sq, :, pr * LANES:(pr + 1) * LANES] = o_t.T.astype(BF16)

    for left in range(TILES_PER_TRIP):
        @pl.when(lax.rem(i, TILES_PER_TRIP) == left)
        def _():
            mask = jnp.concatenate([jnp.zeros((left * KV_TILE, Q_TILE), F32), causal], axis=0) if left else causal
            update(per_tile * (i - left), (left + 1) * per_tile, mask)
            finalize()


def _grouped(a, n):
    return a.reshape((n, a.shape[0] // n) + a.shape[1:])


def _forget_attention(qb, kb, vt, kbm, vtm, batch, seq):
    nq = seq // Q_TILE
    group = batch // SEQS_PER_STEP
    q_spec = pl.BlockSpec((SEQS_PER_STEP, Q_TILE, B_WIDTH), lambda b, i: (0, b * nq + i, 0))
    in_specs = [
        q_spec,
        pl.BlockSpec((SEQS_PER_STEP, seq, B_HEADS * LANES), lambda b, i: (0, b, 0)),
        pl.BlockSpec((SEQS_PER_STEP, seq // LANES, V_ROWS, LANES), lambda b, i: (0, b, 0, 0)),
        _const_spec(kbm.shape), _const_spec(vtm.shape),
    ]
    scratch = [
        pltpu.VMEM((SEQS_PER_STEP, B_HEADS, 1, Q_TILE), F32),
        pltpu.VMEM((SEQS_PER_STEP, B_HEADS, HEAD_DIM + ONES_ROWS, Q_TILE), F32),
    ]
    ob = pl.pallas_call(
        _forget_attn_kernel,
        grid=(group, nq),
        in_specs=in_specs,
        out_specs=q_spec,
        out_shape=jax.ShapeDtypeStruct((SEQS_PER_STEP, group * seq, B_WIDTH), BF16),
        scratch_shapes=scratch,
        compiler_params=pltpu.CompilerParams(
            dimension_semantics=("arbitrary", "arbitrary"), vmem_limit_bytes=VMEM_LIMIT_BYTES),
        name="forget_attention",
    )(_grouped(qb, SEQS_PER_STEP), _grouped(kb, SEQS_PER_STEP), _grouped(vt, SEQS_PER_STEP), kbm, vtm)
    return ob.reshape(batch * seq, B_WIDTH)


def _window_attn_kernel(sinks_ref, qa_ref, kprev_ref, kown_ref, vprev_ref, vown_ref, kam_ref, vtm_ref,
                        band_ref, mtab_ref, slope_ref, oa_ref):
    n = pl.program_id(1)
    low = lax.broadcasted_iota(jnp.int32, (1, LANES), 1) < HEAD_DIM
    own_key = (lax.broadcasted_iota(jnp.int32, (BLOCK, A_GROUP * BLOCK), 0)
               <= (lax.broadcasted_iota(jnp.int32, (BLOCK, A_GROUP * BLOCK), 1) & (BLOCK - 1)))
    ones = jnp.ones((ONES_ROWS, BLOCK), BF16)
    units = [(sq, g) for sq in range(qa_ref.shape[0]) for g in range(A_KV_HEADS)]
    band = [band_ref[jnp.minimum(n, 1), g] for g in range(A_KV_HEADS)]
    meta_bias = [mtab_ref[g] - n.astype(F32) * slope_ref[g] for g in range(A_KV_HEADS)]

    s_bs, s_ms = {}, {}
    for sq, g in units:
        cols = slice(g * LANES, (g + 1) * LANES)
        q = qa_ref[sq]
        qq = jnp.concatenate(
            [_split_heads(q[:, (2 * g + p) * LANES:(2 * g + p + 1) * LANES], low) for p in range(2)], axis=0)
        s2 = _dot_nt(jnp.concatenate([kprev_ref[sq][:, cols], kown_ref[sq][:, cols]], axis=0), qq)
        s_bs[sq, g] = jnp.where(own_key, s2[BLOCK:], s2[:BLOCK]) + band[g]
        s_ms[sq, g] = _dot_nt(kam_ref[:, cols], qq) + meta_bias[g]
    p_bs, p_ms, sink_ps = {}, {}, {}
    for unit in units:
        g = unit[1]
        sink = jnp.concatenate(
            [jnp.full((1, BLOCK), sinks_ref[g * A_GROUP + j] * LOG2E, F32) for j in range(A_GROUP)], axis=1)
        m = jnp.maximum(jnp.max(s_bs[unit], axis=0, keepdims=True), jnp.max(s_ms[unit], axis=0, keepdims=True))
        m = jnp.maximum(m, sink)
        p_bs[unit] = jnp.exp2(s_bs[unit] - m).astype(BF16)
        p_ms[unit] = jnp.exp2(s_ms[unit] - m).astype(BF16)
        sink_ps[unit] = jnp.exp2(sink - m)
    for unit in units:
        sq, g = unit
        rows = slice(B_WIDTH + g * HEAD_DIM, B_WIDTH + (g + 1) * HEAD_DIM)
        p_b = p_bs[unit]
        zero = jnp.zeros_like(p_b)
        o_aug = (_dot(jnp.concatenate([vown_ref[sq, 0, rows, :], ones], axis=0), jnp.where(own_key, p_b, zero))
                 + _dot(jnp.concatenate([vprev_ref[sq, 0, rows, :], ones], axis=0), jnp.where(own_key, zero, p_b))
                 + _dot(jnp.concatenate([vtm_ref[rows, :], jnp.ones((ONES_ROWS, N_META), BF16)], axis=0),
                        p_ms[unit]))
        o_t = o_aug[:HEAD_DIM] / (o_aug[HEAD_DIM:HEAD_DIM + 1] + sink_ps[unit])
        for p in range(2):
            pair_t = jnp.concatenate([o_t[:, (2 * p) * BLOCK:(2 * p + 1) * BLOCK],
                                      o_t[:, (2 * p + 1) * BLOCK:(2 * p + 2) * BLOCK]], axis=0)
            oa_ref[sq, :, (2 * g + p) * LANES:(2 * g + p + 1) * LANES] = pair_t.T.astype(BF16)


def _window_attention(sinks, qa, ka, vt, kam, vtm, tables, batch, seq):
    groups = batch // WINDOW_SEQS
    nb = seq // BLOCK

    def blk_spec(cols, prev):
        return pl.BlockSpec((WINDOW_SEQS, BLOCK, cols),
                            lambda b, n: (b, jnp.maximum(n - 1, 0) if prev else n, 0))

    def v_spec(prev):
        return pl.BlockSpec((WINDOW_SEQS, 1, V_ROWS, LANES),
                            lambda b, n: (b, jnp.maximum(n - 1, 0) if prev else n, 0, 0))

    in_specs = [pl.BlockSpec(memory_space=pltpu.SMEM), blk_spec(A_WIDTH, False), blk_spec(256, True),
                blk_spec(256, False), v_spec(True), v_spec(False)]
    in_specs += [_const_spec(a.shape) for a in (kam, vtm) + tuple(tables)]
    qa3, ka3 = qa.reshape(batch, seq, A_WIDTH), ka.reshape(batch, seq, 256)
    vt4 = vt.reshape(batch, nb, V_ROWS, LANES)
    oa = pl.pallas_call(
        _window_attn_kernel,
        grid=(groups, nb),
        in_specs=in_specs,
        out_specs=blk_spec(A_WIDTH, False),
        out_shape=jax.ShapeDtypeStruct((batch, seq, A_WIDTH), BF16),
        compiler_params=pltpu.CompilerParams(
            dimension_semantics=("arbitrary", "arbitrary"), vmem_limit_bytes=VMEM_LIMIT_BYTES),
        name="window_attention",
    )(sinks, qa3, ka3, ka3, vt4, vt4, kam, vtm, *tables)
    return oa.reshape(batch * seq, A_WIDTH)


def _mix_ffn_kernel(h1_ref, oa_ref, ob_ref, ga_ref, gb_ref, wa_ref, wb_ref, wo_ref,
                    n2_ref, w2i_ref, w2o_ref, nf_ref, out_ref):
    ya = _dot(oa_ref[...], wa_ref[...])
    yb = _dot(ob_ref[...], wb_ref[...])
    mixed = (ga_ref[...].astype(F32) * ya + gb_ref[...].astype(F32) * yb).astype(BF16)
    h2 = h1_ref[...] + _dot(mixed, wo_ref[...])
    xn = _rms(h2, n2_ref[...]).astype(BF16)
    h3 = h2 + 0.5 * _swiglu(xn, w2i_ref, w2o_ref)
    out_ref[...] = _rms(h3, nf_ref[...])


def _mix_ffn(h1, oa, ob, ga, gb, wa, wb, wo, n2, w2i, w2o, nf, tile):
    rows = h1.shape[0]

    def row_spec(cols):
        return pl.BlockSpec((tile, cols), lambda i: (i, 0))

    in_specs = [row_spec(D_MODEL), row_spec(A_WIDTH), row_spec(B_WIDTH), row_spec(D_MODEL), row_spec(D_MODEL)]
    in_specs += [_const_spec(w.shape) for w in (wa, wb, wo, n2, w2i, w2o, nf)]
    return pl.pallas_call(
        _mix_ffn_kernel,
        grid=(rows // tile,),
        in_specs=in_specs,
        out_specs=row_spec(D_MODEL),
        out_shape=jax.ShapeDtypeStruct((rows, D_MODEL), F32),
        compiler_params=pltpu.CompilerParams(
            dimension_semantics=("arbitrary",), vmem_limit_bytes=VMEM_LIMIT_BYTES),
        name="mix_ffn2",
    )(h1, oa, ob, ga, gb, wa, wb, wo, n2, w2i, w2o, nf)


def _prep_proj_weight(w):
    qa = w[:, 0:512]
    ka = w[:, 512:640]
    va = w[:, 640:768]
    qb, kb, vb = w[:, 768:1280], w[:, 1280:1792], w[:, 1792:2304]
    wf = w[:, 2304:2312]
    g_a, g_b = w[:, 2312:3336], w[:, 3336:4360]
    k0, k1 = ka[:, :HEAD_DIM], ka[:, HEAD_DIM:]
    ka_dup = jnp.concatenate([k0, k0, k1, k1], axis=1)
    wf_pad = jnp.pad(wf, ((0, 0), (0, LANES - B_HEADS)))
    wp = jnp.concatenate([qa, ka_dup, qb, kb, g_a, g_b, wf_pad], axis=1).astype(BF16)
    wv = jnp.concatenate([vb, va], axis=1).astype(BF16)
    return wp, wv


def _attention_tables():
    slopes = 2.0 ** -(np.arange(1, A_HEADS + 1, dtype=np.float64))
    k = np.arange(BLOCK)[:, None]
    r = np.arange(BLOCK)[None, :]
    band = np.empty((2, A_KV_HEADS, BLOCK, A_GROUP * BLOCK), np.float32)
    meta = np.empty((A_KV_HEADS, N_META, A_GROUP * BLOCK), np.float32)
    slope_rows = np.empty((A_KV_HEADS, 1, A_GROUP * BLOCK), np.float32)
    mk = np.arange(N_META)[:, None]
    for h in range(A_HEADS):
        g, j = divmod(h, A_GROUP)
        cols = slice(j * BLOCK, (j + 1) * BLOCK)
        dist = np.where(k <= r, r - k, r - k + BLOCK)
        band[1, g, :, cols] = -slopes[h] * dist
        band[0, g, :, cols] = np.where(k <= r, -slopes[h] * dist, NEG)
        meta[g, :, cols] = -slopes[h] * (r + N_META - mk)
        slope_rows[g, :, cols] = slopes[h] * BLOCK
    band = np.where(band > NEG, band * LOG2E, NEG)
    tables = [t.astype(np.float32) for t in (band, meta * LOG2E, slope_rows * LOG2E)]
    return tuple(jnp.asarray(t) for t in tables)


def _bias_lane(h, p):
    return (HEAD_DIM if h % 2 == 0 else 0) + BIAS_PIECES * h + p


def _bias_placement():
    place = np.zeros((BIAS_PIECES, LANES, LANES), np.float32)
    for h in range(B_HEADS):
        for p in range(BIAS_PIECES):
            place[p, h, _bias_lane(h, p)] = 1.0
    return jnp.asarray(place, BF16)


def kernel(x, meta_tokens, ffn1_norm, ffn1_w_in, ffn1_w_out, mix_norm, w_in, b_forget, attn_sinks,
           w_branch_a, w_branch_b, w_out, ffn2_norm, ffn2_w_in, ffn2_w_out, final_norm):
    batch, seq, d = x.shape
    assert d == D_MODEL and seq % TOKEN_TILE == 0 and seq % Q_TILE == 0
    assert ffn1_norm.shape[0] == 1, "single layer"

    n1 = ffn1_norm[0][None].astype(F32)
    nm = mix_norm[0][None].astype(F32)
    n2 = ffn2_norm[0][None].astype(F32)
    nf = final_norm[None].astype(F32)
    w1i, w1o = ffn1_w_in[0].astype(BF16), ffn1_w_out[0].astype(BF16)
    w2i, w2o = ffn2_w_in[0].astype(BF16), ffn2_w_out[0].astype(BF16)
    wp, wv = _prep_proj_weight(w_in[0])
    bfg = jnp.pad(b_forget[0].astype(F32), (0, LANES - B_HEADS))[None]
    wa, wb, wo = w_branch_a[0].astype(BF16), w_branch_b[0].astype(BF16), w_out[0].astype(BF16)
    place = _bias_placement()

    x2d = x.reshape(batch * seq, D_MODEL)
    h1, qa, ka, qb, kb, ga, gb, vt = _ffn_proj(
        x2d, n1, w1i, w1o, nm, wp, wv.T, bfg, place, TOKEN_TILE, seq // TOKEN_TILE, False)
    _, _, kam, _, kbm, _, _, vm = _ffn_proj(
        meta_tokens.astype(F32), n1, w1i, w1o, nm, wp, wv, bfg, place, N_META, 1, True)

    oa = _window_attention(attn_sinks[0].astype(F32), qa, ka, vt, kam, vm.T, _attention_tables(), batch, seq)
    ob = _forget_attention(qb, kb, vt, kbm, vm.T, batch, seq)
    out = _mix_ffn(h1, oa, ob, ga, gb, wa, wb, wo, n2, w2i, w2o, nf, TOKEN_TILE)
    return out.reshape(batch, seq, D_MODEL)
```

```python
import jax
import jax.numpy as jnp
import numpy as np
from jax import lax
from jax.experimental import pallas as pl
from jax.experimental.pallas import tpu as pltpu

F32 = jnp.float32
BF16 = jnp.bfloat16

D_MODEL = 1024
N_META = 16
BLOCK = 128
HEAD_DIM = 64
A_HEADS = 8
A_KV_HEADS = 2
A_GROUP = A_HEADS // A_KV_HEADS
B_HEADS = 8
A_WIDTH = A_HEADS * HEAD_DIM
B_WIDTH = B_HEADS * HEAD_DIM
V_ROWS = B_WIDTH + A_KV_HEADS * HEAD_DIM
D_FF = 2816
EPS = 1e-6
NEG = -1e30
LOG2E = 1.4426950408889634
Q_SCALE = HEAD_DIM ** -0.5 * LOG2E

LANES = 128
TOKEN_TILE = 512
ROW_BLOCKS = 2
Q_TILE = 256
KV_TILE = 256
TILES_PER_TRIP = 2
SEQS_PER_STEP = 2
FF_CHUNKS = ((0, 1024), (1024, 1024), (2048, 768))
VMEM_LIMIT_BYTES = 56 * 1024 * 1024
BIAS_PIECES = 3
ONES_ROWS = 16

_QA, _KA, _QB, _KB, _GA, _GB, _FG = 0, 512, 768, 1280, 1792, 2816, 3840
PROJ_COLS = 3968


def _dot(a, b):
    return jnp.dot(a, b, preferred_element_type=F32)


def _dot_nt(a, b):
    return lax.dot_general(a, b, (((1,), (1,)), ((), ())), preferred_element_type=F32)


def _rms(x, g):
    ms = jnp.mean(x * x, axis=-1, keepdims=True)
    return x * lax.rsqrt(ms + EPS) * g


def _swiglu(xns, w_in_ref, w_out_ref):
    accs = [None] * len(xns)
    for lo, width in FF_CHUNKS:
        gs = [_dot(xn, w_in_ref[:, lo:lo + width]) for xn in xns]
        us = [_dot(xn, w_in_ref[:, D_FF + lo:D_FF + lo + width]) for xn in xns]
        acts = [(g * jax.nn.sigmoid(g) * u).astype(BF16) for g, u in zip(gs, us)]
        parts = [_dot(a, w_out_ref[lo:lo + width, :]) for a in acts]
        accs = [part if acc is None else acc + part for acc, part in zip(accs, parts)]
    return accs


def _split3(x):
    hi = x.astype(BF16)
    r = x - hi.astype(F32)
    mid = r.astype(BF16)
    lo = (r - mid.astype(F32)).astype(BF16)
    return hi, mid, lo


def _ffn_proj_kernel(tiles_per_seq, is_meta,
                     x_ref, n1_ref, w1i_ref, w1o_ref, nm_ref, wp_ref, wv_ref, bf_ref, tri_ref, place_ref,
                     h1_ref, qa_ref, ka_ref, qb_ref, kb_ref, ga_ref, gb_ref, v_ref, carry_ref):
    @pl.when(pl.program_id(0) % tiles_per_seq == 0)
    def _():
        carry_ref[...] = jnp.zeros_like(carry_ref)

    x = x_ref[...]
    tile = x.shape[0]
    xn = _rms(x, n1_ref[...]).astype(BF16)
    h1 = x + 0.5 * _swiglu([xn], w1i_ref, w1o_ref)[0]
    h1_ref[...] = h1
    u = _rms(h1, nm_ref[...]).astype(BF16)

    def proj(lo, width):
        return _dot(u, wp_ref[:, lo:lo + width])

    qa_ref[...] = (proj(_QA, 512) * Q_SCALE).astype(BF16)
    ka_ref[...] = proj(_KA, 256).astype(BF16)
    qb_ref[...] = (proj(_QB, 512) * Q_SCALE).astype(BF16)
    kb = proj(_KB, 512).astype(BF16)

    z = proj(_FG, LANES) + bf_ref[...]
    lane = lax.broadcasted_iota(jnp.int32, (1, LANES), 1)
    lf = jnp.where(lane < B_HEADS, jnp.minimum(z, 0.0) - jnp.log1p(jnp.exp(-jnp.abs(z))), 0.0)
    tri = tri_ref[...]
    c = carry_ref[...] + sum(_dot(tri, piece) for piece in _split3(lf))
    carry_ref[...] = c[tile - 1:tile, :]
    bias = ((c[tile - 1:tile, :] - c) if is_meta else -c) * LOG2E
    cb = sum(_dot(piece, place_ref[p]) for p, piece in enumerate(_split3(bias))).astype(BF16)
    low = lane < HEAD_DIM
    for h in range(B_HEADS):
        k_pair = kb[:, (h // 2) * LANES:(h // 2 + 1) * LANES]
        kb_ref[:, h * LANES:(h + 1) * LANES] = jnp.where(low if h % 2 == 0 else ~low, k_pair, cb)

    ga_ref[...] = jax.nn.sigmoid(proj(_GA, 1024)).astype(BF16)
    gb_ref[...] = jax.nn.sigmoid(proj(_GB, 1024)).astype(BF16)

    if is_meta:
        v_ref[...] = _dot(u, wv_ref[...]).astype(BF16)
    else:
        vt = _dot_nt(wv_ref[...], u).astype(BF16)
        for c_blk in range(tile // LANES):
            v_ref[c_blk] = vt[:, c_blk * LANES:(c_blk + 1) * LANES]


def _const_spec(shape):
    zeros = (0,) * len(shape)
    return pl.BlockSpec(shape, lambda *_: zeros, pipeline_mode=pl.Buffered(1))


def _ffn_proj(x2d, n1, w1i, w1o, nm, wp, wv, bfg, place, tile, tiles_per_seq, is_meta):
    rows = x2d.shape[0]
    grid = (rows // tile,)
    tri = jnp.asarray(np.tril(np.ones((tile, tile), np.float32)), BF16)

    def row_spec(cols):
        return pl.BlockSpec((tile, cols), lambda i: (i, 0))

    out_cols = (D_MODEL, 512, 256, 512, B_HEADS * LANES, 1024, 1024)
    out_dtypes = (F32,) + (BF16,) * 6
    out_shape = [jax.ShapeDtypeStruct((rows, c), dt) for c, dt in zip(out_cols, out_dtypes)]
    out_specs = [row_spec(c) for c in out_cols]
    if is_meta:
        out_shape.append(jax.ShapeDtypeStruct((rows, V_ROWS), BF16))
        out_specs.append(row_spec(V_ROWS))
    else:
        out_shape.append(jax.ShapeDtypeStruct((rows // LANES, V_ROWS, LANES), BF16))
        out_specs.append(pl.BlockSpec((tile // LANES, V_ROWS, LANES), lambda i: (i, 0, 0)))
    consts = (n1, w1i, w1o, nm, wp, wv, bfg, tri, place)
    in_specs = [row_spec(D_MODEL)] + [_const_spec(a.shape) for a in consts]
    return pl.pallas_call(
        lambda *refs: _ffn_proj_kernel(tiles_per_seq, is_meta, *refs),
        grid=grid,
        in_specs=in_specs,
        out_specs=out_specs,
        out_shape=out_shape,
        scratch_shapes=[pltpu.VMEM((1, LANES), F32)],
        compiler_params=pltpu.CompilerParams(
            dimension_semantics=("arbitrary",), vmem_limit_bytes=VMEM_LIMIT_BYTES),
        name="ffn1_proj_meta" if is_meta else "ffn1_proj",
    )(x2d, *consts)


def _split_heads(q, low):
    zero = jnp.zeros_like(q)
    return jnp.concatenate([jnp.where(low, q, zero), jnp.where(low, zero, q)], axis=0)


def _attn_kernel(sinks_ref, qa_ref, ka_ref, qb_ref, kb_ref, vt_ref,
                 kam_ref, kbm_ref, vtm_ref, band_ref, mtab_ref, slope_ref,
                 oa_ref, ob_ref, m_ref, acc_ref):
    i = pl.program_id(1)
    low = lax.broadcasted_iota(jnp.int32, (1, LANES), 1) < HEAD_DIM
    sub = Q_TILE // BLOCK
    seqs = range(SEQS_PER_STEP)
    ones_meta = jnp.ones((ONES_ROWS, N_META), BF16)

    def v_tile(sq, blk, n_blk, row0):
        vals = jnp.concatenate([vt_ref[sq, blk + b, row0:row0 + HEAD_DIM, :] for b in range(n_blk)], axis=1)
        return jnp.concatenate([vals, jnp.ones((ONES_ROWS, n_blk * BLOCK), BF16)], axis=0)

    def v_meta(row0):
        return jnp.concatenate([vtm_ref[row0:row0 + HEAD_DIM, :], ones_meta], axis=0)

    own_key = (lax.broadcasted_iota(jnp.int32, (BLOCK, A_GROUP * BLOCK), 0)
               <= (lax.broadcasted_iota(jnp.int32, (BLOCK, A_GROUP * BLOCK), 1) & (BLOCK - 1)))
    units = [(sq, sb, g) for sq in seqs for sb in range(sub) for g in range(A_KV_HEADS)]
    blks, s_bs, s_ms = {}, {}, {}
    for sq, sb, g in units:
        n = i * sub + sb
        blks[sb] = (jnp.maximum(n - 1, 0), n)
        q = qa_ref[sq, sb * BLOCK:(sb + 1) * BLOCK, :]
        qq = jnp.concatenate(
            [_split_heads(q[:, (2 * g + p) * LANES:(2 * g + p + 1) * LANES], low) for p in range(2)], axis=0)
        kband = jnp.concatenate(
            [ka_ref[sq, pl.ds(pl.multiple_of(b * BLOCK, BLOCK), BLOCK), g * LANES:(g + 1) * LANES]
             for b in blks[sb]], axis=0)
        kmeta = kam_ref[:, g * LANES:(g + 1) * LANES]
        s2 = _dot_nt(kband, qq)
        s_bs[sq, sb, g] = jnp.where(own_key, s2[BLOCK:], s2[:BLOCK]) + band_ref[jnp.minimum(n, 1), g]
        s_ms[sq, sb, g] = _dot_nt(kmeta, qq) + mtab_ref[g] - n.astype(F32) * slope_ref[g]
    p_bs, p_ms, sink_ps = {}, {}, {}
    for unit in units:
        g = unit[2]
        s_b, s_m = s_bs[unit], s_ms[unit]
        sink = jnp.concatenate(
            [jnp.full((1, BLOCK), sinks_ref[g * A_GROUP + j] * LOG2E, F32) for j in range(A_GROUP)], axis=1)
        m = jnp.maximum(jnp.max(s_b, axis=0, keepdims=True), jnp.max(s_m, axis=0, keepdims=True))
        m = jnp.maximum(m, sink)
        p_bs[unit] = jnp.exp2(s_b - m).astype(BF16)
        p_ms[unit] = jnp.exp2(s_m - m).astype(BF16)
        sink_ps[unit] = jnp.exp2(sink - m)
    for unit in units:
        sq, sb, g = unit
        row0 = B_WIDTH + g * HEAD_DIM
        prev_blk, own_blk = blks[sb]
        p_b = p_bs[unit]
        zero = jnp.zeros_like(p_b)
        o_aug = (_dot(v_tile(sq, own_blk, 1, row0), jnp.where(own_key, p_b, zero))
                 + _dot(v_tile(sq, prev_blk, 1, row0), jnp.where(own_key, zero, p_b))
                 + _dot(v_meta(row0), p_ms[unit]))
        o_t = o_aug[:HEAD_DIM] / (o_aug[HEAD_DIM:HEAD_DIM + 1] + sink_ps[unit])
        for p in range(2):
            pair_t = jnp.concatenate([o_t[:, (2 * p) * BLOCK:(2 * p + 1) * BLOCK],
                                      o_t[:, (2 * p + 1) * BLOCK:(2 * p + 2) * BLOCK]], axis=0)
            oa_ref[sq, sb * BLOCK:(sb + 1) * BLOCK, (2 * g + p) * LANES:(2 * g + p + 1) * LANES] = (
                pair_t.T.astype(BF16))

    lane_row = lax.broadcasted_iota(jnp.int32, (1, LANES), 1)
    chains = [(sq, h) for sq in seqs for h in range(B_HEADS)]

    def head_cols(h):
        return slice(h * LANES, (h + 1) * LANES)

    q_aug = {}
    for sq, h in chains:
        first = _bias_lane(h, 0)
        bias_ones = jnp.where((lane_row >= first) & (lane_row < first + BIAS_PIECES), 1.0, 0.0).astype(BF16)
        q_pair = qb_ref[sq, :, (h // 2) * LANES:(h // 2 + 1) * LANES]
        q_aug[sq, h] = jnp.where(low if h % 2 == 0 else ~low, q_pair, bias_ones)

    row = lax.broadcasted_iota(jnp.int32, (KV_TILE, Q_TILE), 0)
    col = lax.broadcasted_iota(jnp.int32, (KV_TILE, Q_TILE), 1)
    causal = jnp.where(row <= col, 0.0, NEG)

    scores = {c: _dot_nt(kbm_ref[:, head_cols(c[1])], q_aug[c]) for c in chains}
    probs = {}
    for c in chains:
        m0 = jnp.max(scores[c], axis=0, keepdims=True)
        m_ref[c] = m0
        probs[c] = jnp.exp2(scores[c] - m0).astype(BF16)
    for c in chains:
        acc_ref[c] = _dot(v_meta(c[1] * HEAD_DIM), probs[c])

    def update(blk, n_blk, mask):
        k0 = pl.multiple_of(blk * BLOCK, KV_TILE)
        rows = pl.ds(k0, n_blk * BLOCK)
        scores = {c: _dot_nt(kb_ref[c[0], rows, head_cols(c[1])], q_aug[c]) for c in chains}
        probs, alphas = {}, {}
        for c in chains:
            s = scores[c]
            if mask is not None:
                s = s + mask
            m_old = m_ref[c]
            m_new = jnp.maximum(m_old, jnp.max(s, axis=0, keepdims=True))
            m_ref[c] = m_new
            alphas[c] = jnp.exp2(m_old - m_new)
            probs[c] = jnp.exp2(s - m_new).astype(BF16)
        for c in chains:
            acc_ref[c] = alphas[c] * acc_ref[c] + _dot(v_tile(c[0], blk, n_blk, c[1] * HEAD_DIM), probs[c])

    per_tile = KV_TILE // BLOCK

    def body(j, carry):
        update(TILES_PER_TRIP * per_tile * j, TILES_PER_TRIP * per_tile, None)
        return carry

    lax.fori_loop(0, lax.div(i, TILES_PER_TRIP), body, 0)

    def finalize():
        for sq in seqs:
            for pr in range(B_HEADS // 2):
                acc0, acc1 = acc_ref[sq, 2 * pr], acc_ref[sq, 2 * pr + 1]
                o_t = jnp.concatenate([acc0[:HEAD_DIM] / acc0[HEAD_DIM:HEAD_DIM + 1],
                                       acc1[:HEAD_DIM] / acc1[HEAD_DIM:HEAD_DIM + 1]], axis=0)
                ob_ref[sq, :, pr * LANES:(pr + 1) * LANES] = o_t.T.astype(BF16)

    for left in range(TILES_PER_TRIP):
        @pl.when(lax.rem(i, TILES_PER_TRIP) == left)
        def _():
            mask = jnp.concatenate([jnp.zeros((left * KV_TILE, Q_TILE), F32), causal], axis=0) if left else causal
            update(per_tile * (i - left), (left + 1) * per_tile, mask)
            finalize()


def _attention(sinks, qa, ka, qb, kb, vt, kam, kbm, vtm, tables, batch, seq):
    nq = seq // Q_TILE
    group = batch // SEQS_PER_STEP
    grid = (group, nq)

    def grouped(a):
        return a.reshape((SEQS_PER_STEP, a.shape[0] // SEQS_PER_STEP) + a.shape[1:])

    def q_spec(cols):
        return pl.BlockSpec((SEQS_PER_STEP, Q_TILE, cols), lambda b, i: (0, b * nq + i, 0))

    def kv_spec(cols):
        return pl.BlockSpec((SEQS_PER_STEP, seq, cols), lambda b, i: (0, b, 0))

    in_specs = [
        pl.BlockSpec(memory_space=pltpu.SMEM),
        q_spec(A_WIDTH), kv_spec(256), q_spec(B_WIDTH), kv_spec(B_HEADS * LANES),
        pl.BlockSpec((SEQS_PER_STEP, seq // LANES, V_ROWS, LANES), lambda b, i: (0, b, 0, 0)),
    ]
    in_specs += [_const_spec(a.shape) for a in (kam, kbm, vtm) + tuple(tables)]
    out_shape = [jax.ShapeDtypeStruct((SEQS_PER_STEP, group * seq, A_WIDTH), BF16),
                 jax.ShapeDtypeStruct((SEQS_PER_STEP, group * seq, B_WIDTH), BF16)]
    out_specs = [q_spec(A_WIDTH), q_spec(B_WIDTH)]
    scratch = [
        pltpu.VMEM((SEQS_PER_STEP, B_HEADS, 1, Q_TILE), F32),
        pltpu.VMEM((SEQS_PER_STEP, B_HEADS, HEAD_DIM + ONES_ROWS, Q_TILE), F32),
    ]
    oa, ob = pl.pallas_call(
        _attn_kernel,
        grid=grid,
        in_specs=in_specs,
        out_specs=out_specs,
        out_shape=out_shape,
        scratch_shapes=scratch,
        compiler_params=pltpu.CompilerParams(
            dimension_semantics=("arbitrary", "arbitrary"), vmem_limit_bytes=VMEM_LIMIT_BYTES),
        name="attention",
    )(sinks, grouped(qa), grouped(ka), grouped(qb), grouped(kb), grouped(vt), kam, kbm, vtm, *tables)
    return oa.reshape(batch * seq, A_WIDTH), ob.reshape(batch * seq, B_WIDTH)


def _mix_ffn_kernel(h1_ref, oa_ref, ob_ref, ga_ref, gb_ref, wa_ref, wb_ref, wo_ref,
                    n2_ref, w2i_ref, w2o_ref, nf_ref, out_ref):
    rows = h1_ref.shape[0] // ROW_BLOCKS
    blocks = [slice(r * rows, (r + 1) * rows) for r in range(ROW_BLOCKS)]
    yas = [_dot(oa_ref[b, :], wa_ref[...]) for b in blocks]
    ybs = [_dot(ob_ref[b, :], wb_ref[...]) for b in blocks]
    mixed = [(ga_ref[b, :].astype(F32) * ya + gb_ref[b, :].astype(F32) * yb).astype(BF16)
             for b, ya, yb in zip(blocks, yas, ybs)]
    h2s = [h1_ref[b, :] + _dot(mx, wo_ref[...]) for b, mx in zip(blocks, mixed)]
    xns = [_rms(h2, n2_ref[...]).astype(BF16) for h2 in h2s]
    ffns = _swiglu(xns, w2i_ref, w2o_ref)
    for b, h2, ffn in zip(blocks, h2s, ffns):
        out_ref[b, :] = _rms(h2 + 0.5 * ffn, nf_ref[...])


def _mix_ffn(h1, oa, ob, ga, gb, wa, wb, wo, n2, w2i, w2o, nf, tile):
    rows = h1.shape[0]

    def row_spec(cols):
        return pl.BlockSpec((tile, cols), lambda i: (i, 0))

    in_specs = [row_spec(D_MODEL), row_spec(A_WIDTH), row_spec(B_WIDTH), row_spec(D_MODEL), row_spec(D_MODEL)]
    in_specs += [_const_spec(w.shape) for w in (wa, wb, wo, n2, w2i, w2o, nf)]
    return pl.pallas_call(
        _mix_ffn_kernel,
        grid=(rows // tile,),
        in_specs=in_specs,
        out_specs=row_spec(D_MODEL),
        out_shape=jax.ShapeDtypeStruct((rows, D_MODEL), F32),
        compiler_params=pltpu.CompilerParams(
            dimension_semantics=("arbitrary",), vmem_limit_bytes=VMEM_LIMIT_BYTES),
        name="mix_ffn2",
    )(h1, oa, ob, ga, gb, wa, wb, wo, n2, w2i, w2o, nf)


def _prep_proj_weight(w):
    qa = w[:, 0:512]
    ka = w[:, 512:640]
    va = w[:, 640:768]
    qb, kb, vb = w[:, 768:1280], w[:, 1280:1792], w[:, 1792:2304]
    wf = w[:, 2304:2312]
    g_a, g_b = w[:, 2312:3336], w[:, 3336:4360]
    k0, k1 = ka[:, :HEAD_DIM], ka[:, HEAD_DIM:]
    ka_dup = jnp.concatenate([k0, k0, k1, k1], axis=1)
    wf_pad = jnp.pad(wf, ((0, 0), (0, LANES - B_HEADS)))
    wp = jnp.concatenate([qa, ka_dup, qb, kb, g_a, g_b, wf_pad], axis=1).astype(BF16)
    wv = jnp.concatenate([vb, va], axis=1).astype(BF16)
    return wp, wv


def _attention_tables():
    slopes = 2.0 ** -(np.arange(1, A_HEADS + 1, dtype=np.float64))
    k = np.arange(BLOCK)[:, None]
    r = np.arange(BLOCK)[None, :]
    band = np.empty((2, A_KV_HEADS, BLOCK, A_GROUP * BLOCK), np.float32)
    meta = np.empty((A_KV_HEADS, N_META, A_GROUP * BLOCK), np.float32)
    slope_rows = np.empty((A_KV_HEADS, 1, A_GROUP * BLOCK), np.float32)
    mk = np.arange(N_META)[:, None]
    for h in range(A_HEADS):
        g, j = divmod(h, A_GROUP)
        cols = slice(j * BLOCK, (j + 1) * BLOCK)
        dist = np.where(k <= r, r - k, r - k + BLOCK)
        band[1, g, :, cols] = -slopes[h] * dist
        band[0, g, :, cols] = np.where(k <= r, -slopes[h] * dist, NEG)
        meta[g, :, cols] = -slopes[h] * (r + N_META - mk)
        slope_rows[g, :, cols] = slopes[h] * BLOCK
    band = np.where(band > NEG, band * LOG2E, NEG)
    tables = [t.astype(np.float32) for t in (band, meta * LOG2E, slope_rows * LOG2E)]
    return tuple(jnp.asarray(t) for t in tables)


def _bias_lane(h, p):
    return (HEAD_DIM if h % 2 == 0 else 0) + BIAS_PIECES * h + p


def _bias_placement():
    place = np.zeros((BIAS_PIECES, LANES, LANES), np.float32)
    for h in range(B_HEADS):
        for p in range(BIAS_PIECES):
            place[p, h, _bias_lane(h, p)] = 1.0
    return jnp.asarray(place, BF16)


def kernel(x, meta_tokens, ffn1_norm, ffn1_w_in, ffn1_w_out, mix_norm, w_in, b_forget, attn_sinks,
           w_branch_a, w_branch_b, w_out, ffn2_norm, ffn2_w_in, ffn2_w_out, final_norm):
    batch, seq, d = x.shape
    assert d == D_MODEL and seq % TOKEN_TILE == 0 and seq % Q_TILE == 0
    assert ffn1_norm.shape[0] == 1, "single layer"

    n1 = ffn1_norm[0][None].astype(F32)
    nm = mix_norm[0][None].astype(F32)
    n2 = ffn2_norm[0][None].astype(F32)
    nf = final_norm[None].astype(F32)
    w1i, w1o = ffn1_w_in[0].astype(BF16), ffn1_w_out[0].astype(BF16)
    w2i, w2o = ffn2_w_in[0].astype(BF16), ffn2_w_out[0].astype(BF16)
    wp, wv = _prep_proj_weight(w_in[0])
    bfg = jnp.pad(b_forget[0].astype(F32), (0, LANES - B_HEADS))[None]
    wa, wb, wo = w_branch_a[0].astype(BF16), w_branch_b[0].astype(BF16), w_out[0].astype(BF16)
    place = _bias_placement()

    x2d = x.reshape(batch * seq, D_MODEL)
    h1, qa, ka, qb, kb, ga, gb, vt = _ffn_proj(
        x2d, n1, w1i, w1o, nm, wp, wv.T, bfg, place, TOKEN_TILE, seq // TOKEN_TILE, False)
    _, _, kam, _, kbm, _, _, vm = _ffn_proj(
        meta_tokens.astype(F32), n1, w1i, w1o, nm, wp, wv, bfg, place, N_META, 1, True)

    oa, ob = _attention(attn_sinks[0].astype(F32), qa, ka, qb, kb, vt, kam, kbm, vm.T,
                        _attention_tables(), batch, seq)
    out = _mix_ffn(h1, oa, ob, ga, gb, wa, wb, wo, n2, w2i, w2o, nf, TOKEN_TILE)
    return out.reshape(batch, seq, D_MODEL)
```

```python
import jax
import jax.numpy as jnp
import numpy as np
from jax import lax
from jax.experimental import pallas as pl
from jax.experimental.pallas import tpu as pltpu

F32 = jnp.float32
BF16 = jnp.bfloat16

D_MODEL = 1024
N_META = 16
BLOCK = 128
HEAD_DIM = 64
A_HEADS = 8
A_KV_HEADS = 2
A_GROUP = A_HEADS // A_KV_HEADS
B_HEADS = 8
A_WIDTH = A_HEADS * HEAD_DIM
B_WIDTH = B_HEADS * HEAD_DIM
V_ROWS = B_WIDTH + A_KV_HEADS * HEAD_DIM
D_FF = 2816
EPS = 1e-6
NEG = -1e30
LOG2E = 1.4426950408889634
Q_SCALE = HEAD_DIM ** -0.5 * LOG2E

LANES = 128
TOKEN_TILE = 512
ROW_BLOCKS = 2
Q_TILE = 256
KV_TILE = 256
TILES_PER_TRIP = 2
SEQS_PER_STEP = 2
FF_CHUNKS = ((0, 1024), (1024, 1024), (2048, 768))
VMEM_LIMIT_BYTES = 56 * 1024 * 1024
BIAS_PIECES = 3
ONES_ROWS = 16

def _dot(a, b):
    return jnp.dot(a, b, preferred_element_type=F32)


def _dot_nt(a, b):
    return lax.dot_general(a, b, (((1,), (1,)), ((), ())), preferred_element_type=F32)


def _rms(x, g):
    ms = jnp.mean(x * x, axis=-1, keepdims=True)
    return x * lax.rsqrt(ms + EPS) * g


def _swiglu(xns, w_in_ref, w_out_ref):
    accs = [None] * len(xns)
    for lo, width in FF_CHUNKS:
        gs = [_dot(xn, w_in_ref[:, lo:lo + width]) for xn in xns]
        us = [_dot(xn, w_in_ref[:, D_FF + lo:D_FF + lo + width]) for xn in xns]
        acts = [(g * jax.nn.sigmoid(g) * u).astype(BF16) for g, u in zip(gs, us)]
        parts = [_dot(a, w_out_ref[lo:lo + width, :]) for a in acts]
        accs = [part if acc is None else acc + part for acc, part in zip(accs, parts)]
    return accs


def _split3(x):
    hi = x.astype(BF16)
    r = x - hi.astype(F32)
    mid = r.astype(BF16)
    lo = (r - mid.astype(F32)).astype(BF16)
    return hi, mid, lo


def _ffn_proj_kernel(tiles_per_seq, is_meta,
                     x_ref, n1_ref, w1i_ref, w1o_ref, nm_ref,
                     wqa_ref, wka_ref, wqb_ref, wkb_ref, wga_ref, wgb_ref, wfg_ref, wv_ref,
                     bf_ref, tri_ref, place_ref,
                     h1_ref, qa_ref, ka_ref, qb_ref, kb_ref, ga_ref, gb_ref, v_ref, carry_ref):
    @pl.when(pl.program_id(0) % tiles_per_seq == 0)
    def _():
        carry_ref[...] = jnp.zeros_like(carry_ref)

    x = x_ref[...]
    tile = x.shape[0]
    xn = _rms(x, n1_ref[...]).astype(BF16)
    h1 = x + 0.5 * _swiglu([xn], w1i_ref, w1o_ref)[0]
    h1_ref[...] = h1
    u = _rms(h1, nm_ref[...]).astype(BF16)

    def proj(w_ref):
        return _dot(u, w_ref[...])

    qa_ref[...] = (proj(wqa_ref) * Q_SCALE).astype(BF16)
    ka_ref[...] = proj(wka_ref).astype(BF16)
    qb_ref[...] = (proj(wqb_ref) * Q_SCALE).astype(BF16)
    kb = proj(wkb_ref).astype(BF16)

    z = proj(wfg_ref) + bf_ref[...]
    lane = lax.broadcasted_iota(jnp.int32, (1, LANES), 1)
    lf = jnp.where(lane < B_HEADS, jnp.minimum(z, 0.0) - jnp.log1p(jnp.exp(-jnp.abs(z))), 0.0)
    tri = tri_ref[...]
    c = carry_ref[...] + sum(_dot(tri, piece) for piece in _split3(lf))
    carry_ref[...] = c[tile - 1:tile, :]
    bias = ((c[tile - 1:tile, :] - c) if is_meta else -c) * LOG2E
    cb = sum(_dot(piece, place_ref[p]) for p, piece in enumerate(_split3(bias))).astype(BF16)
    low = lane < HEAD_DIM
    for h in range(B_HEADS):
        k_pair = kb[:, (h // 2) * LANES:(h // 2 + 1) * LANES]
        kb_ref[:, h * LANES:(h + 1) * LANES] = jnp.where(low if h % 2 == 0 else ~low, k_pair, cb)

    ga_ref[...] = jax.nn.sigmoid(proj(wga_ref)).astype(BF16)
    gb_ref[...] = jax.nn.sigmoid(proj(wgb_ref)).astype(BF16)

    if is_meta:
        v_ref[...] = _dot(u, wv_ref[...]).astype(BF16)
    else:
        vt = _dot_nt(wv_ref[...], u).astype(BF16)
        for c_blk in range(tile // LANES):
            v_ref[c_blk] = vt[:, c_blk * LANES:(c_blk + 1) * LANES]


def _const_spec(shape):
    zeros = (0,) * len(shape)
    return pl.BlockSpec(shape, lambda *_: zeros, pipeline_mode=pl.Buffered(1))


def _ffn_proj(x2d, n1, w1i, w1o, nm, wp, wv, bfg, place, tile, tiles_per_seq, is_meta):
    rows = x2d.shape[0]
    grid = (rows // tile,)
    tri = jnp.asarray(np.tril(np.ones((tile, tile), np.float32)), BF16)

    def row_spec(cols):
        return pl.BlockSpec((tile, cols), lambda i: (i, 0))

    out_cols = (D_MODEL, 512, 256, 512, B_HEADS * LANES, 1024, 1024)
    out_dtypes = (F32,) + (BF16,) * 6
    out_shape = [jax.ShapeDtypeStruct((rows, c), dt) for c, dt in zip(out_cols, out_dtypes)]
    out_specs = [row_spec(c) for c in out_cols]
    if is_meta:
        out_shape.append(jax.ShapeDtypeStruct((rows, V_ROWS), BF16))
        out_specs.append(row_spec(V_ROWS))
    else:
        out_shape.append(jax.ShapeDtypeStruct((rows // LANES, V_ROWS, LANES), BF16))
        out_specs.append(pl.BlockSpec((tile // LANES, V_ROWS, LANES), lambda i: (i, 0, 0)))
    consts = (n1, w1i, w1o, nm) + tuple(wp) + (wv, bfg, tri, place)
    in_specs = [row_spec(D_MODEL)] + [_const_spec(a.shape) for a in consts]
    return pl.pallas_call(
        lambda *refs: _ffn_proj_kernel(tiles_per_seq, is_meta, *refs),
        grid=grid,
        in_specs=in_specs,
        out_specs=out_specs,
        out_shape=out_shape,
        scratch_shapes=[pltpu.VMEM((1, LANES), F32)],
        compiler_params=pltpu.CompilerParams(
            dimension_semantics=("arbitrary",), vmem_limit_bytes=VMEM_LIMIT_BYTES),
        name="ffn1_proj_meta" if is_meta else "ffn1_proj",
    )(x2d, *consts)


def _split_heads(q, low):
    zero = jnp.zeros_like(q)
    return jnp.concatenate([jnp.where(low, q, zero), jnp.where(low, zero, q)], axis=0)


def _attn_kernel(sinks_ref, qa_ref, ka_ref, qb_ref, kb_ref, vt_ref,
                 kam_ref, kbm_ref, vtm_ref, band_ref, mtab_ref, slope_ref,
                 oa_ref, ob_ref, m_ref, acc_ref):
    i = pl.program_id(1)
    low = lax.broadcasted_iota(jnp.int32, (1, LANES), 1) < HEAD_DIM
    sub = Q_TILE // BLOCK
    seqs = range(SEQS_PER_STEP)
    ones_meta = jnp.ones((ONES_ROWS, N_META), BF16)

    def v_tile(sq, blk, n_blk, row0):
        vals = jnp.concatenate([vt_ref[sq, blk + b, row0:row0 + HEAD_DIM, :] for b in range(n_blk)], axis=1)
        return jnp.concatenate([vals, jnp.ones((ONES_ROWS, n_blk * BLOCK), BF16)], axis=0)

    def v_meta(row0):
        return jnp.concatenate([vtm_ref[row0:row0 + HEAD_DIM, :], ones_meta], axis=0)

    own_key = (lax.broadcasted_iota(jnp.int32, (BLOCK, A_GROUP * BLOCK), 0)
               <= (lax.broadcasted_iota(jnp.int32, (BLOCK, A_GROUP * BLOCK), 1) & (BLOCK - 1)))
    units = [(sq, sb, g) for sq in seqs for sb in range(sub) for g in range(A_KV_HEADS)]
    blks, s_bs, s_ms = {}, {}, {}
    for sq, sb, g in units:
        n = i * sub + sb
        blks[sb] = (jnp.maximum(n - 1, 0), n)
        q = qa_ref[sq, sb * BLOCK:(sb + 1) * BLOCK, :]
        qq = jnp.concatenate(
            [_split_heads(q[:, (2 * g + p) * LANES:(2 * g + p + 1) * LANES], low) for p in range(2)], axis=0)
        kband = jnp.concatenate(
            [ka_ref[sq, pl.ds(pl.multiple_of(b * BLOCK, BLOCK), BLOCK), g * LANES:(g + 1) * LANES]
             for b in blks[sb]], axis=0)
        kmeta = kam_ref[:, g * LANES:(g + 1) * LANES]
        s2 = _dot_nt(kband, qq)
        s_bs[sq, sb, g] = jnp.where(own_key, s2[BLOCK:], s2[:BLOCK]) + band_ref[jnp.minimum(n, 1), g]
        s_ms[sq, sb, g] = _dot_nt(kmeta, qq) + mtab_ref[g] - n.astype(F32) * slope_ref[g]
    p_bs, p_ms, sink_ps = {}, {}, {}
    for unit in units:
        g = unit[2]
        s_b, s_m = s_bs[unit], s_ms[unit]
        sink = jnp.concatenate(
            [jnp.full((1, BLOCK), sinks_ref[g * A_GROUP + j] * LOG2E, F32) for j in range(A_GROUP)], axis=1)
        m = jnp.maximum(jnp.max(s_b, axis=0, keepdims=True), jnp.max(s_m, axis=0, keepdims=True))
        m = jnp.maximum(m, sink)
        p_bs[unit] = jnp.exp2(s_b - m).astype(BF16)
        p_ms[unit] = jnp.exp2(s_m - m).astype(BF16)
        sink_ps[unit] = jnp.exp2(sink - m)
    for unit in units:
        sq, sb, g = unit
        row0 = B_WIDTH + g * HEAD_DIM
        prev_blk, own_blk = blks[sb]
        p_b = p_bs[unit]
        zero = jnp.zeros_like(p_b)
        o_aug = (_dot(v_tile(sq, own_blk, 1, row0), jnp.where(own_key, p_b, zero))
                 + _dot(v_tile(sq, prev_blk, 1, row0), jnp.where(own_key, zero, p_b))
                 + _dot(v_meta(row0), p_ms[unit]))
        o_t = o_aug[:HEAD_DIM] / (o_aug[HEAD_DIM:HEAD_DIM + 1] + sink_ps[unit])
        for p in range(2):
            pair_t = jnp.concatenate([o_t[:, (2 * p) * BLOCK:(2 * p + 1) * BLOCK],
                                      o_t[:, (2 * p + 1) * BLOCK:(2 * p + 2) * BLOCK]], axis=0)
            oa_ref[sq, sb * BLOCK:(sb + 1) * BLOCK, (2 * g + p) * LANES:(2 * g + p + 1) * LANES] = (
                pair_t.T.astype(BF16))

    lane_row = lax.broadcasted_iota(jnp.int32, (1, LANES), 1)
    chains = [(sq, h) for sq in seqs for h in range(B_HEADS)]

    def head_cols(h):
        return slice(h * LANES, (h + 1) * LANES)

    q_aug = {}
    for sq, h in chains:
        first = _bias_lane(h, 0)
        bias_ones = jnp.where((lane_row >= first) & (lane_row < first + BIAS_PIECES), 1.0, 0.0).astype(BF16)
        q_pair = qb_ref[sq, :, (h // 2) * LANES:(h // 2 + 1) * LANES]
        q_aug[sq, h] = jnp.where(low if h % 2 == 0 else ~low, q_pair, bias_ones)

    row = lax.broadcasted_iota(jnp.int32, (KV_TILE, Q_TILE), 0)
    col = lax.broadcasted_iota(jnp.int32, (KV_TILE, Q_TILE), 1)
    causal = jnp.where(row <= col, 0.0, NEG)

    scores = {c: _dot_nt(kbm_ref[:, head_cols(c[1])], q_aug[c]) for c in chains}
    probs = {}
    for c in chains:
        m0 = jnp.max(scores[c], axis=0, keepdims=True)
        m_ref[c] = m0
        probs[c] = jnp.exp2(scores[c] - m0).astype(BF16)
    for c in chains:
        acc_ref[c] = _dot(v_meta(c[1] * HEAD_DIM), probs[c])

    def update(blk, n_blk, mask):
        k0 = pl.multiple_of(blk * BLOCK, KV_TILE)
        rows = pl.ds(k0, n_blk * BLOCK)
        scores = {c: _dot_nt(kb_ref[c[0], rows, head_cols(c[1])], q_aug[c]) for c in chains}
        probs, alphas = {}, {}
        for c in chains:
            s = scores[c]
            if mask is not None:
                s = s + mask
            m_old = m_ref[c]
            m_new = jnp.maximum(m_old, jnp.max(s, axis=0, keepdims=True))
            m_ref[c] = m_new
            alphas[c] = jnp.exp2(m_old - m_new)
            probs[c] = jnp.exp2(s - m_new).astype(BF16)
        for c in chains:
            acc_ref[c] = alphas[c] * acc_ref[c] + _dot(v_tile(c[0], blk, n_blk, c[1] * HEAD_DIM), probs[c])

    per_tile = KV_TILE // BLOCK

    def body(j, carry):
        update(TILES_PER_TRIP * per_tile * j, TILES_PER_TRIP * per_tile, None)
        return carry

    lax.fori_loop(0, lax.div(i, TILES_PER_TRIP), body, 0)

    def finalize():
        for sq in seqs:
            for pr in range(B_HEADS // 2):
                acc0, acc1 = acc_ref[sq, 2 * pr], acc_ref[sq, 2 * pr + 1]
                o_t = jnp.concatenate([acc0[:HEAD_DIM] / acc0[HEAD_DIM:HEAD_DIM + 1],
                                       acc1[:HEAD_DIM] / acc1[HEAD_DIM:HEAD_DIM + 1]], axis=0)
                ob_ref[sq, :, pr * LANES:(pr + 1) * LANES] = o_t.T.astype(BF16)

    for left in range(TILES_PER_TRIP):
        @pl.when(lax.rem(i, TILES_PER_TRIP) == left)
        def _():
            mask = jnp.concatenate([jnp.zeros((left * KV_TILE, Q_TILE), F32), causal], axis=0) if left else causal
            update(per_tile * (i - left), (left + 1) * per_tile, mask)
            finalize()


def _attention(sinks, qa, ka, qb, kb, vt, kam, kbm, vtm, tables, batch, seq):
    nq = seq // Q_TILE
    group = batch // SEQS_PER_STEP
    grid = (group, nq)

    def grouped(a):
        return a.reshape((SEQS_PER_STEP, a.shape[0] // SEQS_PER_STEP) + a.shape[1:])

    def q_spec(cols):
        return pl.BlockSpec((SEQS_PER_STEP, Q_TILE, cols), lambda b, i: (0, b * nq + i, 0))

    def kv_spec(cols):
        return pl.BlockSpec((SEQS_PER_STEP, seq, cols), lambda b, i: (0, b, 0))

    in_specs = [
        pl.BlockSpec(memory_space=pltpu.SMEM),
        q_spec(A_WIDTH), kv_spec(256), q_spec(B_WIDTH), kv_spec(B_HEADS * LANES),
        pl.BlockSpec((SEQS_PER_STEP, seq // LANES, V_ROWS, LANES), lambda b, i: (0, b, 0, 0)),
    ]
    in_specs += [_const_spec(a.shape) for a in (kam, kbm, vtm) + tuple(tables)]
    out_shape = [jax.ShapeDtypeStruct((SEQS_PER_STEP, group * seq, A_WIDTH), BF16),
                 jax.ShapeDtypeStruct((SEQS_PER_STEP, group * seq, B_WIDTH), BF16)]
    out_specs = [q_spec(A_WIDTH), q_spec(B_WIDTH)]
    scratch = [
        pltpu.VMEM((SEQS_PER_STEP, B_HEADS, 1, Q_TILE), F32),
        pltpu.VMEM((SEQS_PER_STEP, B_HEADS, HEAD_DIM + ONES_ROWS, Q_TILE), F32),
    ]
    oa, ob = pl.pallas_call(
        _attn_kernel,
        grid=grid,
        in_specs=in_specs,
        out_specs=out_specs,
        out_shape=out_shape,
        scratch_shapes=scratch,
        compiler_params=pltpu.CompilerParams(
            dimension_semantics=("arbitrary", "arbitrary"), vmem_limit_bytes=VMEM_LIMIT_BYTES),
        name="attention",
    )(sinks, grouped(qa), grouped(ka), grouped(qb), grouped(kb), grouped(vt), kam, kbm, vtm, *tables)
    return oa.reshape(batch * seq, A_WIDTH), ob.reshape(batch * seq, B_WIDTH)


def _mix_ffn_kernel(h1_ref, oa_ref, ob_ref, ga_ref, gb_ref, wa_ref, wb_ref, wo_ref,
                    n2_ref, w2i_ref, w2o_ref, nf_ref, out_ref):
    rows = h1_ref.shape[0] // ROW_BLOCKS
    blocks = [slice(r * rows, (r + 1) * rows) for r in range(ROW_BLOCKS)]
    yas = [_dot(oa_ref[b, :], wa_ref[...]) for b in blocks]
    ybs = [_dot(ob_ref[b, :], wb_ref[...]) for b in blocks]
    mixed = [(ga_ref[b, :].astype(F32) * ya + gb_ref[b, :].astype(F32) * yb).astype(BF16)
             for b, ya, yb in zip(blocks, yas, ybs)]
    h2s = [h1_ref[b, :] + _dot(mx, wo_ref[...]) for b, mx in zip(blocks, mixed)]
    xns = [_rms(h2, n2_ref[...]).astype(BF16) for h2 in h2s]
    ffns = _swiglu(xns, w2i_ref, w2o_ref)
    for b, h2, ffn in zip(blocks, h2s, ffns):
        out_ref[b, :] = _rms(h2 + 0.5 * ffn, nf_ref[...])


def _mix_ffn(h1, oa, ob, ga, gb, wa, wb, wo, n2, w2i, w2o, nf, tile):
    rows = h1.shape[0]

    def row_spec(cols):
        return pl.BlockSpec((tile, cols), lambda i: (i, 0))

    in_specs = [row_spec(D_MODEL), row_spec(A_WIDTH), row_spec(B_WIDTH), row_spec(D_MODEL), row_spec(D_MODEL)]
    in_specs += [_const_spec(w.shape) for w in (wa, wb, wo, n2, w2i, w2o, nf)]
    return pl.pallas_call(
        _mix_ffn_kernel,
        grid=(rows // tile,),
        in_specs=in_specs,
        out_specs=row_spec(D_MODEL),
        out_shape=jax.ShapeDtypeStruct((rows, D_MODEL), F32),
        compiler_params=pltpu.CompilerParams(
            dimension_semantics=("arbitrary",), vmem_limit_bytes=VMEM_LIMIT_BYTES),
        name="mix_ffn2",
    )(h1, oa, ob, ga, gb, wa, wb, wo, n2, w2i, w2o, nf)


def _prep_proj_weight(w):
    qa = w[:, 0:512]
    ka = w[:, 512:640]
    va = w[:, 640:768]
    qb, kb, vb = w[:, 768:1280], w[:, 1280:1792], w[:, 1792:2304]
    wf = w[:, 2304:2312]
    g_a, g_b = w[:, 2312:3336], w[:, 3336:4360]
    k0, k1 = ka[:, :HEAD_DIM], ka[:, HEAD_DIM:]
    ka_dup = jnp.concatenate([k0, k0, k1, k1], axis=1)
    wf_pad = jnp.pad(wf, ((0, 0), (0, LANES - B_HEADS)))
    wp = tuple(a.astype(BF16) for a in (qa, ka_dup, qb, kb, g_a, g_b, wf_pad))
    wv = jnp.concatenate([vb, va], axis=1).astype(BF16)
    return wp, wv


def _attention_tables():
    slopes = 2.0 ** -(np.arange(1, A_HEADS + 1, dtype=np.float64))
    k = np.arange(BLOCK)[:, None]
    r = np.arange(BLOCK)[None, :]
    band = np.empty((2, A_KV_HEADS, BLOCK, A_GROUP * BLOCK), np.float32)
    meta = np.empty((A_KV_HEADS, N_META, A_GROUP * BLOCK), np.float32)
    slope_rows = np.empty((A_KV_HEADS, 1, A_GROUP * BLOCK), np.float32)
    mk = np.arange(N_META)[:, None]
    for h in range(A_HEADS):
        g, j = divmod(h, A_GROUP)
        cols = slice(j * BLOCK, (j + 1) * BLOCK)
        dist = np.where(k <= r, r - k, r - k + BLOCK)
        band[1, g, :, cols] = -slopes[h] * dist
        band[0, g, :, cols] = np.where(k <= r, -slopes[h] * dist, NEG)
        meta[g, :, cols] = -slopes[h] * (r + N_META - mk)
        slope_rows[g, :, cols] = slopes[h] * BLOCK
    band = np.where(band > NEG, band * LOG2E, NEG)
    tables = [t.astype(np.float32) for t in (band, meta * LOG2E, slope_rows * LOG2E)]
    return tuple(jnp.asarray(t) for t in tables)


def _bias_lane(h, p):
    return (HEAD_DIM if h % 2 == 0 else 0) + BIAS_PIECES * h + p


def _bias_placement():
    place = np.zeros((BIAS_PIECES, LANES, LANES), np.float32)
    for h in range(B_HEADS):
        for p in range(BIAS_PIECES):
            place[p, h, _bias_lane(h, p)] = 1.0
    return jnp.asarray(place, BF16)


def kernel(x, meta_tokens, ffn1_norm, ffn1_w_in, ffn1_w_out, mix_norm, w_in, b_forget, attn_sinks,
           w_branch_a, w_branch_b, w_out, ffn2_norm, ffn2_w_in, ffn2_w_out, final_norm):
    batch, seq, d = x.shape
    assert d == D_MODEL and seq % TOKEN_TILE == 0 and seq % Q_TILE == 0
    assert ffn1_norm.shape[0] == 1, "single layer"

    n1 = ffn1_norm[0][None].astype(F32)
    nm = mix_norm[0][None].astype(F32)
    n2 = ffn2_norm[0][None].astype(F32)
    nf = final_norm[None].astype(F32)
    w1i, w1o = ffn1_w_in[0].astype(BF16), ffn1_w_out[0].astype(BF16)
    w2i, w2o = ffn2_w_in[0].astype(BF16), ffn2_w_out[0].astype(BF16)
    wp, wv = _prep_proj_weight(w_in[0])
    bfg = jnp.pad(b_forget[0].astype(F32), (0, LANES - B_HEADS))[None]
    wa, wb, wo = w_branch_a[0].astype(BF16), w_branch_b[0].astype(BF16), w_out[0].astype(BF16)
    place = _bias_placement()

    x2d = x.reshape(batch * seq, D_MODEL)
    h1, qa, ka, qb, kb, ga, gb, vt = _ffn_proj(
        x2d, n1, w1i, w1o, nm, wp, wv.T, bfg, place, TOKEN_TILE, seq // TOKEN_TILE, False)
    _, _, kam, _, kbm, _, _, vm = _ffn_proj(
        meta_tokens.astype(F32), n1, w1i, w1o, nm, wp, wv, bfg, place, N_META, 1, True)

    oa, ob = _attention(attn_sinks[0].astype(F32), qa, ka, qb, kb, vt, kam, kbm, vm.T,
                        _attention_tables(), batch, seq)
    out = _mix_ffn(h1, oa, ob, ga, gb, wa, wb, wo, n2, w2i, w2o, nf, TOKEN_TILE)
    return out.reshape(batch, seq, D_MODEL)
```

```python
import jax
import jax.numpy as jnp
import numpy as np
from jax import lax
from jax.experimental import pallas as pl
from jax.experimental.pallas import tpu as pltpu

F32 = jnp.float32
BF16 = jnp.bfloat16

D_MODEL = 1024
N_META = 16
BLOCK = 128
HEAD_DIM = 64
A_HEADS = 8
A_KV_HEADS = 2
A_GROUP = A_HEADS // A_KV_HEADS
B_HEADS = 8
A_WIDTH = A_HEADS * HEAD_DIM
B_WIDTH = B_HEADS * HEAD_DIM
V_ROWS = B_WIDTH + A_KV_HEADS * HEAD_DIM
D_FF = 2816
EPS = 1e-6
NEG = -1e30
LOG2E = 1.4426950408889634
Q_SCALE = HEAD_DIM ** -0.5 * LOG2E

LANES = 128
TOKEN_TILE = 512
ROW_BLOCKS = 2
Q_TILE = 256
KV_TILE = 256
TILES_PER_TRIP = 2
SEQS_PER_STEP = 2
FF_CHUNKS = ((0, 1024), (1024, 1024), (2048, 768))
VMEM_LIMIT_BYTES = 56 * 1024 * 1024
BIAS_PIECES = 3
ONES_ROWS = 16

A_KEY_COLS = 2 * A_KV_HEADS * HEAD_DIM
_PROJ_WIDTHS = (A_WIDTH, A_KEY_COLS, B_WIDTH, B_WIDTH, D_MODEL, D_MODEL, LANES)
_QA, _KA, _QB, _KB, _GA, _GB, _FG = ((sum(_PROJ_WIDTHS[:j]), w) for j, w in enumerate(_PROJ_WIDTHS))


def _dot(a, b):
    return jnp.dot(a, b, preferred_element_type=F32)


def _dot_nt(a, b):
    return lax.dot_general(a, b, (((1,), (1,)), ((), ())), preferred_element_type=F32)


def _rms(x, g):
    ms = jnp.mean(x * x, axis=-1, keepdims=True)
    return x * lax.rsqrt(ms + EPS) * g


def _swiglu(xns, w_in_ref, w_out_ref):
    accs = [None] * len(xns)
    for lo, width in FF_CHUNKS:
        gs = [_dot(xn, w_in_ref[:, lo:lo + width]) for xn in xns]
        us = [_dot(xn, w_in_ref[:, D_FF + lo:D_FF + lo + width]) for xn in xns]
        acts = [(g * jax.nn.sigmoid(g) * u).astype(BF16) for g, u in zip(gs, us)]
        parts = [_dot(a, w_out_ref[lo:lo + width, :]) for a in acts]
        accs = [part if acc is None else acc + part for acc, part in zip(accs, parts)]
    return accs


def _split3(x):
    hi = x.astype(BF16)
    r = x - hi.astype(F32)
    mid = r.astype(BF16)
    lo = (r - mid.astype(F32)).astype(BF16)
    return hi, mid, lo


def _ffn_proj_kernel(tiles_per_seq, is_meta,
                     x_ref, n1_ref, w1i_ref, w1o_ref, nm_ref, wp_ref, wv_ref, bf_ref, tri_ref, place_ref,
                     h1_ref, qa_ref, ka_ref, qb_ref, kb_ref, ga_ref, gb_ref, v_ref, carry_ref):
    @pl.when(pl.program_id(0) % tiles_per_seq == 0)
    def _():
        carry_ref[...] = jnp.zeros_like(carry_ref)

    x = x_ref[...]
    tile = x.shape[0]
    xn = _rms(x, n1_ref[...]).astype(BF16)
    h1 = x + 0.5 * _swiglu([xn], w1i_ref, w1o_ref)[0]
    h1_ref[...] = h1
    u = _rms(h1, nm_ref[...]).astype(BF16)

    def proj(lo, width):
        return _dot(u, wp_ref[:, lo:lo + width])

    qa_ref[...] = (proj(*_QA) * Q_SCALE).astype(BF16)
    ka_ref[...] = proj(*_KA).astype(BF16)
    qb_ref[...] = (proj(*_QB) * Q_SCALE).astype(BF16)
    kb = proj(*_KB).astype(BF16)

    z = proj(*_FG) + bf_ref[...]
    lane = lax.broadcasted_iota(jnp.int32, (1, LANES), 1)
    lf = jnp.where(lane < B_HEADS, jnp.minimum(z, 0.0) - jnp.log1p(jnp.exp(-jnp.abs(z))), 0.0)
    tri = tri_ref[...]
    c = carry_ref[...] + sum(_dot(tri, piece) for piece in _split3(lf))
    carry_ref[...] = c[tile - 1:tile, :]
    bias = ((c[tile - 1:tile, :] - c) if is_meta else -c) * LOG2E
    cb = sum(_dot(piece, place_ref[p]) for p, piece in enumerate(_split3(bias))).astype(BF16)
    low = lane < HEAD_DIM
    for h in range(B_HEADS):
        k_pair = kb[:, (h // 2) * LANES:(h // 2 + 1) * LANES]
        kb_ref[:, h * LANES:(h + 1) * LANES] = jnp.where(low if h % 2 == 0 else ~low, k_pair, cb)

    ga_ref[...] = jax.nn.sigmoid(proj(*_GA)).astype(BF16)
    gb_ref[...] = jax.nn.sigmoid(proj(*_GB)).astype(BF16)

    if is_meta:
        v_ref[...] = _dot(u, wv_ref[...]).astype(BF16)
    else:
        vt = _dot_nt(wv_ref[...], u).astype(BF16)
        for c_blk in range(tile // LANES):
            v_ref[c_blk] = vt[:, c_blk * LANES:(c_blk + 1) * LANES]


def _const_spec(shape):
    zeros = (0,) * len(shape)
    return pl.BlockSpec(shape, lambda *_: zeros, pipeline_mode=pl.Buffered(1))


def _ffn_proj(x2d, n1, w1i, w1o, nm, wp, wv, bfg, place, tile, tiles_per_seq, is_meta):
    rows = x2d.shape[0]
    grid = (rows // tile,)
    tri = jnp.asarray(np.tril(np.ones((tile, tile), np.float32)), BF16)

    def row_spec(cols):
        return pl.BlockSpec((tile, cols), lambda i: (i, 0))

    out_cols = (D_MODEL, A_WIDTH, A_KEY_COLS, B_WIDTH, B_HEADS * LANES, D_MODEL, D_MODEL)
    out_dtypes = (F32,) + (BF16,) * 6
    out_shape = [jax.ShapeDtypeStruct((rows, c), dt) for c, dt in zip(out_cols, out_dtypes)]
    out_specs = [row_spec(c) for c in out_cols]
    if is_meta:
        out_shape.append(jax.ShapeDtypeStruct((rows, V_ROWS), BF16))
        out_specs.append(row_spec(V_ROWS))
    else:
        out_shape.append(jax.ShapeDtypeStruct((rows // LANES, V_ROWS, LANES), BF16))
        out_specs.append(pl.BlockSpec((tile // LANES, V_ROWS, LANES), lambda i: (i, 0, 0)))
    consts = (n1, w1i, w1o, nm, wp, wv, bfg, tri, place)
    in_specs = [row_spec(D_MODEL)] + [_const_spec(a.shape) for a in consts]
    return pl.pallas_call(
        lambda *refs: _ffn_proj_kernel(tiles_per_seq, is_meta, *refs),
        grid=grid,
        in_specs=in_specs,
        out_specs=out_specs,
        out_shape=out_shape,
        scratch_shapes=[pltpu.VMEM((1, LANES), F32)],
        compiler_params=pltpu.CompilerParams(
            dimension_semantics=("arbitrary",), vmem_limit_bytes=VMEM_LIMIT_BYTES),
        name="ffn1_proj_meta" if is_meta else "ffn1_proj",
    )(x2d, *consts)


def _split_heads(q, low):
    zero = jnp.zeros_like(q)
    return jnp.concatenate([jnp.where(low, q, zero), jnp.where(low, zero, q)], axis=0)


def _attn_kernel(sinks_ref, qa_ref, ka_ref, qb_ref, kb_ref, vt_ref,
                 kam_ref, kbm_ref, vtm_ref, band_ref, mtab_ref, slope_ref,
                 oa_ref, ob_ref, m_ref, acc_ref):
    i = pl.program_id(1)
    low = lax.broadcasted_iota(jnp.int32, (1, LANES), 1) < HEAD_DIM
    sub = Q_TILE // BLOCK
    seqs = range(SEQS_PER_STEP)
    ones_meta = jnp.ones((ONES_ROWS, N_META), BF16)

    def v_tile(sq, blk, n_blk, row0):
        vals = jnp.concatenate([vt_ref[sq, blk + b, row0:row0 + HEAD_DIM, :] for b in range(n_blk)], axis=1)
        return jnp.concatenate([vals, jnp.ones((ONES_ROWS, n_blk * BLOCK), BF16)], axis=0)

    def v_meta(row0):
        return jnp.concatenate([vtm_ref[row0:row0 + HEAD_DIM, :], ones_meta], axis=0)

    own_key = (lax.broadcasted_iota(jnp.int32, (BLOCK, A_GROUP * BLOCK), 0)
               <= (lax.broadcasted_iota(jnp.int32, (BLOCK, A_GROUP * BLOCK), 1) & (BLOCK - 1)))
    units = [(sq, sb, g) for sq in seqs for sb in range(sub) for g in range(A_KV_HEADS)]
    blks, s_bs, s_ms = {}, {}, {}
    for sq, sb, g in units:
        n = i * sub + sb
        blks[sb] = (jnp.maximum(n - 1, 0), n)
        q = qa_ref[sq, sb * BLOCK:(sb + 1) * BLOCK, :]
        qq = jnp.concatenate(
            [_split_heads(q[:, (2 * g + p) * LANES:(2 * g + p + 1) * LANES], low) for p in range(2)], axis=0)
        kband = jnp.concatenate(
            [ka_ref[sq, pl.ds(pl.multiple_of(b * BLOCK, BLOCK), BLOCK), g * LANES:(g + 1) * LANES]
             for b in blks[sb]], axis=0)
        kmeta = kam_ref[:, g * LANES:(g + 1) * LANES]
        s2 = _dot_nt(kband, qq)
        s_bs[sq, sb, g] = jnp.where(own_key, s2[BLOCK:], s2[:BLOCK]) + band_ref[jnp.minimum(n, 1), g]
        s_ms[sq, sb, g] = _dot_nt(kmeta, qq) + mtab_ref[g] - n.astype(F32) * slope_ref[g]
    p_bs, p_ms, sink_ps = {}, {}, {}
    for unit in units:
        g = unit[2]
        s_b, s_m = s_bs[unit], s_ms[unit]
        sink = jnp.concatenate(
            [jnp.full((1, BLOCK), sinks_ref[g * A_GROUP + j] * LOG2E, F32) for j in range(A_GROUP)], axis=1)
        m = jnp.maximum(jnp.max(s_b, axis=0, keepdims=True), jnp.max(s_m, axis=0, keepdims=True))
        m = jnp.maximum(m, sink)
        p_bs[unit] = jnp.exp2(s_b - m).astype(BF16)
        p_ms[unit] = jnp.exp2(s_m - m).astype(BF16)
        sink_ps[unit] = jnp.exp2(sink - m)
    for unit in units:
        sq, sb, g = unit
        row0 = B_WIDTH + g * HEAD_DIM
        prev_blk, own_blk = blks[sb]
        p_b = p_bs[unit]
        zero = jnp.zeros_like(p_b)
        o_aug = (_dot(v_tile(sq, own_blk, 1, row0), jnp.where(own_key, p_b, zero))
                 + _dot(v_tile(sq, prev_blk, 1, row0), jnp.where(own_key, zero, p_b))
                 + _dot(v_meta(row0), p_ms[unit]))
        o_t = o_aug[:HEAD_DIM] / (o_aug[HEAD_DIM:HEAD_DIM + 1] + sink_ps[unit])
        for p in range(2):
            pair_t = jnp.concatenate([o_t[:, (2 * p) * BLOCK:(2 * p + 1) * BLOCK],
                                      o_t[:, (2 * p + 1) * BLOCK:(2 * p + 2) * BLOCK]], axis=0)
            oa_ref[sq, sb * BLOCK:(sb + 1) * BLOCK, (2 * g + p) * LANES:(2 * g + p + 1) * LANES] = (
                pair_t.T.astype(BF16))

    lane_row = lax.broadcasted_iota(jnp.int32, (1, LANES), 1)
    chains = [(sq, h) for sq in seqs for h in range(B_HEADS)]

    def head_cols(h):
        return slice(h * LANES, (h + 1) * LANES)

    q_aug = {}
    for sq, h in chains:
        first = _bias_lane(h, 0)
        bias_ones = jnp.where((lane_row >= first) & (lane_row < first + BIAS_PIECES), 1.0, 0.0).astype(BF16)
        q_pair = qb_ref[sq, :, (h // 2) * LANES:(h // 2 + 1) * LANES]
        q_aug[sq, h] = jnp.where(low if h % 2 == 0 else ~low, q_pair, bias_ones)

    row = lax.broadcasted_iota(jnp.int32, (KV_TILE, Q_TILE), 0)
    col = lax.broadcasted_iota(jnp.int32, (KV_TILE, Q_TILE), 1)
    causal = jnp.where(row <= col, 0.0, NEG)

    scores = {c: _dot_nt(kbm_ref[:, head_cols(c[1])], q_aug[c]) for c in chains}
    probs = {}
    for c in chains:
        m0 = jnp.max(scores[c], axis=0, keepdims=True)
        m_ref[c] = m0
        probs[c] = jnp.exp2(scores[c] - m0).astype(BF16)
    for c in chains:
        acc_ref[c] = _dot(v_meta(c[1] * HEAD_DIM), probs[c])

    def update(blk, n_blk, mask):
        k0 = pl.multiple_of(blk * BLOCK, KV_TILE)
        rows = pl.ds(k0, n_blk * BLOCK)
        scores = {c: _dot_nt(kb_ref[c[0], rows, head_cols(c[1])], q_aug[c]) for c in chains}
        probs, alphas = {}, {}
        for c in chains:
            s = scores[c]
            if mask is not None:
                s = s + mask
            m_old = m_ref[c]
            m_new = jnp.maximum(m_old, jnp.max(s, axis=0, keepdims=True))
            m_ref[c] = m_new
            alphas[c] = jnp.exp2(m_old - m_new)
            probs[c] = jnp.exp2(s - m_new).astype(BF16)
        for c in chains:
            acc_ref[c] = alphas[c] * acc_ref[c] + _dot(v_tile(c[0], blk, n_blk, c[1] * HEAD_DIM), probs[c])

    per_tile = KV_TILE // BLOCK

    def body(j, carry):
        update(TILES_PER_TRIP * per_tile * j, TILES_PER_TRIP * per_tile, None)
        return carry

    lax.fori_loop(0, lax.div(i, TILES_PER_TRIP), body, 0)

    def finalize():
        for sq in seqs:
            for pr in range(B_HEADS // 2):
                acc0, acc1 = acc_ref[sq, 2 * pr], acc_ref[sq, 2 * pr + 1]
                o_t = jnp.concatenate([acc0[:HEAD_DIM] / acc0[HEAD_DIM:HEAD_DIM + 1],
                                       acc1[:HEAD_DIM] / acc1[HEAD_DIM:HEAD_DIM + 1]], axis=0)
                ob_ref[sq, :, pr * LANES:(pr + 1) * LANES] = o_t.T.astype(BF16)

    for left in range(TILES_PER_TRIP):
        @pl.when(lax.rem(i, TILES_PER_TRIP) == left)
        def _():
            mask = jnp.concatenate([jnp.zeros((left * KV_TILE, Q_TILE), F32), causal], axis=0) if left else causal
            update(per_tile * (i - left), (left + 1) * per_tile, mask)
            finalize()


def _attention(sinks, qa, ka, qb, kb, vt, kam, kbm, vtm, tables, batch, seq):
    nq = seq // Q_TILE
    group = batch // SEQS_PER_STEP
    grid = (group, nq)

    def grouped(a):
        return a.reshape((SEQS_PER_STEP, a.shape[0] // SEQS_PER_STEP) + a.shape[1:])

    def q_spec(cols):
        return pl.BlockSpec((SEQS_PER_STEP, Q_TILE, cols), lambda b, i: (0, b * nq + i, 0))

    def kv_spec(cols):
        return pl.BlockSpec((SEQS_PER_STEP, seq, cols), lambda b, i: (0, b, 0))

    in_specs = [
        pl.BlockSpec(memory_space=pltpu.SMEM),
        q_spec(A_WIDTH), kv_spec(A_KEY_COLS), q_spec(B_WIDTH), kv_spec(B_HEADS * LANES),
        pl.BlockSpec((SEQS_PER_STEP, seq // LANES, V_ROWS, LANES), lambda b, i: (0, b, 0, 0)),
    ]
    in_specs += [_const_spec(a.shape) for a in (kam, kbm, vtm) + tuple(tables)]
    out_shape = [jax.ShapeDtypeStruct((SEQS_PER_STEP, group * seq, A_WIDTH), BF16),
                 jax.ShapeDtypeStruct((SEQS_PER_STEP, group * seq, B_WIDTH), BF16)]
    out_specs = [q_spec(A_WIDTH), q_spec(B_WIDTH)]
    scratch = [
        pltpu.VMEM((SEQS_PER_STEP, B_HEADS, 1, Q_TILE), F32),
        pltpu.VMEM((SEQS_PER_STEP, B_HEADS, HEAD_DIM + ONES_ROWS, Q_TILE), F32),
    ]
    oa, ob = pl.pallas_call(
        _attn_kernel,
        grid=grid,
        in_specs=in_specs,
        out_specs=out_specs,
        out_shape=out_shape,
        scratch_shapes=scratch,
        compiler_params=pltpu.CompilerParams(
            dimension_semantics=("arbitrary", "arbitrary"), vmem_limit_bytes=VMEM_LIMIT_BYTES),
        name="attention",
    )(sinks, grouped(qa), grouped(ka), grouped(qb), grouped(kb), grouped(vt), kam, kbm, vtm, *tables)
    return oa.reshape(batch * seq, A_WIDTH), ob.reshape(batch * seq, B_WIDTH)


def _mix_ffn_kernel(h1_ref, oa_ref, ob_ref, ga_ref, gb_ref, wa_ref, wb_ref, wo_ref,
                    n2_ref, w2i_ref, w2o_ref, nf_ref, out_ref):
    rows = h1_ref.shape[0] // ROW_BLOCKS
    blocks = [slice(r * rows, (r + 1) * rows) for r in range(ROW_BLOCKS)]
    yas = [_dot(oa_ref[b, :], wa_ref[...]) for b in blocks]
    ybs = [_dot(ob_ref[b, :], wb_ref[...]) for b in blocks]
    mixed = [(ga_ref[b, :].astype(F32) * ya + gb_ref[b, :].astype(F32) * yb).astype(BF16)
             for b, ya, yb in zip(blocks, yas, ybs)]
    h2s = [h1_ref[b, :] + _dot(mx, wo_ref[...]) for b, mx in zip(blocks, mixed)]
    xns = [_rms(h2, n2_ref[...]).astype(BF16) for h2 in h2s]
    ffns = _swiglu(xns, w2i_ref, w2o_ref)
    for b, h2, ffn in zip(blocks, h2s, ffns):
        out_ref[b, :] = _rms(h2 + 0.5 * ffn, nf_ref[...])


def _mix_ffn(h1, oa, ob, ga, gb, wa, wb, wo, n2, w2i, w2o, nf, tile):
    rows = h1.shape[0]

    def row_spec(cols):
        return pl.BlockSpec((tile, cols), lambda i: (i, 0))

    in_specs = [row_spec(D_MODEL), row_spec(A_WIDTH), row_spec(B_WIDTH), row_spec(D_MODEL), row_spec(D_MODEL)]
    in_specs += [_const_spec(w.shape) for w in (wa, wb, wo, n2, w2i, w2o, nf)]
    return pl.pallas_call(
        _mix_ffn_kernel,
        grid=(rows // tile,),
        in_specs=in_specs,
        out_specs=row_spec(D_MODEL),
        out_shape=jax.ShapeDtypeStruct((rows, D_MODEL), F32),
        compiler_params=pltpu.CompilerParams(
            dimension_semantics=("arbitrary",), vmem_limit_bytes=VMEM_LIMIT_BYTES),
        name="mix_ffn2",
    )(h1, oa, ob, ga, gb, wa, wb, wo, n2, w2i, w2o, nf)


def _prep_proj_weight(w):
    a_kv_width = A_KV_HEADS * HEAD_DIM
    widths = (A_WIDTH, a_kv_width, a_kv_width, B_WIDTH, B_WIDTH, B_WIDTH, B_HEADS, D_MODEL, D_MODEL)
    offsets = np.cumsum((0,) + widths)
    qa, ka, va, qb, kb, vb, wf, g_a, g_b = (w[:, lo:hi] for lo, hi in zip(offsets[:-1], offsets[1:]))
    k0, k1 = ka[:, :HEAD_DIM], ka[:, HEAD_DIM:]
    ka_dup = jnp.concatenate([k0, k0, k1, k1], axis=1)
    wf_pad = jnp.pad(wf, ((0, 0), (0, LANES - B_HEADS)))
    wp = jnp.concatenate([qa, ka_dup, qb, kb, g_a, g_b, wf_pad], axis=1).astype(BF16)
    wv = jnp.concatenate([vb, va], axis=1).astype(BF16)
    return wp, wv


def _attention_tables():
    slopes = 2.0 ** -(np.arange(1, A_HEADS + 1, dtype=np.float64))
    k = np.arange(BLOCK)[:, None]
    r = np.arange(BLOCK)[None, :]
    band = np.empty((2, A_KV_HEADS, BLOCK, A_GROUP * BLOCK), np.float32)
    meta = np.empty((A_KV_HEADS, N_META, A_GROUP * BLOCK), np.float32)
    slope_rows = np.empty((A_KV_HEADS, 1, A_GROUP * BLOCK), np.float32)
    mk = np.arange(N_META)[:, None]
    for h in range(A_HEADS):
        g, j = divmod(h, A_GROUP)
        cols = slice(j * BLOCK, (j + 1) * BLOCK)
        dist = np.where(k <= r, r - k, r - k + BLOCK)
        band[1, g, :, cols] = -slopes[h] * dist
        band[0, g, :, cols] = np.where(k <= r, -slopes[h] * dist, NEG)
        meta[g, :, cols] = -slopes[h] * (r + N_META - mk)
        slope_rows[g, :, cols] = slopes[h] * BLOCK
    band = np.where(band > NEG, band * LOG2E, NEG)
    tables = [t.astype(np.float32) for t in (band, meta * LOG2E, slope_rows * LOG2E)]
    return tuple(jnp.asarray(t) for t in tables)


def _bias_lane(h, p):
    return (HEAD_DIM if h % 2 == 0 else 0) + BIAS_PIECES * h + p


def _bias_placement():
    place = np.zeros((BIAS_PIECES, LANES, LANES), np.float32)
    for h in range(B_HEADS):
        for p in range(BIAS_PIECES):
            place[p, h, _bias_lane(h, p)] = 1.0
    return jnp.asarray(place, BF16)


def kernel(x, meta_tokens, ffn1_norm, ffn1_w_in, ffn1_w_out, mix_norm, w_in, b_forget, attn_sinks,
           w_branch_a, w_branch_b, w_out, ffn2_norm, ffn2_w_in, ffn2_w_out, final_norm):
    batch, seq, d = x.shape
    assert d == D_MODEL and seq % TOKEN_TILE == 0 and seq % Q_TILE == 0
    assert batch % SEQS_PER_STEP == 0 and KV_TILE == Q_TILE
    assert ffn1_norm.shape[0] == 1, "single layer"

    n1 = ffn1_norm[0][None].astype(F32)
    nm = mix_norm[0][None].astype(F32)
    n2 = ffn2_norm[0][None].astype(F32)
    nf = final_norm[None].astype(F32)
    w1i, w1o = ffn1_w_in[0].astype(BF16), ffn1_w_out[0].astype(BF16)
    w2i, w2o = ffn2_w_in[0].astype(BF16), ffn2_w_out[0].astype(BF16)
    wp, wv = _prep_proj_weight(w_in[0])
    bfg = jnp.pad(b_forget[0].astype(F32), (0, LANES - B_HEADS))[None]
    wa, wb, wo = w_branch_a[0].astype(BF16), w_branch_b[0].astype(BF16), w_out[0].astype(BF16)
    place = _bias_placement()

    x2d = x.reshape(batch * seq, D_MODEL)
    h1, qa, ka, qb, kb, ga, gb, vt = _ffn_proj(
        x2d, n1, w1i, w1o, nm, wp, wv.T, bfg, place, TOKEN_TILE, seq // TOKEN_TILE, False)
    _, _, kam, _, kbm, _, _, vm = _ffn_proj(
        meta_tokens.astype(F32), n1, w1i, w1o, nm, wp, wv, bfg, place, N_META, 1, True)

    oa, ob = _attention(attn_sinks[0].astype(F32), qa, ka, qb, kb, vt, kam, kbm, vm.T,
                        _attention_tables(), batch, seq)
    out = _mix_ffn(h1, oa, ob, ga, gb, wa, wb, wo, n2, w2i, w2o, nf, TOKEN_TILE)
    return out.reshape(batch, seq, D_MODEL)
```

```python
import jax
import jax.numpy as jnp
import numpy as np
from jax import lax
from jax.experimental import pallas as pl
from jax.experimental.pallas import tpu as pltpu

F32 = jnp.float32
BF16 = jnp.bfloat16

D_MODEL = 1024
N_META = 16
BLOCK = 128
HEAD_DIM = 64
A_HEADS = 8
A_KV_HEADS = 2
A_GROUP = A_HEADS // A_KV_HEADS
B_HEADS = 8
A_WIDTH = A_HEADS * HEAD_DIM
B_WIDTH = B_HEADS * HEAD_DIM
V_ROWS = B_WIDTH + A_KV_HEADS * HEAD_DIM
D_FF = 2816
EPS = 1e-6
NEG = -1e30
LOG2E = 1.4426950408889634
Q_SCALE = HEAD_DIM ** -0.5 * LOG2E

LANES = 128
TOKEN_TILE = 512
ROW_BLOCKS = 2
Q_TILE = 256
KV_TILE = 256
TILES_PER_TRIP = 2
SEQS_PER_STEP = 2
FF_CHUNKS = ((0, 1024), (1024, 1024), (2048, 768))
VMEM_LIMIT_BYTES = 56 * 1024 * 1024
BIAS_PIECES = 3
ONES_ROWS = 16

A_KEY_COLS = 2 * A_KV_HEADS * HEAD_DIM
_PROJ_WIDTHS = (A_WIDTH, A_KEY_COLS, B_WIDTH, B_WIDTH, D_MODEL, D_MODEL, LANES)
_QA, _KA, _QB, _KB, _GA, _GB, _FG = ((sum(_PROJ_WIDTHS[:j]), w) for j, w in enumerate(_PROJ_WIDTHS))


def _dot(a, b):
    return jnp.dot(a, b, preferred_element_type=F32)


def _dot_nt(a, b):
    return lax.dot_general(a, b, (((1,), (1,)), ((), ())), preferred_element_type=F32)


def _rms(x, g):
    ms = jnp.mean(x * x, axis=-1, keepdims=True)
    return x * lax.rsqrt(ms + EPS) * g


def _swiglu(xns, w_in_ref, w_out_ref):
    accs = [None] * len(xns)
    for lo, width in FF_CHUNKS:
        gs = [_dot(xn, w_in_ref[:, lo:lo + width]) for xn in xns]
        us = [_dot(xn, w_in_ref[:, D_FF + lo:D_FF + lo + width]) for xn in xns]
        acts = [(g * jax.nn.sigmoid(g) * u).astype(BF16) for g, u in zip(gs, us)]
        parts = [_dot(a, w_out_ref[lo:lo + width, :]) for a in acts]
        accs = [part if acc is None else acc + part for acc, part in zip(accs, parts)]
    return accs


def _split3(x):
    hi = x.astype(BF16)
    r = x - hi.astype(F32)
    mid = r.astype(BF16)
    lo = (r - mid.astype(F32)).astype(BF16)
    return hi, mid, lo


def _ffn_proj_kernel(tiles_per_seq, is_meta,
                     x_ref, n1_ref, w1i_ref, w1o_ref, nm_ref, wp_ref, wv_ref, bf_ref, tri_ref, place_ref,
                     h1_ref, qa_ref, ka_ref, qb_ref, kb_ref, ga_ref, gb_ref, v_ref, carry_ref):
    @pl.when(pl.program_id(0) % tiles_per_seq == 0)
    def _():
        carry_ref[...] = jnp.zeros_like(carry_ref)

    x = x_ref[...]
    tile = x.shape[0]
    xn = _rms(x, n1_ref[...]).astype(BF16)
    h1 = x + 0.5 * _swiglu([xn], w1i_ref, w1o_ref)[0]
    h1_ref[...] = h1
    u = _rms(h1, nm_ref[...]).astype(BF16)

    def proj(lo, width):
        return _dot(u, wp_ref[:, lo:lo + width])

    qa_ref[...] = (proj(*_QA) * Q_SCALE).astype(BF16)
    ka_ref[...] = proj(*_KA).astype(BF16)
    qb_ref[...] = (proj(*_QB) * Q_SCALE).astype(BF16)
    kb = proj(*_KB).astype(BF16)

    z = proj(*_FG) + bf_ref[...]
    lane = lax.broadcasted_iota(jnp.int32, (1, LANES), 1)
    lf = jnp.where(lane < B_HEADS, jnp.minimum(z, 0.0) - jnp.log1p(jnp.exp(-jnp.abs(z))), 0.0)
    tri = tri_ref[...]
    blk = tri.shape[0]
    pieces = _split3(lf)
    offset, parts = carry_ref[...], []
    for r in range(tile // blk):
        local = sum(_dot(tri, piece[r * blk:(r + 1) * blk]) for piece in pieces)
        parts.append(local + offset)
        offset = offset + local[blk - 1:blk, :]
    c = jnp.concatenate(parts, axis=0)
    carry_ref[...] = offset
    bias = ((c[tile - 1:tile, :] - c) if is_meta else -c) * LOG2E
    cb = sum(_dot(piece, place_ref[p]) for p, piece in enumerate(_split3(bias))).astype(BF16)
    low = lane < HEAD_DIM
    for h in range(B_HEADS):
        k_pair = kb[:, (h // 2) * LANES:(h // 2 + 1) * LANES]
        kb_ref[:, h * LANES:(h + 1) * LANES] = jnp.where(low if h % 2 == 0 else ~low, k_pair, cb)

    ga_ref[...] = jax.nn.sigmoid(proj(*_GA)).astype(BF16)
    gb_ref[...] = jax.nn.sigmoid(proj(*_GB)).astype(BF16)

    if is_meta:
        v_ref[...] = _dot(u, wv_ref[...]).astype(BF16)
    else:
        vt = _dot_nt(wv_ref[...], u).astype(BF16)
        for c_blk in range(tile // LANES):
            v_ref[c_blk] = vt[:, c_blk * LANES:(c_blk + 1) * LANES]


def _const_spec(shape):
    zeros = (0,) * len(shape)
    return pl.BlockSpec(shape, lambda *_: zeros, pipeline_mode=pl.Buffered(1))


def _ffn_proj(x2d, n1, w1i, w1o, nm, wp, wv, bfg, place, tile, tiles_per_seq, is_meta):
    rows = x2d.shape[0]
    grid = (rows // tile,)
    scan_rows = min(tile, BLOCK)
    tri = jnp.asarray(np.tril(np.ones((scan_rows, scan_rows), np.float32)), BF16)

    def row_spec(cols):
        return pl.BlockSpec((tile, cols), lambda i: (i, 0))

    out_cols = (D_MODEL, A_WIDTH, A_KEY_COLS, B_WIDTH, B_HEADS * LANES, D_MODEL, D_MODEL)
    out_dtypes = (F32,) + (BF16,) * 6
    out_shape = [jax.ShapeDtypeStruct((rows, c), dt) for c, dt in zip(out_cols, out_dtypes)]
    out_specs = [row_spec(c) for c in out_cols]
    if is_meta:
        out_shape.append(jax.ShapeDtypeStruct((rows, V_ROWS), BF16))
        out_specs.append(row_spec(V_ROWS))
    else:
        out_shape.append(jax.ShapeDtypeStruct((rows // LANES, V_ROWS, LANES), BF16))
        out_specs.append(pl.BlockSpec((tile // LANES, V_ROWS, LANES), lambda i: (i, 0, 0)))
    consts = (n1, w1i, w1o, nm, wp, wv, bfg, tri, place)
    in_specs = [row_spec(D_MODEL)] + [_const_spec(a.shape) for a in consts]
    return pl.pallas_call(
        lambda *refs: _ffn_proj_kernel(tiles_per_seq, is_meta, *refs),
        grid=grid,
        in_specs=in_specs,
        out_specs=out_specs,
        out_shape=out_shape,
        scratch_shapes=[pltpu.VMEM((1, LANES), F32)],
        compiler_params=pltpu.CompilerParams(
            dimension_semantics=("arbitrary",), vmem_limit_bytes=VMEM_LIMIT_BYTES),
        name="ffn1_proj_meta" if is_meta else "ffn1_proj",
    )(x2d, *consts)


def _split_heads(q, low):
    zero = jnp.zeros_like(q)
    return jnp.concatenate([jnp.where(low, q, zero), jnp.where(low, zero, q)], axis=0)


def _attn_kernel(sinks_ref, qa_ref, ka_ref, qb_ref, kb_ref, vt_ref,
                 kam_ref, kbm_ref, vtm_ref, band_ref, mtab_ref, slope_ref,
                 oa_ref, ob_ref, m_ref, acc_ref):
    i = pl.program_id(1)
    low = lax.broadcasted_iota(jnp.int32, (1, LANES), 1) < HEAD_DIM
    sub = Q_TILE // BLOCK
    seqs = range(SEQS_PER_STEP)
    ones_meta = jnp.ones((ONES_ROWS, N_META), BF16)

    def v_tile(sq, blk, n_blk, row0):
        vals = jnp.concatenate([vt_ref[sq, blk + b, row0:row0 + HEAD_DIM, :] for b in range(n_blk)], axis=1)
        return jnp.concatenate([vals, jnp.ones((ONES_ROWS, n_blk * BLOCK), BF16)], axis=0)

    def v_meta(row0):
        return jnp.concatenate([vtm_ref[row0:row0 + HEAD_DIM, :], ones_meta], axis=0)

    own_key = (lax.broadcasted_iota(jnp.int32, (BLOCK, A_GROUP * BLOCK), 0)
               <= (lax.broadcasted_iota(jnp.int32, (BLOCK, A_GROUP * BLOCK), 1) & (BLOCK - 1)))
    units = [(sq, sb, g) for sq in seqs for sb in range(sub) for g in range(A_KV_HEADS)]
    blks, s_bs, s_ms = {}, {}, {}
    for sq, sb, g in units:
        n = i * sub + sb
        blks[sb] = (jnp.maximum(n - 1, 0), n)
        q = qa_ref[sq, sb * BLOCK:(sb + 1) * BLOCK, :]
        qq = jnp.concatenate(
            [_split_heads(q[:, (2 * g + p) * LANES:(2 * g + p + 1) * LANES], low) for p in range(2)], axis=0)
        kband = jnp.concatenate(
            [ka_ref[sq, pl.ds(pl.multiple_of(b * BLOCK, BLOCK), BLOCK), g * LANES:(g + 1) * LANES]
             for b in blks[sb]], axis=0)
        kmeta = kam_ref[:, g * LANES:(g + 1) * LANES]
        s2 = _dot_nt(kband, qq)
        s_bs[sq, sb, g] = jnp.where(own_key, s2[BLOCK:], s2[:BLOCK]) + band_ref[jnp.minimum(n, 1), g]
        s_ms[sq, sb, g] = _dot_nt(kmeta, qq) + mtab_ref[g] - n.astype(F32) * slope_ref[g]
    p_bs, p_ms, sink_ps = {}, {}, {}
    for unit in units:
        g = unit[2]
        s_b, s_m = s_bs[unit], s_ms[unit]
        sink = jnp.concatenate(
            [jnp.full((1, BLOCK), sinks_ref[g * A_GROUP + j] * LOG2E, F32) for j in range(A_GROUP)], axis=1)
        m = jnp.maximum(jnp.max(s_b, axis=0, keepdims=True), jnp.max(s_m, axis=0, keepdims=True))
        m = jnp.maximum(m, sink)
        p_bs[unit] = jnp.exp2(s_b - m).astype(BF16)
        p_ms[unit] = jnp.exp2(s_m - m).astype(BF16)
        sink_ps[unit] = jnp.exp2(sink - m)
    for unit in units:
        sq, sb, g = unit
        row0 = B_WIDTH + g * HEAD_DIM
        prev_blk, own_blk = blks[sb]
        p_b = p_bs[unit]
        zero = jnp.zeros_like(p_b)
        o_aug = (_dot(v_tile(sq, own_blk, 1, row0), jnp.where(own_key, p_b, zero))
                 + _dot(v_tile(sq, prev_blk, 1, row0), jnp.where(own_key, zero, p_b))
                 + _dot(v_meta(row0), p_ms[unit]))
        o_t = o_aug[:HEAD_DIM] / (o_aug[HEAD_DIM:HEAD_DIM + 1] + sink_ps[unit])
        for p in range(2):
            pair_t = jnp.concatenate([o_t[:, (2 * p) * BLOCK:(2 * p + 1) * BLOCK],
                                      o_t[:, (2 * p + 1) * BLOCK:(2 * p + 2) * BLOCK]], axis=0)
            oa_ref[sq, sb * BLOCK:(sb + 1) * BLOCK, (2 * g + p) * LANES:(2 * g + p + 1) * LANES] = (
                pair_t.T.astype(BF16))

    lane_row = lax.broadcasted_iota(jnp.int32, (1, LANES), 1)
    chains = [(sq, h) for sq in seqs for h in range(B_HEADS)]

    def head_cols(h):
        return slice(h * LANES, (h + 1) * LANES)

    q_aug = {}
    for sq, h in chains:
        first = _bias_lane(h, 0)
        bias_ones = jnp.where((lane_row >= first) & (lane_row < first + BIAS_PIECES), 1.0, 0.0).astype(BF16)
        q_pair = qb_ref[sq, :, (h // 2) * LANES:(h // 2 + 1) * LANES]
        q_aug[sq, h] = jnp.where(low if h % 2 == 0 else ~low, q_pair, bias_ones)

    row = lax.broadcasted_iota(jnp.int32, (KV_TILE, Q_TILE), 0)
    col = lax.broadcasted_iota(jnp.int32, (KV_TILE, Q_TILE), 1)
    causal = jnp.where(row <= col, 0.0, NEG)

    scores = {c: _dot_nt(kbm_ref[:, head_cols(c[1])], q_aug[c]) for c in chains}
    probs = {}
    for c in chains:
        m0 = jnp.max(scores[c], axis=0, keepdims=True)
        m_ref[c] = m0
        probs[c] = jnp.exp2(scores[c] - m0).astype(BF16)
    for c in chains:
        acc_ref[c] = _dot(v_meta(c[1] * HEAD_DIM), probs[c])

    def update(blk, n_blk, mask):
        k0 = pl.multiple_of(blk * BLOCK, KV_TILE)
        rows = pl.ds(k0, n_blk * BLOCK)
        scores = {c: _dot_nt(kb_ref[c[0], rows, head_cols(c[1])], q_aug[c]) for c in chains}
        probs, alphas = {}, {}
        for c in chains:
            s = scores[c]
            if mask is not None:
                s = s + mask
            m_old = m_ref[c]
            m_new = jnp.maximum(m_old, jnp.max(s, axis=0, keepdims=True))
            m_ref[c] = m_new
            alphas[c] = jnp.exp2(m_old - m_new)
            probs[c] = jnp.exp2(s - m_new).astype(BF16)
        for c in chains:
            acc_ref[c] = alphas[c] * acc_ref[c] + _dot(v_tile(c[0], blk, n_blk, c[1] * HEAD_DIM), probs[c])

    per_tile = KV_TILE // BLOCK

    def body(j, carry):
        update(TILES_PER_TRIP * per_tile * j, TILES_PER_TRIP * per_tile, None)
        return carry

    lax.fori_loop(0, lax.div(i, TILES_PER_TRIP), body, 0)

    def finalize():
        for sq in seqs:
            for pr in range(B_HEADS // 2):
                acc0, acc1 = acc_ref[sq, 2 * pr], acc_ref[sq, 2 * pr + 1]
                o_t = jnp.concatenate([acc0[:HEAD_DIM] / acc0[HEAD_DIM:HEAD_DIM + 1],
                                       acc1[:HEAD_DIM] / acc1[HEAD_DIM:HEAD_DIM + 1]], axis=0)
                ob_ref[sq, :, pr * LANES:(pr + 1) * LANES] = o_t.T.astype(BF16)

    for left in range(TILES_PER_TRIP):
        @pl.when(lax.rem(i, TILES_PER_TRIP) == left)
        def _():
            mask = jnp.concatenate([jnp.zeros((left * KV_TILE, Q_TILE), F32), causal], axis=0) if left else causal
            update(per_tile * (i - left), (left + 1) * per_tile, mask)
            finalize()


def _attention(sinks, qa, ka, qb, kb, vt, kam, kbm, vtm, tables, batch, seq):
    nq = seq // Q_TILE
    group = batch // SEQS_PER_STEP
    grid = (group, nq)

    def grouped(a):
        return a.reshape((SEQS_PER_STEP, a.shape[0] // SEQS_PER_STEP) + a.shape[1:])

    def q_spec(cols):
        return pl.BlockSpec((SEQS_PER_STEP, Q_TILE, cols), lambda b, i: (0, b * nq + i, 0))

    def kv_spec(cols):
        return pl.BlockSpec((SEQS_PER_STEP, seq, cols), lambda b, i: (0, b, 0))

    in_specs = [
        pl.BlockSpec(memory_space=pltpu.SMEM),
        q_spec(A_WIDTH), kv_spec(A_KEY_COLS), q_spec(B_WIDTH), kv_spec(B_HEADS * LANES),
        pl.BlockSpec((SEQS_PER_STEP, seq // LANES, V_ROWS, LANES), lambda b, i: (0, b, 0, 0)),
    ]
    in_specs += [_const_spec(a.shape) for a in (kam, kbm, vtm) + tuple(tables)]
    out_shape = [jax.ShapeDtypeStruct((SEQS_PER_STEP, group * seq, A_WIDTH), BF16),
                 jax.ShapeDtypeStruct((SEQS_PER_STEP, group * seq, B_WIDTH), BF16)]
    out_specs = [q_spec(A_WIDTH), q_spec(B_WIDTH)]
    scratch = [
        pltpu.VMEM((SEQS_PER_STEP, B_HEADS, 1, Q_TILE), F32),
        pltpu.VMEM((SEQS_PER_STEP, B_HEADS, HEAD_DIM + ONES_ROWS, Q_TILE), F32),
    ]
    oa, ob = pl.pallas_call(
        _attn_kernel,
        grid=grid,
        in_specs=in_specs,
        out_specs=out_specs,
        out_shape=out_shape,
        scratch_shapes=scratch,
        compiler_params=pltpu.CompilerParams(
            dimension_semantics=("arbitrary", "arbitrary"), vmem_limit_bytes=VMEM_LIMIT_BYTES),
        name="attention",
    )(sinks, grouped(qa), grouped(ka), grouped(qb), grouped(kb), grouped(vt), kam, kbm, vtm, *tables)
    return oa.reshape(batch * seq, A_WIDTH), ob.reshape(batch * seq, B_WIDTH)


def _mix_ffn_kernel(h1_ref, oa_ref, ob_ref, ga_ref, gb_ref, wa_ref, wb_ref, wo_ref,
                    n2_ref, w2i_ref, w2o_ref, nf_ref, out_ref):
    rows = h1_ref.shape[0] // ROW_BLOCKS
    blocks = [slice(r * rows, (r + 1) * rows) for r in range(ROW_BLOCKS)]
    yas = [_dot(oa_ref[b, :], wa_ref[...]) for b in blocks]
    ybs = [_dot(ob_ref[b, :], wb_ref[...]) for b in blocks]
    mixed = [(ga_ref[b, :].astype(F32) * ya + gb_ref[b, :].astype(F32) * yb).astype(BF16)
             for b, ya, yb in zip(blocks, yas, ybs)]
    h2s = [h1_ref[b, :] + _dot(mx, wo_ref[...]) for b, mx in zip(blocks, mixed)]
    xns = [_rms(h2, n2_ref[...]).astype(BF16) for h2 in h2s]
    ffns = _swiglu(xns, w2i_ref, w2o_ref)
    for b, h2, ffn in zip(blocks, h2s, ffns):
        out_ref[b, :] = _rms(h2 + 0.5 * ffn, nf_ref[...])


def _mix_ffn(h1, oa, ob, ga, gb, wa, wb, wo, n2, w2i, w2o, nf, tile):
    rows = h1.shape[0]

    def row_spec(cols):
        return pl.BlockSpec((tile, cols), lambda i: (i, 0))

    in_specs = [row_spec(D_MODEL), row_spec(A_WIDTH), row_spec(B_WIDTH), row_spec(D_MODEL), row_spec(D_MODEL)]
    in_specs += [_const_spec(w.shape) for w in (wa, wb, wo, n2, w2i, w2o, nf)]
    return pl.pallas_call(
        _mix_ffn_kernel,
        grid=(rows // tile,),
        in_specs=in_specs,
        out_specs=row_spec(D_MODEL),
        out_shape=jax.ShapeDtypeStruct((rows, D_MODEL), F32),
        compiler_params=pltpu.CompilerParams(
            dimension_semantics=("arbitrary",), vmem_limit_bytes=VMEM_LIMIT_BYTES),
        name="mix_ffn2",
    )(h1, oa, ob, ga, gb, wa, wb, wo, n2, w2i, w2o, nf)


def _prep_proj_weight(w):
    a_kv_width = A_KV_HEADS * HEAD_DIM
    widths = (A_WIDTH, a_kv_width, a_kv_width, B_WIDTH, B_WIDTH, B_WIDTH, B_HEADS, D_MODEL, D_MODEL)
    offsets = np.cumsum((0,) + widths)
    qa, ka, va, qb, kb, vb, wf, g_a, g_b = (w[:, lo:hi] for lo, hi in zip(offsets[:-1], offsets[1:]))
    k0, k1 = ka[:, :HEAD_DIM], ka[:, HEAD_DIM:]
    ka_dup = jnp.concatenate([k0, k0, k1, k1], axis=1)
    wf_pad = jnp.pad(wf, ((0, 0), (0, LANES - B_HEADS)))
    wp = jnp.concatenate([qa, ka_dup, qb, kb, g_a, g_b, wf_pad], axis=1).astype(BF16)
    wv = jnp.concatenate([vb, va], axis=1).astype(BF16)
    return wp, wv


def _attention_tables():
    slopes = 2.0 ** -(np.arange(1, A_HEADS + 1, dtype=np.float64))
    k = np.arange(BLOCK)[:, None]
    r = np.arange(BLOCK)[None, :]
    band = np.empty((2, A_KV_HEADS, BLOCK, A_GROUP * BLOCK), np.float32)
    meta = np.empty((A_KV_HEADS, N_META, A_GROUP * BLOCK), np.float32)
    slope_rows = np.empty((A_KV_HEADS, 1, A_GROUP * BLOCK), np.float32)
    mk = np.arange(N_META)[:, None]
    for h in range(A_HEADS):
        g, j = divmod(h, A_GROUP)
        cols = slice(j * BLOCK, (j + 1) * BLOCK)
        dist = np.where(k <= r, r - k, r - k + BLOCK)
        band[1, g, :, cols] = -slopes[h] * dist
        band[0, g, :, cols] = np.where(k <= r, -slopes[h] * dist, NEG)
        meta[g, :, cols] = -slopes[h] * (r + N_META - mk)
        slope_rows[g, :, cols] = slopes[h] * BLOCK
    band = np.where(band > NEG, band * LOG2E, NEG)
    tables = [t.astype(np.float32) for t in (band, meta * LOG2E, slope_rows * LOG2E)]
    return tuple(jnp.asarray(t) for t in tables)


def _bias_lane(h, p):
    return (HEAD_DIM if h % 2 == 0 else 0) + BIAS_PIECES * h + p


def _bias_placement():
    place = np.zeros((BIAS_PIECES, LANES, LANES), np.float32)
    for h in range(B_HEADS):
        for p in range(BIAS_PIECES):
            place[p, h, _bias_lane(h, p)] = 1.0
    return jnp.asarray(place, BF16)


def kernel(x, meta_tokens, ffn1_norm, ffn1_w_in, ffn1_w_out, mix_norm, w_in, b_forget, attn_sinks,
           w_branch_a, w_branch_b, w_out, ffn2_norm, ffn2_w_in, ffn2_w_out, final_norm):
    batch, seq, d = x.shape
    assert d == D_MODEL and seq % TOKEN_TILE == 0 and seq % Q_TILE == 0
    assert batch % SEQS_PER_STEP == 0 and KV_TILE == Q_TILE
    assert ffn1_norm.shape[0] == 1, "single layer"

    n1 = ffn1_norm[0][None].astype(F32)
    nm = mix_norm[0][None].astype(F32)
    n2 = ffn2_norm[0][None].astype(F32)
    nf = final_norm[None].astype(F32)
    w1i, w1o = ffn1_w_in[0].astype(BF16), ffn1_w_out[0].astype(BF16)
    w2i, w2o = ffn2_w_in[0].astype(BF16), ffn2_w_out[0].astype(BF16)
    wp, wv = _prep_proj_weight(w_in[0])
    bfg = jnp.pad(b_forget[0].astype(F32), (0, LANES - B_HEADS))[None]
    wa, wb, wo = w_branch_a[0].astype(BF16), w_branch_b[0].astype(BF16), w_out[0].astype(BF16)
    place = _bias_placement()

    x2d = x.reshape(batch * seq, D_MODEL)
    h1, qa, ka, qb, kb, ga, gb, vt = _ffn_proj(
        x2d, n1, w1i, w1o, nm, wp, wv.T, bfg, place, TOKEN_TILE, seq // TOKEN_TILE, False)
    _, _, kam, _, kbm, _, _, vm = _ffn_proj(
        meta_tokens.astype(F32), n1, w1i, w1o, nm, wp, wv, bfg, place, N_META, 1, True)

    oa, ob = _attention(attn_sinks[0].astype(F32), qa, ka, qb, kb, vt, kam, kbm, vm.T,
                        _attention_tables(), batch, seq)
    out = _mix_ffn(h1, oa, ob, ga, gb, wa, wb, wo, n2, w2i, w2o, nf, TOKEN_TILE)
    return out.reshape(batch, seq, D_MODEL)
```

```python
import jax
import jax.numpy as jnp
import numpy as np
from jax import lax
from jax.experimental import pallas as pl
from jax.experimental.pallas import tpu as pltpu

F32 = jnp.float32
BF16 = jnp.bfloat16

D_MODEL = 1024
N_META = 16
BLOCK = 128
HEAD_DIM = 64
A_HEADS = 8
A_KV_HEADS = 2
A_GROUP = A_HEADS // A_KV_HEADS
B_HEADS = 8
A_WIDTH = A_HEADS * HEAD_DIM
B_WIDTH = B_HEADS * HEAD_DIM
V_ROWS = B_WIDTH + A_KV_HEADS * HEAD_DIM
D_FF = 2816
EPS = 1e-6
NEG = -1e30
LOG2E = 1.4426950408889634
Q_SCALE = HEAD_DIM ** -0.5 * LOG2E

LANES = 128
TOKEN_TILE = 512
ROW_BLOCKS = 2
Q_TILE = 256
KV_TILE = 256
TILES_PER_TRIP = 2
SEQS_PER_STEP = 2
FF_CHUNKS = ((0, 1024), (1024, 1024), (2048, 768))
VMEM_LIMIT_BYTES = 56 * 1024 * 1024
BIAS_PIECES = 3
ONES_ROWS = 16

A_KEY_COLS = 2 * A_KV_HEADS * HEAD_DIM
_PROJ_WIDTHS = (A_WIDTH, 2 * LANES, B_WIDTH, B_WIDTH, D_MODEL, D_MODEL)
_QA, _KF, _QB, _KB, _GA, _GB = ((sum(_PROJ_WIDTHS[:j]), w) for j, w in enumerate(_PROJ_WIDTHS))


def _dot(a, b):
    return jnp.dot(a, b, preferred_element_type=F32)


def _dot_nt(a, b):
    return lax.dot_general(a, b, (((1,), (1,)), ((), ())), preferred_element_type=F32)


def _rms(x, g):
    ms = jnp.mean(x * x, axis=-1, keepdims=True)
    return x * lax.rsqrt(ms + EPS) * g


def _swiglu(xns, w_in_ref, w_out_ref):
    accs = [None] * len(xns)
    for lo, width in FF_CHUNKS:
        gs = [_dot(xn, w_in_ref[:, lo:lo + width]) for xn in xns]
        us = [_dot(xn, w_in_ref[:, D_FF + lo:D_FF + lo + width]) for xn in xns]
        acts = [(g * jax.nn.sigmoid(g) * u).astype(BF16) for g, u in zip(gs, us)]
        parts = [_dot(a, w_out_ref[lo:lo + width, :]) for a in acts]
        accs = [part if acc is None else acc + part for acc, part in zip(accs, parts)]
    return accs


def _split3(x):
    hi = x.astype(BF16)
    r = x - hi.astype(F32)
    mid = r.astype(BF16)
    lo = (r - mid.astype(F32)).astype(BF16)
    return hi, mid, lo


def _ffn_proj_kernel(tiles_per_seq, is_meta,
                     x_ref, n1_ref, w1i_ref, w1o_ref, nm_ref, wp_ref, wv_ref, bf_ref, tri_ref, place_ref,
                     h1_ref, qa_ref, ka_ref, qb_ref, kb_ref, ga_ref, gb_ref, v_ref, carry_ref):
    @pl.when(pl.program_id(0) % tiles_per_seq == 0)
    def _():
        carry_ref[...] = jnp.zeros_like(carry_ref)

    x = x_ref[...]
    tile = x.shape[0]
    xn = _rms(x, n1_ref[...]).astype(BF16)
    h1 = x + 0.5 * _swiglu([xn], w1i_ref, w1o_ref)[0]
    h1_ref[...] = h1
    u = _rms(h1, nm_ref[...]).astype(BF16)

    def proj(lo, width):
        return _dot(u, wp_ref[:, lo:lo + width])

    qa_ref[...] = (proj(*_QA) * Q_SCALE).astype(BF16)
    lane = lax.broadcasted_iota(jnp.int32, (1, LANES), 1)
    low = lane < HEAD_DIM
    kf = proj(*_KF)
    ka = kf[:, :LANES]
    swapped = pltpu.roll(ka, HEAD_DIM, 1)
    ka_ref[...] = jnp.concatenate([jnp.where(low, ka, swapped), jnp.where(low, swapped, ka)], axis=1).astype(BF16)
    qb_ref[...] = (proj(*_QB) * Q_SCALE).astype(BF16)
    kb = proj(*_KB).astype(BF16)

    z = kf[:, LANES:] + bf_ref[...]
    lf = jnp.where(lane < B_HEADS, jnp.minimum(z, 0.0) - jnp.log1p(jnp.exp(-jnp.abs(z))), 0.0)
    tri = tri_ref[...]
    blk = tri.shape[0]
    pieces = _split3(lf)
    offset, parts = carry_ref[...], []
    for r in range(tile // blk):
        local = sum(_dot(tri, piece[r * blk:(r + 1) * blk]) for piece in pieces)
        parts.append(local + offset)
        offset = offset + local[blk - 1:blk, :]
    c = jnp.concatenate(parts, axis=0)
    carry_ref[...] = offset
    bias = ((c[tile - 1:tile, :] - c) if is_meta else -c) * LOG2E
    cb = sum(_dot(piece, place_ref[p]) for p, piece in enumerate(_split3(bias))).astype(BF16)
    for h in range(B_HEADS):
        k_pair = kb[:, (h // 2) * LANES:(h // 2 + 1) * LANES]
        kb_ref[:, h * LANES:(h + 1) * LANES] = jnp.where(low if h % 2 == 0 else ~low, k_pair, cb)

    ga_ref[...] = jax.nn.sigmoid(proj(*_GA)).astype(BF16)
    gb_ref[...] = jax.nn.sigmoid(proj(*_GB)).astype(BF16)

    if is_meta:
        v_ref[...] = _dot(u, wv_ref[...]).astype(BF16)
    else:
        vt = _dot_nt(wv_ref[...], u).astype(BF16)
        for c_blk in range(tile // LANES):
            v_ref[c_blk] = vt[:, c_blk * LANES:(c_blk + 1) * LANES]


def _const_spec(shape):
    zeros = (0,) * len(shape)
    return pl.BlockSpec(shape, lambda *_: zeros, pipeline_mode=pl.Buffered(1))


def _ffn_proj(x2d, n1, w1i, w1o, nm, wp, wv, bfg, place, tile, tiles_per_seq, is_meta):
    rows = x2d.shape[0]
    grid = (rows // tile,)
    scan_rows = min(tile, BLOCK)
    tri = jnp.asarray(np.tril(np.ones((scan_rows, scan_rows), np.float32)), BF16)

    def row_spec(cols):
        return pl.BlockSpec((tile, cols), lambda i: (i, 0))

    out_cols = (D_MODEL, A_WIDTH, A_KEY_COLS, B_WIDTH, B_HEADS * LANES, D_MODEL, D_MODEL)
    out_dtypes = (F32,) + (BF16,) * 6
    out_shape = [jax.ShapeDtypeStruct((rows, c), dt) for c, dt in zip(out_cols, out_dtypes)]
    out_specs = [row_spec(c) for c in out_cols]
    if is_meta:
        out_shape.append(jax.ShapeDtypeStruct((rows, V_ROWS), BF16))
        out_specs.append(row_spec(V_ROWS))
    else:
        out_shape.append(jax.ShapeDtypeStruct((rows // LANES, V_ROWS, LANES), BF16))
        out_specs.append(pl.BlockSpec((tile // LANES, V_ROWS, LANES), lambda i: (i, 0, 0)))
    consts = (n1, w1i, w1o, nm, wp, wv, bfg, tri, place)
    in_specs = [row_spec(D_MODEL)] + [_const_spec(a.shape) for a in consts]
    return pl.pallas_call(
        lambda *refs: _ffn_proj_kernel(tiles_per_seq, is_meta, *refs),
        grid=grid,
        in_specs=in_specs,
        out_specs=out_specs,
        out_shape=out_shape,
        scratch_shapes=[pltpu.VMEM((1, LANES), F32)],
        compiler_params=pltpu.CompilerParams(
            dimension_semantics=("arbitrary",), vmem_limit_bytes=VMEM_LIMIT_BYTES),
        name="ffn1_proj_meta" if is_meta else "ffn1_proj",
    )(x2d, *consts)


def _split_heads(q, low):
    zero = jnp.zeros_like(q)
    return jnp.concatenate([jnp.where(low, q, zero), jnp.where(low, zero, q)], axis=0)


def _attn_kernel(sinks_ref, qa_ref, ka_ref, qb_ref, kb_ref, vt_ref,
                 kam_ref, kbm_ref, vtm_ref, band_ref, mtab_ref, slope_ref,
                 oa_ref, ob_ref, m_ref, acc_ref):
    i = pl.program_id(1)
    low = lax.broadcasted_iota(jnp.int32, (1, LANES), 1) < HEAD_DIM
    sub = Q_TILE // BLOCK
    seqs = range(SEQS_PER_STEP)
    ones_meta = jnp.ones((ONES_ROWS, N_META), BF16)

    def v_tile(sq, blk, n_blk, row0):
        vals = jnp.concatenate([vt_ref[sq, blk + b, row0:row0 + HEAD_DIM, :] for b in range(n_blk)], axis=1)
        return jnp.concatenate([vals, jnp.ones((ONES_ROWS, n_blk * BLOCK), BF16)], axis=0)

    def v_meta(row0):
        return jnp.concatenate([vtm_ref[row0:row0 + HEAD_DIM, :], ones_meta], axis=0)

    own_key = (lax.broadcasted_iota(jnp.int32, (BLOCK, A_GROUP * BLOCK), 0)
               <= (lax.broadcasted_iota(jnp.int32, (BLOCK, A_GROUP * BLOCK), 1) & (BLOCK - 1)))
    units = [(sq, sb, g) for sq in seqs for sb in range(sub) for g in range(A_KV_HEADS)]
    blks, s_bs, s_ms = {}, {}, {}
    for sq, sb, g in units:
        n = i * sub + sb
        blks[sb] = (jnp.maximum(n - 1, 0), n)
        q = qa_ref[sq, sb * BLOCK:(sb + 1) * BLOCK, :]
        qq = jnp.concatenate(
            [_split_heads(q[:, (2 * g + p) * LANES:(2 * g + p + 1) * LANES], low) for p in range(2)], axis=0)
        kband = jnp.concatenate(
            [ka_ref[sq, pl.ds(pl.multiple_of(b * BLOCK, BLOCK), BLOCK), g * LANES:(g + 1) * LANES]
             for b in blks[sb]], axis=0)
        kmeta = kam_ref[:, g * LANES:(g + 1) * LANES]
        s2 = _dot_nt(kband, qq)
        s_bs[sq, sb, g] = jnp.where(own_key, s2[BLOCK:], s2[:BLOCK]) + band_ref[jnp.minimum(n, 1), g]
        s_ms[sq, sb, g] = _dot_nt(kmeta, qq) + mtab_ref[g] - n.astype(F32) * slope_ref[g]
    p_bs, p_ms, sink_ps = {}, {}, {}
    for unit in units:
        g = unit[2]
        s_b, s_m = s_bs[unit], s_ms[unit]
        sink = jnp.concatenate(
            [jnp.full((1, BLOCK), sinks_ref[g * A_GROUP + j] * LOG2E, F32) for j in range(A_GROUP)], axis=1)
        m = jnp.maximum(jnp.max(s_b, axis=0, keepdims=True), jnp.max(s_m, axis=0, keepdims=True))
        m = jnp.maximum(m, sink)
        p_bs[unit] = jnp.exp2(s_b - m).astype(BF16)
        p_ms[unit] = jnp.exp2(s_m - m).astype(BF16)
        sink_ps[unit] = jnp.exp2(sink - m)
    for unit in units:
        sq, sb, g = unit
        row0 = B_WIDTH + g * HEAD_DIM
        prev_blk, own_blk = blks[sb]
        p_b = p_bs[unit]
        zero = jnp.zeros_like(p_b)
        o_aug = (_dot(v_tile(sq, own_blk, 1, row0), jnp.where(own_key, p_b, zero))
                 + _dot(v_tile(sq, prev_blk, 1, row0), jnp.where(own_key, zero, p_b))
                 + _dot(v_meta(row0), p_ms[unit]))
        o_t = o_aug[:HEAD_DIM] / (o_aug[HEAD_DIM:HEAD_DIM + 1] + sink_ps[unit])
        for p in range(2):
            pair_t = jnp.concatenate([o_t[:, (2 * p) * BLOCK:(2 * p + 1) * BLOCK],
                                      o_t[:, (2 * p + 1) * BLOCK:(2 * p + 2) * BLOCK]], axis=0)
            oa_ref[sq, sb * BLOCK:(sb + 1) * BLOCK, (2 * g + p) * LANES:(2 * g + p + 1) * LANES] = (
                pair_t.T.astype(BF16))

    lane_row = lax.broadcasted_iota(jnp.int32, (1, LANES), 1)
    chains = [(sq, h) for sq in seqs for h in range(B_HEADS)]

    def head_cols(h):
        return slice(h * LANES, (h + 1) * LANES)

    q_aug = {}
    for sq, h in chains:
        first = _bias_lane(h, 0)
        bias_ones = jnp.where((lane_row >= first) & (lane_row < first + BIAS_PIECES), 1.0, 0.0).astype(BF16)
        q_pair = qb_ref[sq, :, (h // 2) * LANES:(h // 2 + 1) * LANES]
        q_aug[sq, h] = jnp.where(low if h % 2 == 0 else ~low, q_pair, bias_ones)

    row = lax.broadcasted_iota(jnp.int32, (KV_TILE, Q_TILE), 0)
    col = lax.broadcasted_iota(jnp.int32, (KV_TILE, Q_TILE), 1)
    causal = jnp.where(row <= col, 0.0, NEG)

    scores = {c: _dot_nt(kbm_ref[:, head_cols(c[1])], q_aug[c]) for c in chains}
    probs = {}
    for c in chains:
        m0 = jnp.max(scores[c], axis=0, keepdims=True)
        m_ref[c] = m0
        probs[c] = jnp.exp2(scores[c] - m0).astype(BF16)
    for c in chains:
        acc_ref[c] = _dot(v_meta(c[1] * HEAD_DIM), probs[c])

    def update(blk, n_blk, mask):
        k0 = pl.multiple_of(blk * BLOCK, KV_TILE)
        rows = pl.ds(k0, n_blk * BLOCK)
        scores = {c: _dot_nt(kb_ref[c[0], rows, head_cols(c[1])], q_aug[c]) for c in chains}
        probs, alphas = {}, {}
        for c in chains:
            s = scores[c]
            if mask is not None:
                s = s + mask
            m_old = m_ref[c]
            m_new = jnp.maximum(m_old, jnp.max(s, axis=0, keepdims=True))
            m_ref[c] = m_new
            alphas[c] = jnp.exp2(m_old - m_new)
            probs[c] = jnp.exp2(s - m_new).astype(BF16)
        for c in chains:
            acc_ref[c] = alphas[c] * acc_ref[c] + _dot(v_tile(c[0], blk, n_blk, c[1] * HEAD_DIM), probs[c])

    per_tile = KV_TILE // BLOCK

    def body(j, carry):
        update(TILES_PER_TRIP * per_tile * j, TILES_PER_TRIP * per_tile, None)
        return carry

    lax.fori_loop(0, lax.div(i, TILES_PER_TRIP), body, 0)

    def finalize():
        for sq in seqs:
            for pr in range(B_HEADS // 2):
                acc0, acc1 = acc_ref[sq, 2 * pr], acc_ref[sq, 2 * pr + 1]
                o_t = jnp.concatenate([acc0[:HEAD_DIM] / acc0[HEAD_DIM:HEAD_DIM + 1],
                                       acc1[:HEAD_DIM] / acc1[HEAD_DIM:HEAD_DIM + 1]], axis=0)
                ob_ref[sq, :, pr * LANES:(pr + 1) * LANES] = o_t.T.astype(BF16)

    for left in range(TILES_PER_TRIP):
        @pl.when(lax.rem(i, TILES_PER_TRIP) == left)
        def _():
            mask = jnp.concatenate([jnp.zeros((left * KV_TILE, Q_TILE), F32), causal], axis=0) if left else causal
            update(per_tile * (i - left), (left + 1) * per_tile, mask)
            finalize()


def _attention(sinks, qa, ka, qb, kb, vt, kam, kbm, vtm, tables, batch, seq):
    nq = seq // Q_TILE
    group = batch // SEQS_PER_STEP
    grid = (group, nq)

    def grouped(a):
        return a.reshape((SEQS_PER_STEP, a.shape[0] // SEQS_PER_STEP) + a.shape[1:])

    def q_spec(cols):
        return pl.BlockSpec((SEQS_PER_STEP, Q_TILE, cols), lambda b, i: (0, b * nq + i, 0))

    def kv_spec(cols):
        return pl.BlockSpec((SEQS_PER_STEP, seq, cols), lambda b, i: (0, b, 0))

    in_specs = [
        pl.BlockSpec(memory_space=pltpu.SMEM),
        q_spec(A_WIDTH), kv_spec(A_KEY_COLS), q_spec(B_WIDTH), kv_spec(B_HEADS * LANES),
        pl.BlockSpec((SEQS_PER_STEP, seq // LANES, V_ROWS, LANES), lambda b, i: (0, b, 0, 0)),
    ]
    in_specs += [_const_spec(a.shape) for a in (kam, kbm, vtm) + tuple(tables)]
    out_shape = [jax.ShapeDtypeStruct((SEQS_PER_STEP, group * seq, A_WIDTH), BF16),
                 jax.ShapeDtypeStruct((SEQS_PER_STEP, group * seq, B_WIDTH), BF16)]
    out_specs = [q_spec(A_WIDTH), q_spec(B_WIDTH)]
    scratch = [
        pltpu.VMEM((SEQS_PER_STEP, B_HEADS, 1, Q_TILE), F32),
        pltpu.VMEM((SEQS_PER_STEP, B_HEADS, HEAD_DIM + ONES_ROWS, Q_TILE), F32),
    ]
    oa, ob = pl.pallas_call(
        _attn_kernel,
        grid=grid,
        in_specs=in_specs,
        out_specs=out_specs,
        out_shape=out_shape,
        scratch_shapes=scratch,
        compiler_params=pltpu.CompilerParams(
            dimension_semantics=("arbitrary", "arbitrary"), vmem_limit_bytes=VMEM_LIMIT_BYTES),
        name="attention",
    )(sinks, grouped(qa), grouped(ka), grouped(qb), grouped(kb), grouped(vt), kam, kbm, vtm, *tables)
    return oa.reshape(batch * seq, A_WIDTH), ob.reshape(batch * seq, B_WIDTH)


def _mix_ffn_kernel(h1_ref, oa_ref, ob_ref, ga_ref, gb_ref, wa_ref, wb_ref, wo_ref,
                    n2_ref, w2i_ref, w2o_ref, nf_ref, out_ref):
    rows = h1_ref.shape[0] // ROW_BLOCKS
    blocks = [slice(r * rows, (r + 1) * rows) for r in range(ROW_BLOCKS)]
    yas = [_dot(oa_ref[b, :], wa_ref[...]) for b in blocks]
    ybs = [_dot(ob_ref[b, :], wb_ref[...]) for b in blocks]
    mixed = [(ga_ref[b, :].astype(F32) * ya + gb_ref[b, :].astype(F32) * yb).astype(BF16)
             for b, ya, yb in zip(blocks, yas, ybs)]
    h2s = [h1_ref[b, :] + _dot(mx, wo_ref[...]) for b, mx in zip(blocks, mixed)]
    xns = [_rms(h2, n2_ref[...]).astype(BF16) for h2 in h2s]
    ffns = _swiglu(xns, w2i_ref, w2o_ref)
    for b, h2, ffn in zip(blocks, h2s, ffns):
        out_ref[b, :] = _rms(h2 + 0.5 * ffn, nf_ref[...])


def _mix_ffn(h1, oa, ob, ga, gb, wa, wb, wo, n2, w2i, w2o, nf, tile):
    rows = h1.shape[0]

    def row_spec(cols):
        return pl.BlockSpec((tile, cols), lambda i: (i, 0))

    in_specs = [row_spec(D_MODEL), row_spec(A_WIDTH), row_spec(B_WIDTH), row_spec(D_MODEL), row_spec(D_MODEL)]
    in_specs += [_const_spec(w.shape) for w in (wa, wb, wo, n2, w2i, w2o, nf)]
    return pl.pallas_call(
        _mix_ffn_kernel,
        grid=(rows // tile,),
        in_specs=in_specs,
        out_specs=row_spec(D_MODEL),
        out_shape=jax.ShapeDtypeStruct((rows, D_MODEL), F32),
        compiler_params=pltpu.CompilerParams(
            dimension_semantics=("arbitrary",), vmem_limit_bytes=VMEM_LIMIT_BYTES),
        name="mix_ffn2",
    )(h1, oa, ob, ga, gb, wa, wb, wo, n2, w2i, w2o, nf)


def _prep_proj_weight(w):
    a_kv_width = A_KV_HEADS * HEAD_DIM
    widths = (A_WIDTH, a_kv_width, a_kv_width, B_WIDTH, B_WIDTH, B_WIDTH, B_HEADS, D_MODEL, D_MODEL)
    offsets = np.cumsum((0,) + widths)
    qa, ka, va, qb, kb, vb, wf, g_a, g_b = (w[:, lo:hi] for lo, hi in zip(offsets[:-1], offsets[1:]))
    wf_pad = jnp.pad(wf, ((0, 0), (0, LANES - B_HEADS)))
    wp = jnp.concatenate([qa, ka, wf_pad, qb, kb, g_a, g_b], axis=1).astype(BF16)
    wv = jnp.concatenate([vb, va], axis=1).astype(BF16)
    return wp, wv


def _attention_tables():
    slopes = 2.0 ** -(np.arange(1, A_HEADS + 1, dtype=np.float64))
    k = np.arange(BLOCK)[:, None]
    r = np.arange(BLOCK)[None, :]
    band = np.empty((2, A_KV_HEADS, BLOCK, A_GROUP * BLOCK), np.float32)
    meta = np.empty((A_KV_HEADS, N_META, A_GROUP * BLOCK), np.float32)
    slope_rows = np.empty((A_KV_HEADS, 1, A_GROUP * BLOCK), np.float32)
    mk = np.arange(N_META)[:, None]
    for h in range(A_HEADS):
        g, j = divmod(h, A_GROUP)
        cols = slice(j * BLOCK, (j + 1) * BLOCK)
        dist = np.where(k <= r, r - k, r - k + BLOCK)
        band[1, g, :, cols] = -slopes[h] * dist
        band[0, g, :, cols] = np.where(k <= r, -slopes[h] * dist, NEG)
        meta[g, :, cols] = -slopes[h] * (r + N_META - mk)
        slope_rows[g, :, cols] = slopes[h] * BLOCK
    band = np.where(band > NEG, band * LOG2E, NEG)
    tables = [t.astype(np.float32) for t in (band, meta * LOG2E, slope_rows * LOG2E)]
    return tuple(jnp.asarray(t) for t in tables)


def _bias_lane(h, p):
    return (HEAD_DIM if h % 2 == 0 else 0) + BIAS_PIECES * h + p


def _bias_placement():
    place = np.zeros((BIAS_PIECES, LANES, LANES), np.float32)
    for h in range(B_HEADS):
        for p in range(BIAS_PIECES):
            place[p, h, _bias_lane(h, p)] = 1.0
    return jnp.asarray(place, BF16)


def kernel(x, meta_tokens, ffn1_norm, ffn1_w_in, ffn1_w_out, mix_norm, w_in, b_forget, attn_sinks,
           w_branch_a, w_branch_b, w_out, ffn2_norm, ffn2_w_in, ffn2_w_out, final_norm):
    batch, seq, d = x.shape
    assert d == D_MODEL and seq % TOKEN_TILE == 0 and seq % Q_TILE == 0
    assert batch % SEQS_PER_STEP == 0 and KV_TILE == Q_TILE
    assert ffn1_norm.shape[0] == 1, "single layer"

    n1 = ffn1_norm[0][None].astype(F32)
    nm = mix_norm[0][None].astype(F32)
    n2 = ffn2_norm[0][None].astype(F32)
    nf = final_norm[None].astype(F32)
    w1i, w1o = ffn1_w_in[0].astype(BF16), ffn1_w_out[0].astype(BF16)
    w2i, w2o = ffn2_w_in[0].astype(BF16), ffn2_w_out[0].astype(BF16)
    wp, wv = _prep_proj_weight(w_in[0])
    bfg = jnp.pad(b_forget[0].astype(F32), (0, LANES - B_HEADS))[None]
    wa, wb, wo = w_branch_a[0].astype(BF16), w_branch_b[0].astype(BF16), w_out[0].astype(BF16)
    place = _bias_placement()

    x2d = x.reshape(batch * seq, D_MODEL)
    h1, qa, ka, qb, kb, ga, gb, vt = _ffn_proj(
        x2d, n1, w1i, w1o, nm, wp, wv.T, bfg, place, TOKEN_TILE, seq // TOKEN_TILE, False)
    _, _, kam, _, kbm, _, _, vm = _ffn_proj(
        meta_tokens.astype(F32), n1, w1i, w1o, nm, wp, wv, bfg, place, N_META, 1, True)

    oa, ob = _attention(attn_sinks[0].astype(F32), qa, ka, qb, kb, vt, kam, kbm, vm.T,
                        _attention_tables(), batch, seq)
    out = _mix_ffn(h1, oa, ob, ga, gb, wa, wb, wo, n2, w2i, w2o, nf, TOKEN_TILE)
    return out.reshape(batch, seq, D_MODEL)
```

```python
import jax
import jax.numpy as jnp
import numpy as np
from jax import lax
from jax.experimental import pallas as pl
from jax.experimental.pallas import tpu as pltpu

F32 = jnp.float32
BF16 = jnp.bfloat16

D_MODEL = 1024
N_META = 16
BLOCK = 128
HEAD_DIM = 64
A_HEADS = 8
A_KV_HEADS = 2
A_GROUP = A_HEADS // A_KV_HEADS
B_HEADS = 8
A_WIDTH = A_HEADS * HEAD_DIM
B_WIDTH = B_HEADS * HEAD_DIM
V_ROWS = B_WIDTH + A_KV_HEADS * HEAD_DIM
D_FF = 2816
EPS = 1e-6
NEG = -1e30
LOG2E = 1.4426950408889634
Q_SCALE = HEAD_DIM ** -0.5 * LOG2E

LANES = 128
TOKEN_TILE = 512
ROW_BLOCKS = 2
Q_TILE = 256
KV_TILE = 256
TILES_PER_TRIP = 2
SEQS_PER_STEP = 2
FF_CHUNKS = ((0, 1024), (1024, 1024), (2048, 768))
VMEM_LIMIT_BYTES = 56 * 1024 * 1024
BIAS_PIECES = 3
ONES_ROWS = 16

A_KEY_COLS = 2 * A_KV_HEADS * HEAD_DIM
_PROJ_WIDTHS = (A_WIDTH, 2 * LANES, B_WIDTH, B_WIDTH, D_MODEL, D_MODEL)
_QA, _KF, _QB, _KB, _GA, _GB = ((sum(_PROJ_WIDTHS[:j]), w) for j, w in enumerate(_PROJ_WIDTHS))


def _dot(a, b):
    return jnp.dot(a, b, preferred_element_type=F32)


def _dot_nt(a, b):
    return lax.dot_general(a, b, (((1,), (1,)), ((), ())), preferred_element_type=F32)


def _rms(x, g):
    ms = jnp.mean(x * x, axis=-1, keepdims=True)
    return x * lax.rsqrt(ms + EPS) * g


def _swiglu(xns, w_in_ref, w_out_ref):
    accs = [None] * len(xns)
    for lo, width in FF_CHUNKS:
        gs = [_dot(xn, w_in_ref[:, lo:lo + width]) for xn in xns]
        us = [_dot(xn, w_in_ref[:, D_FF + lo:D_FF + lo + width]) for xn in xns]
        acts = [(g * jax.nn.sigmoid(g) * u).astype(BF16) for g, u in zip(gs, us)]
        parts = [_dot(a, w_out_ref[lo:lo + width, :]) for a in acts]
        accs = [part if acc is None else acc + part for acc, part in zip(accs, parts)]
    return accs


def _split3(x):
    hi = x.astype(BF16)
    r = x - hi.astype(F32)
    mid = r.astype(BF16)
    lo = (r - mid.astype(F32)).astype(BF16)
    return hi, mid, lo


def _ffn_proj_kernel(tiles_per_seq, is_meta,
                     x_ref, n1_ref, w1i_ref, w1o_ref, nm_ref, wp_ref, wv_ref, bf_ref, tri_ref, place_ref,
                     h1_ref, qa_ref, ka_ref, qb_ref, kb_ref, ga_ref, gb_ref, v_ref, carry_ref):
    @pl.when(pl.program_id(0) % tiles_per_seq == 0)
    def _():
        carry_ref[...] = jnp.zeros_like(carry_ref)

    x = x_ref[...]
    tile = x.shape[0]
    xn = _rms(x, n1_ref[...]).astype(BF16)
    h1 = x + 0.5 * _swiglu([xn], w1i_ref, w1o_ref)[0]
    h1_ref[...] = h1
    u = _rms(h1, nm_ref[...]).astype(BF16)

    def proj(lo, width):
        return _dot(u, wp_ref[:, lo:lo + width])

    qa_ref[...] = (proj(*_QA) * Q_SCALE).astype(BF16)
    lane = lax.broadcasted_iota(jnp.int32, (1, LANES), 1)
    low = lane < HEAD_DIM
    kf = proj(*_KF)
    ka = kf[:, :LANES]
    swapped = pltpu.roll(ka, HEAD_DIM, 1)
    ka_ref[...] = jnp.concatenate([jnp.where(low, ka, swapped), jnp.where(low, swapped, ka)], axis=1).astype(BF16)
    qb_ref[...] = (proj(*_QB) * Q_SCALE).astype(BF16)
    kb = proj(*_KB).astype(BF16)

    z = kf[:, LANES:] + bf_ref[...]
    lf = jnp.where(lane < B_HEADS, jnp.minimum(z, 0.0) - jnp.log1p(jnp.exp(-jnp.abs(z))), 0.0)
    tri = tri_ref[...]
    blk = tri.shape[0]
    pieces = _split3(lf)
    offset, parts = carry_ref[...], []
    for r in range(tile // blk):
        local = sum(_dot(tri, piece[r * blk:(r + 1) * blk]) for piece in pieces)
        parts.append(local + offset)
        offset = offset + local[blk - 1:blk, :]
    c = jnp.concatenate(parts, axis=0)
    carry_ref[...] = offset
    bias = ((c[tile - 1:tile, :] - c) if is_meta else -c) * LOG2E
    cb = sum(_dot(piece, place_ref[p]) for p, piece in enumerate(_split3(bias))).astype(BF16)
    for h in range(B_HEADS):
        k_pair = kb[:, (h // 2) * LANES:(h // 2 + 1) * LANES]
        kb_ref[:, h * LANES:(h + 1) * LANES] = jnp.where(low if h % 2 == 0 else ~low, k_pair, cb)

    ga_ref[...] = jax.nn.sigmoid(proj(*_GA)).astype(BF16)
    gb_ref[...] = jax.nn.sigmoid(proj(*_GB)).astype(BF16)

    if is_meta:
        v_ref[...] = _dot(u, wv_ref[...]).astype(BF16)
    else:
        vt = _dot_nt(wv_ref[...], u).astype(BF16)
        for c_blk in range(tile // LANES):
            v_ref[c_blk] = vt[:, c_blk * LANES:(c_blk + 1) * LANES]


def _const_spec(shape):
    zeros = (0,) * len(shape)
    return pl.BlockSpec(shape, lambda *_: zeros, pipeline_mode=pl.Buffered(1))


def _ffn_proj(x2d, n1, w1i, w1o, nm, wp, wv, bfg, place, tile, tiles_per_seq, is_meta):
    rows = x2d.shape[0]
    grid = (rows // tile,)
    scan_rows = min(tile, BLOCK)
    tri = jnp.asarray(np.tril(np.ones((scan_rows, scan_rows), np.float32)), BF16)

    def row_spec(cols):
        return pl.BlockSpec((tile, cols), lambda i: (i, 0))

    out_cols = (D_MODEL, A_WIDTH, A_KEY_COLS, B_WIDTH, B_HEADS * LANES, D_MODEL, D_MODEL)
    out_dtypes = (F32,) + (BF16,) * 6
    out_shape = [jax.ShapeDtypeStruct((rows, c), dt) for c, dt in zip(out_cols, out_dtypes)]
    out_specs = [row_spec(c) for c in out_cols]
    if is_meta:
        out_shape.append(jax.ShapeDtypeStruct((rows, V_ROWS), BF16))
        out_specs.append(row_spec(V_ROWS))
    else:
        out_shape.append(jax.ShapeDtypeStruct((rows // LANES, V_ROWS, LANES), BF16))
        out_specs.append(pl.BlockSpec((tile // LANES, V_ROWS, LANES), lambda i: (i, 0, 0)))
    consts = (n1, w1i, w1o, nm, wp, wv, bfg, tri, place)
    in_specs = [row_spec(D_MODEL)] + [_const_spec(a.shape) for a in consts]
    return pl.pallas_call(
        lambda *refs: _ffn_proj_kernel(tiles_per_seq, is_meta, *refs),
        grid=grid,
        in_specs=in_specs,
        out_specs=out_specs,
        out_shape=out_shape,
        scratch_shapes=[pltpu.VMEM((1, LANES), F32)],
        compiler_params=pltpu.CompilerParams(
            dimension_semantics=("arbitrary",), vmem_limit_bytes=VMEM_LIMIT_BYTES),
        name="ffn1_proj_meta" if is_meta else "ffn1_proj",
    )(x2d, *consts)


def _split_heads(q, low):
    zero = jnp.zeros_like(q)
    return jnp.concatenate([jnp.where(low, q, zero), jnp.where(low, zero, q)], axis=0)


def _attn_kernel(sinks_ref, qa_ref, ka_ref, qb_ref, kb_ref, vt_ref,
                 kam_ref, kbm_ref, vtm_ref, band_ref, mtab_ref, slope_ref,
                 oa_ref, ob_ref, m_ref, acc_ref):
    i = pl.program_id(1)
    low = lax.broadcasted_iota(jnp.int32, (1, LANES), 1) < HEAD_DIM
    sub = Q_TILE // BLOCK
    seqs = range(SEQS_PER_STEP)
    ones_meta = jnp.ones((ONES_ROWS, N_META), BF16)

    def v_tile(sq, blk, n_blk, row0):
        vals = jnp.concatenate([vt_ref[sq, blk + b, row0:row0 + HEAD_DIM, :] for b in range(n_blk)], axis=1)
        return jnp.concatenate([vals, jnp.ones((ONES_ROWS, n_blk * BLOCK), BF16)], axis=0)

    def v_meta(row0):
        return jnp.concatenate([vtm_ref[row0:row0 + HEAD_DIM, :], ones_meta], axis=0)

    own_key = (lax.broadcasted_iota(jnp.int32, (BLOCK, A_GROUP * BLOCK), 0)
               <= (lax.broadcasted_iota(jnp.int32, (BLOCK, A_GROUP * BLOCK), 1) & (BLOCK - 1)))
    units = [(sq, sb, g) for sq in seqs for sb in range(sub) for g in range(A_KV_HEADS)]
    blks, s_bs, s_ms = {}, {}, {}
    for sq, sb, g in units:
        n = i * sub + sb
        blks[sb] = (jnp.maximum(n - 1, 0), n)
        q = qa_ref[sq, sb * BLOCK:(sb + 1) * BLOCK, :]
        qq = jnp.concatenate(
            [_split_heads(q[:, (2 * g + p) * LANES:(2 * g + p + 1) * LANES], low) for p in range(2)], axis=0)
        keys = jnp.concatenate(
            [ka_ref[sq, pl.ds(pl.multiple_of(b * BLOCK, BLOCK), BLOCK), g * LANES:(g + 1) * LANES]
             for b in blks[sb]] + [kam_ref[:, g * LANES:(g + 1) * LANES]], axis=0)
        s2 = _dot_nt(keys, qq)
        s_bs[sq, sb, g] = jnp.where(own_key, s2[BLOCK:2 * BLOCK], s2[:BLOCK]) + band_ref[jnp.minimum(n, 1), g]
        s_ms[sq, sb, g] = s2[2 * BLOCK:] + mtab_ref[g] - n.astype(F32) * slope_ref[g]
    p_bs, p_ms, sink_ps = {}, {}, {}
    for unit in units:
        g = unit[2]
        s_b, s_m = s_bs[unit], s_ms[unit]
        sink = jnp.concatenate(
            [jnp.full((1, BLOCK), sinks_ref[g * A_GROUP + j] * LOG2E, F32) for j in range(A_GROUP)], axis=1)
        m = jnp.maximum(jnp.max(s_b, axis=0, keepdims=True), jnp.max(s_m, axis=0, keepdims=True))
        m = jnp.maximum(m, sink)
        p_bs[unit] = jnp.exp2(s_b - m).astype(BF16)
        p_ms[unit] = jnp.exp2(s_m - m).astype(BF16)
        sink_ps[unit] = jnp.exp2(sink - m)
    for unit in units:
        sq, sb, g = unit
        row0 = B_WIDTH + g * HEAD_DIM
        prev_blk, own_blk = blks[sb]
        p_b = p_bs[unit]
        zero = jnp.zeros_like(p_b)
        o_aug = (_dot(v_tile(sq, own_blk, 1, row0), jnp.where(own_key, p_b, zero))
                 + _dot(v_tile(sq, prev_blk, 1, row0), jnp.where(own_key, zero, p_b))
                 + _dot(v_meta(row0), p_ms[unit]))
        o_t = o_aug[:HEAD_DIM] / (o_aug[HEAD_DIM:HEAD_DIM + 1] + sink_ps[unit])
        for p in range(2):
            pair_t = jnp.concatenate([o_t[:, (2 * p) * BLOCK:(2 * p + 1) * BLOCK],
                                      o_t[:, (2 * p + 1) * BLOCK:(2 * p + 2) * BLOCK]], axis=0)
            oa_ref[sq, sb * BLOCK:(sb + 1) * BLOCK, (2 * g + p) * LANES:(2 * g + p + 1) * LANES] = (
                pair_t.T.astype(BF16))

    lane_row = lax.broadcasted_iota(jnp.int32, (1, LANES), 1)
    chains = [(sq, h) for sq in seqs for h in range(B_HEADS)]

    def head_cols(h):
        return slice(h * LANES, (h + 1) * LANES)

    q_aug = {}
    for sq, h in chains:
        first = _bias_lane(h, 0)
        bias_ones = jnp.where((lane_row >= first) & (lane_row < first + BIAS_PIECES), 1.0, 0.0).astype(BF16)
        q_pair = qb_ref[sq, :, (h // 2) * LANES:(h // 2 + 1) * LANES]
        q_aug[sq, h] = jnp.where(low if h % 2 == 0 else ~low, q_pair, bias_ones)

    row = lax.broadcasted_iota(jnp.int32, (KV_TILE, Q_TILE), 0)
    col = lax.broadcasted_iota(jnp.int32, (KV_TILE, Q_TILE), 1)
    causal = jnp.where(row <= col, 0.0, NEG)

    scores = {c: _dot_nt(kbm_ref[:, head_cols(c[1])], q_aug[c]) for c in chains}
    probs = {}
    for c in chains:
        m0 = jnp.max(scores[c], axis=0, keepdims=True)
        m_ref[c] = m0
        probs[c] = jnp.exp2(scores[c] - m0).astype(BF16)
    for c in chains:
        acc_ref[c] = _dot(v_meta(c[1] * HEAD_DIM), probs[c])

    def update(blk, n_blk, mask):
        k0 = pl.multiple_of(blk * BLOCK, KV_TILE)
        rows = pl.ds(k0, n_blk * BLOCK)
        scores = {c: _dot_nt(kb_ref[c[0], rows, head_cols(c[1])], q_aug[c]) for c in chains}
        probs, alphas = {}, {}
        for c in chains:
            s = scores[c]
            if mask is not None:
                s = s + mask
            m_old = m_ref[c]
            m_new = jnp.maximum(m_old, jnp.max(s, axis=0, keepdims=True))
            m_ref[c] = m_new
            alphas[c] = jnp.exp2(m_old - m_new)
            probs[c] = jnp.exp2(s - m_new).astype(BF16)
        for c in chains:
            acc_ref[c] = alphas[c] * acc_ref[c] + _dot(v_tile(c[0], blk, n_blk, c[1] * HEAD_DIM), probs[c])

    per_tile = KV_TILE // BLOCK

    def body(j, carry):
        update(TILES_PER_TRIP * per_tile * j, TILES_PER_TRIP * per_tile, None)
        return carry

    lax.fori_loop(0, lax.div(i, TILES_PER_TRIP), body, 0)

    def finalize():
        for sq in seqs:
            for pr in range(B_HEADS // 2):
                acc0, acc1 = acc_ref[sq, 2 * pr], acc_ref[sq, 2 * pr + 1]
                o_t = jnp.concatenate([acc0[:HEAD_DIM] / acc0[HEAD_DIM:HEAD_DIM + 1],
                                       acc1[:HEAD_DIM] / acc1[HEAD_DIM:HEAD_DIM + 1]], axis=0)
                ob_ref[sq, :, pr * LANES:(pr + 1) * LANES] = o_t.T.astype(BF16)

    for left in range(TILES_PER_TRIP):
        @pl.when(lax.rem(i, TILES_PER_TRIP) == left)
        def _():
            mask = jnp.concatenate([jnp.zeros((left * KV_TILE, Q_TILE), F32), causal], axis=0) if left else causal
            update(per_tile * (i - left), (left + 1) * per_tile, mask)
            finalize()


def _attention(sinks, qa, ka, qb, kb, vt, kam, kbm, vtm, tables, batch, seq):
    nq = seq // Q_TILE
    group = batch // SEQS_PER_STEP
    grid = (group, nq)

    def grouped(a):
        return a.reshape((SEQS_PER_STEP, a.shape[0] // SEQS_PER_STEP) + a.shape[1:])

    def q_spec(cols):
        return pl.BlockSpec((SEQS_PER_STEP, Q_TILE, cols), lambda b, i: (0, b * nq + i, 0))

    def kv_spec(cols):
        return pl.BlockSpec((SEQS_PER_STEP, seq, cols), lambda b, i: (0, b, 0))

    in_specs = [
        pl.BlockSpec(memory_space=pltpu.SMEM),
        q_spec(A_WIDTH), kv_spec(A_KEY_COLS), q_spec(B_WIDTH), kv_spec(B_HEADS * LANES),
        pl.BlockSpec((SEQS_PER_STEP, seq // LANES, V_ROWS, LANES), lambda b, i: (0, b, 0, 0)),
    ]
    in_specs += [_const_spec(a.shape) for a in (kam, kbm, vtm) + tuple(tables)]
    out_shape = [jax.ShapeDtypeStruct((SEQS_PER_STEP, group * seq, A_WIDTH), BF16),
                 jax.ShapeDtypeStruct((SEQS_PER_STEP, group * seq, B_WIDTH), BF16)]
    out_specs = [q_spec(A_WIDTH), q_spec(B_WIDTH)]
    scratch = [
        pltpu.VMEM((SEQS_PER_STEP, B_HEADS, 1, Q_TILE), F32),
        pltpu.VMEM((SEQS_PER_STEP, B_HEADS, HEAD_DIM + ONES_ROWS, Q_TILE), F32),
    ]
    oa, ob = pl.pallas_call(
        _attn_kernel,
        grid=grid,
        in_specs=in_specs,
        out_specs=out_specs,
        out_shape=out_shape,
        scratch_shapes=scratch,
        compiler_params=pltpu.CompilerParams(
            dimension_semantics=("arbitrary", "arbitrary"), vmem_limit_bytes=VMEM_LIMIT_BYTES),
        name="attention",
    )(sinks, grouped(qa), grouped(ka), grouped(qb), grouped(kb), grouped(vt), kam, kbm, vtm, *tables)
    return oa.reshape(batch * seq, A_WIDTH), ob.reshape(batch * seq, B_WIDTH)


def _mix_ffn_kernel(h1_ref, oa_ref, ob_ref, ga_ref, gb_ref, wa_ref, wb_ref, wo_ref,
                    n2_ref, w2i_ref, w2o_ref, nf_ref, out_ref):
    rows = h1_ref.shape[0] // ROW_BLOCKS
    blocks = [slice(r * rows, (r + 1) * rows) for r in range(ROW_BLOCKS)]
    yas = [_dot(oa_ref[b, :], wa_ref[...]) for b in blocks]
    ybs = [_dot(ob_ref[b, :], wb_ref[...]) for b in blocks]
    mixed = [(ga_ref[b, :].astype(F32) * ya + gb_ref[b, :].astype(F32) * yb).astype(BF16)
             for b, ya, yb in zip(blocks, yas, ybs)]
    h2s = [h1_ref[b, :] + _dot(mx, wo_ref[...]) for b, mx in zip(blocks, mixed)]
    xns = [_rms(h2, n2_ref[...]).astype(BF16) for h2 in h2s]
    ffns = _swiglu(xns, w2i_ref, w2o_ref)
    for b, h2, ffn in zip(blocks, h2s, ffns):
        out_ref[b, :] = _rms(h2 + 0.5 * ffn, nf_ref[...])


def _mix_ffn(h1, oa, ob, ga, gb, wa, wb, wo, n2, w2i, w2o, nf, tile):
    rows = h1.shape[0]

    def row_spec(cols):
        return pl.BlockSpec((tile, cols), lambda i: (i, 0))

    in_specs = [row_spec(D_MODEL), row_spec(A_WIDTH), row_spec(B_WIDTH), row_spec(D_MODEL), row_spec(D_MODEL)]
    in_specs += [_const_spec(w.shape) for w in (wa, wb, wo, n2, w2i, w2o, nf)]
    return pl.pallas_call(
        _mix_ffn_kernel,
        grid=(rows // tile,),
        in_specs=in_specs,
        out_specs=row_spec(D_MODEL),
        out_shape=jax.ShapeDtypeStruct((rows, D_MODEL), F32),
        compiler_params=pltpu.CompilerParams(
            dimension_semantics=("arbitrary",), vmem_limit_bytes=VMEM_LIMIT_BYTES),
        name="mix_ffn2",
    )(h1, oa, ob, ga, gb, wa, wb, wo, n2, w2i, w2o, nf)


def _prep_proj_weight(w):
    a_kv_width = A_KV_HEADS * HEAD_DIM
    widths = (A_WIDTH, a_kv_width, a_kv_width, B_WIDTH, B_WIDTH, B_WIDTH, B_HEADS, D_MODEL, D_MODEL)
    offsets = np.cumsum((0,) + widths)
    qa, ka, va, qb, kb, vb, wf, g_a, g_b = (w[:, lo:hi] for lo, hi in zip(offsets[:-1], offsets[1:]))
    wf_pad = jnp.pad(wf, ((0, 0), (0, LANES - B_HEADS)))
    wp = jnp.concatenate([qa, ka, wf_pad, qb, kb, g_a, g_b], axis=1).astype(BF16)
    wv = jnp.concatenate([vb, va], axis=1).astype(BF16)
    return wp, wv


def _attention_tables():
    slopes = 2.0 ** -(np.arange(1, A_HEADS + 1, dtype=np.float64))
    k = np.arange(BLOCK)[:, None]
    r = np.arange(BLOCK)[None, :]
    band = np.empty((2, A_KV_HEADS, BLOCK, A_GROUP * BLOCK), np.float32)
    meta = np.empty((A_KV_HEADS, N_META, A_GROUP * BLOCK), np.float32)
    slope_rows = np.empty((A_KV_HEADS, 1, A_GROUP * BLOCK), np.float32)
    mk = np.arange(N_META)[:, None]
    for h in range(A_HEADS):
        g, j = divmod(h, A_GROUP)
        cols = slice(j * BLOCK, (j + 1) * BLOCK)
        dist = np.where(k <= r, r - k, r - k + BLOCK)
        band[1, g, :, cols] = -slopes[h] * dist
        band[0, g, :, cols] = np.where(k <= r, -slopes[h] * dist, NEG)
        meta[g, :, cols] = -slopes[h] * (r + N_META - mk)
        slope_rows[g, :, cols] = slopes[h] * BLOCK
    band = np.where(band > NEG, band * LOG2E, NEG)
    tables = [t.astype(np.float32) for t in (band, meta * LOG2E, slope_rows * LOG2E)]
    return tuple(jnp.asarray(t) for t in tables)


def _bias_lane(h, p):
    return (HEAD_DIM if h % 2 == 0 else 0) + BIAS_PIECES * h + p


def _bias_placement():
    place = np.zeros((BIAS_PIECES, LANES, LANES), np.float32)
    for h in range(B_HEADS):
        for p in range(BIAS_PIECES):
            place[p, h, _bias_lane(h, p)] = 1.0
    return jnp.asarray(place, BF16)


def kernel(x, meta_tokens, ffn1_norm, ffn1_w_in, ffn1_w_out, mix_norm, w_in, b_forget, attn_sinks,
           w_branch_a, w_branch_b, w_out, ffn2_norm, ffn2_w_in, ffn2_w_out, final_norm):
    batch, seq, d = x.shape
    assert d == D_MODEL and seq % TOKEN_TILE == 0 and seq % Q_TILE == 0
    assert batch % SEQS_PER_STEP == 0 and KV_TILE == Q_TILE
    assert ffn1_norm.shape[0] == 1, "single layer"

    n1 = ffn1_norm[0][None].astype(F32)
    nm = mix_norm[0][None].astype(F32)
    n2 = ffn2_norm[0][None].astype(F32)
    nf = final_norm[None].astype(F32)
    w1i, w1o = ffn1_w_in[0].astype(BF16), ffn1_w_out[0].astype(BF16)
    w2i, w2o = ffn2_w_in[0].astype(BF16), ffn2_w_out[0].astype(BF16)
    wp, wv = _prep_proj_weight(w_in[0])
    bfg = jnp.pad(b_forget[0].astype(F32), (0, LANES - B_HEADS))[None]
    wa, wb, wo = w_branch_a[0].astype(BF16), w_branch_b[0].astype(BF16), w_out[0].astype(BF16)
    place = _bias_placement()

    x2d = x.reshape(batch * seq, D_MODEL)
    h1, qa, ka, qb, kb, ga, gb, vt = _ffn_proj(
        x2d, n1, w1i, w1o, nm, wp, wv.T, bfg, place, TOKEN_TILE, seq // TOKEN_TILE, False)
    _, _, kam, _, kbm, _, _, vm = _ffn_proj(
        meta_tokens.astype(F32), n1, w1i, w1o, nm, wp, wv, bfg, place, N_META, 1, True)

    oa, ob = _attention(attn_sinks[0].astype(F32), qa, ka, qb, kb, vt, kam, kbm, vm.T,
                        _attention_tables(), batch, seq)
    out = _mix_ffn(h1, oa, ob, ga, gb, wa, wb, wo, n2, w2i, w2o, nf, TOKEN_TILE)
    return out.reshape(batch, seq, D_MODEL)
```
